```python
import math
import jax
import jax.numpy as jnp
from jax import lax
import numpy as np

D_MODEL = 1024
BATCH = 1
SEQ = 16384
DEPTH = 2
DEC_BATCH = 128
DEC_SEQ = 8
PAST_LEN = 16384
PAGE_SIZE = 128

N_EVEN = (DEPTH + 1) // 2
N_ODD = DEPTH // 2
EPS = 1e-6
LRU_WIDTH = D_MODEL
LRU_HEADS = 16
LRU_BLOCK = LRU_WIDTH // LRU_HEADS
LRU_C = 8.0
CONV_W = 4
SWA_HEADS = 8
SWA_KV_HEADS = 2
SWA_GROUP = SWA_HEADS // SWA_KV_HEADS
SWA_HEAD_DIM = 64
WINDOW = 128
SWA_SCALE = 1.0 / math.sqrt(SWA_HEAD_DIM)
SSD_INNER = D_MODEL
SSD_HEAD_DIM = 64
SSD_HEADS = SSD_INNER // SSD_HEAD_DIM
SSD_GROUPS = 2
SSD_STATE = 128
SSD_CONV_DIM = SSD_INNER + 2 * SSD_GROUPS * SSD_STATE
SSD_CHUNK = 128
MLA_HEADS = 8
Q_LORA = 256
KV_LORA = 256
QK_NOPE = 64
QK_ROPE = 32
V_HEAD = 64
ROPE_THETA = 10000.0
Q_BLOCK = 128
MLA_SCALE = 1.0 / math.sqrt(QK_NOPE + QK_ROPE)
D_FF = 2816
FFN_CONV = 3

EVEN_IN = 2 * LRU_WIDTH + (SWA_HEADS + 2 * SWA_KV_HEADS) * SWA_HEAD_DIM
EVEN_MIX = LRU_WIDTH + SWA_HEADS * SWA_HEAD_DIM
ODD_IN = SSD_INNER + SSD_CONV_DIM + SSD_HEADS + Q_LORA + KV_LORA + QK_ROPE
ODD_MIX = SSD_INNER + MLA_HEADS * V_HEAD
F32 = jnp.float32

kernel_name = 'hybrid_rglru_swa_ssd_mla_decoder_step'


def rmsnorm(x, g):
    xf = x.astype(F32)
    y = xf * lax.rsqrt(jnp.mean(xf * xf, axis=-1, keepdims=True) + EPS)
    return (y * g.astype(F32)).astype(x.dtype)


def split_last(u, sizes):
    bounds = [int(b) for b in np.cumsum(sizes)[:-1]]
    return jnp.split(u, bounds, axis=-1)


def causal_dwconv(x_ext, w, b):
    k_w = w.shape[0]
    length = x_ext.shape[1] - k_w + 1
    y = b
    for k in range(k_w):
        y = y + x_ext[:, k:k + length] * w[k]
    return y


def rope(x, pos):
    half = x.shape[-1] // 2
    inv = jnp.exp(-(math.log(ROPE_THETA) / half) * jnp.arange(half, dtype=F32))
    ang = pos.astype(F32)[:, None] * inv[None, :]
    cos = jnp.cos(ang)[:, None, :]
    sin = jnp.sin(ang)[:, None, :]
    xf = x.astype(F32)
    x1, x2 = xf[..., :half], xf[..., half:]
    return jnp.concatenate([x1 * cos - x2 * sin, x1 * sin + x2 * cos], axis=-1).astype(x.dtype)


def linear_scan(a, b, h0):
    b = b.at[:, 0].add(a[:, 0] * h0)
    def combine(left, right):
        a1, b1 = left
        a2, b2 = right
        return a1 * a2, a2 * b1 + b2
    _, h = lax.associative_scan(combine, (a, b), axis=1)
    return h


def rg_lru(xc, wa, ba, wx, bx, lam, h0):
    bsz, length, width = xc.shape
    xf = xc.astype(F32)
    xh = xf.reshape(bsz, length, LRU_HEADS, LRU_BLOCK)
    r = jax.nn.sigmoid(jnp.einsum('blhi,hij->blhj', xh, wa.astype(F32)).reshape(bsz, length, width) + ba.astype(F32))
    ig = jax.nn.sigmoid(jnp.einsum('blhi,hij->blhj', xh, wx.astype(F32)).reshape(bsz, length, width) + bx.astype(F32))
    log_a = -LRU_C * r * jax.nn.softplus(-lam.astype(F32))
    a = jnp.exp(log_a)
    b = jnp.sqrt(-jnp.expm1(2.0 * log_a)) * (ig * xf)
    hs = linear_scan(a, b, h0.astype(F32))
    return hs.astype(xc.dtype), hs[:, -1].astype(xc.dtype)


def sink_attention(q, k, v, mask, sink):
    s = jnp.einsum('...qkgd,...skd->...kgqs', q, k).astype(F32) * SWA_SCALE
    s = jnp.where(mask, s, -jnp.inf)
    sk = sink.astype(F32).reshape(SWA_KV_HEADS, SWA_GROUP, 1, 1)
    m = jnp.maximum(jnp.max(s, axis=-1, keepdims=True), sk)
    p = jnp.exp(s - m)
    probs = p / (jnp.sum(p, axis=-1, keepdims=True) + jnp.exp(sk - m))
    return jnp.einsum('...kgqs,...skd->...qkgd', probs.astype(v.dtype), v)


def swa_prompt(q, k, v, sink):
    bsz, length = q.shape[:2]
    nb = length // WINDOW
    qb = q.reshape(bsz, nb, WINDOW, SWA_KV_HEADS, SWA_GROUP, SWA_HEAD_DIM)
    kb = k.reshape(bsz, nb, WINDOW, SWA_KV_HEADS, SWA_HEAD_DIM)
    vb = v.reshape(bsz, nb, WINDOW, SWA_KV_HEADS, SWA_HEAD_DIM)
    pad = ((0, 0), (1, 0), (0, 0), (0, 0), (0, 0))
    kk = jnp.concatenate([jnp.pad(kb[:, :-1], pad), kb], axis=2)
    vv = jnp.concatenate([jnp.pad(vb[:, :-1], pad), vb], axis=2)
    qi = jnp.arange(WINDOW)[:, None] + WINDOW
    sj = jnp.arange(2 * WINDOW)[None, :]
    dist = qi - sj
    band = (dist >= 0) & (dist <= WINDOW)
    valid = (jnp.arange(nb)[:, None, None] > 0) | (sj[None] >= WINDOW)
    mask = (band[None] & valid)[None, :, None, None]
    out = sink_attention(qb, kk, vv, mask, sink)
    return out.reshape(bsz, length, SWA_HEADS * SWA_HEAD_DIM)


def swa_sample(q, k, v, k_buf, v_buf, sink):
    s_len = q.shape[1]
    kk = jnp.concatenate([k_buf.astype(k.dtype), k], axis=1)
    vv = jnp.concatenate([v_buf.astype(v.dtype), v], axis=1)
    dist = (jnp.arange(s_len)[:, None] + WINDOW) - jnp.arange(WINDOW + s_len)[None, :]
    mask = (dist >= 0) & (dist <= WINDOW)
    out = sink_attention(q, kk, vv, mask, sink)
    return out, kk[:, -WINDOW:], vv[:, -WINDOW:]


def segsum(x):
    t = x.shape[-1]
    xx = jnp.broadcast_to(x[..., None], x.shape + (t,))
    strict = jnp.tril(jnp.ones((t, t), bool), -1)
    ss = jnp.cumsum(jnp.where(strict, xx, 0.0), axis=-2)
    return jnp.where(jnp.tril(jnp.ones((t, t), bool)), ss, -jnp.inf)


def ssd_scan(x, dt, a, bm, cm, h0, chunk):
    bsz, length, nh, hp = x.shape
    n = bm.shape[-1]
    rep = nh // bm.shape[2]
    nc = length // chunk
    xd = (x.astype(F32) * dt[..., None]).reshape(bsz, nc, chunk, nh, hp)
    da = (dt * a).reshape(bsz, nc, chunk, nh).transpose(0, 3, 1, 2)
    bh = jnp.repeat(bm.astype(F32), rep, axis=2).reshape(bsz, nc, chunk, nh, n)
    ch = jnp.repeat(cm.astype(F32), rep, axis=2).reshape(bsz, nc, chunk, nh, n)
    cs = jnp.cumsum(da, axis=-1)
    lmat = jnp.exp(segsum(da))
    y_diag = jnp.einsum('bhcls,bcshp->bclhp', jnp.einsum('bclhn,bcshn->bhcls', ch, bh) * lmat, xd)
    decay_states = jnp.exp(cs[..., -1:] - cs)
    states = jnp.einsum('bclhn,bhcl,bclhp->bchpn', bh, decay_states, xd)
    states = jnp.concatenate([h0.astype(F32)[:, None], states], axis=1)
    decay_chunk = jnp.exp(segsum(jnp.pad(cs[..., -1], ((0, 0), (0, 0), (1, 0)))))
    states = jnp.einsum('bhzc,bchpn->bzhpn', decay_chunk, states)
    y_off = jnp.einsum('bclhn,bchpn,bhcl->bclhp', ch, states[:, :-1], jnp.exp(cs))
    return (y_diag + y_off).reshape(bsz, length, nh, hp), states[:, -1]


def mla_prompt(q_nope, q_rope, ckv, kr, w_uk, w_uv):
    bsz, length = q_nope.shape[:2]
    nb = length // Q_BLOCK
    k_nope = jnp.einsum('blc,chd->blhd', ckv, w_uk)
    v = jnp.einsum('blc,chd->blhd', ckv, w_uv)
    k_pos = jnp.arange(length)

    def block(args):
        qn, qr, qp = args
        s = (jnp.einsum('bqhd,bkhd->bhqk', qn, k_nope).astype(F32)
             + jnp.einsum('bqhr,bkr->bhqk', qr, kr).astype(F32)) * MLA_SCALE
        s = jnp.where(qp[:, None] >= k_pos[None, :], s, -jnp.inf)
        p = jax.nn.softmax(s, axis=-1)
        return jnp.einsum('bhqk,bkhd->bqhd', p.astype(v.dtype), v)

    def to_blocks(t):
        return t.reshape((bsz, nb, Q_BLOCK) + t.shape[2:]).swapaxes(0, 1)

    out = lax.map(block, (to_blocks(q_nope), to_blocks(q_rope), k_pos.reshape(nb, Q_BLOCK)))
    return out.swapaxes(0, 1).reshape(bsz, length, MLA_HEADS * V_HEAD)


def mla_sample(q_nope, q_rope, ckv, kr, w_uk, w_uv, cache_lat, cache_rope, page_table, li):
    bsz, s_len = q_nope.shape[:2]
    q_lat = jnp.einsum('bqhd,chd->bqhc', q_nope, w_uk).astype(F32)
    q_r = q_rope.astype(F32)

    def scores(lat, rp):
        return (jnp.einsum('bqhc,bkc->bhqk', q_lat, lat.astype(F32))
                + jnp.einsum('bqhr,bkr->bhqk', q_r, rp.astype(F32))) * MLA_SCALE

    def update(carry, s, lat):
        m, l, acc = carry
        m_new = jnp.maximum(m, jnp.max(s, axis=-1))
        corr = jnp.exp(m - m_new)
        p = jnp.exp(s - m_new[..., None])
        l = l * corr + jnp.sum(p, axis=-1)
        acc = acc * corr[..., None] + jnp.einsum('bhqk,bkc->bhqc', p, lat.astype(F32))
        return (m_new, l, acc)

    def step(carry, pages):
        lat = cache_lat[li, pages]
        rp = cache_rope[li, pages]
        return update(carry, scores(lat, rp), lat), None

    init = (jnp.full((bsz, MLA_HEADS, s_len), -jnp.inf, F32),
            jnp.zeros((bsz, MLA_HEADS, s_len), F32),
            jnp.zeros((bsz, MLA_HEADS, s_len, KV_LORA), F32))
    carry, _ = lax.scan(step, init, page_table.T)
    s_loc = jnp.where(jnp.tril(jnp.ones((s_len, s_len), bool)), scores(ckv, kr), -jnp.inf)
    _, l, acc = update(carry, s_loc, ckv)
    out = jnp.einsum('bhqc,chd->bqhd', acc / l[..., None], w_uv.astype(F32))
    return out.reshape(bsz, s_len, MLA_HEADS * V_HEAD).astype(q_nope.dtype)


def even_mixer(h, P, i, conv_hist, h0, kv_buf):
    bsz, length, _ = h.shape
    u = h @ P['even_w_in'][i]
    xr, gate, q, k, v = split_last(u, [LRU_WIDTH, LRU_WIDTH, SWA_HEADS * SWA_HEAD_DIM,
                                       SWA_KV_HEADS * SWA_HEAD_DIM, SWA_KV_HEADS * SWA_HEAD_DIM])
    x_ext = jnp.concatenate([conv_hist.astype(xr.dtype), xr], axis=1)
    new_conv = x_ext[:, -(CONV_W - 1):]
    xc = causal_dwconv(x_ext, P['lru_conv_w'][i], P['lru_conv_b'][i])
    lru, h_last = rg_lru(xc, P['lru_wa'][i], P['lru_ba'][i], P['lru_wx'][i], P['lru_bx'][i],
                         P['lru_lambda'][i], h0)
    rec = lru * jax.nn.gelu(gate)
    q = q.reshape(bsz, length, SWA_KV_HEADS, SWA_GROUP, SWA_HEAD_DIM)
    k = k.reshape(bsz, length, SWA_KV_HEADS, SWA_HEAD_DIM)
    v = v.reshape(bsz, length, SWA_KV_HEADS, SWA_HEAD_DIM)
    sink = P['swa_sink'][i]
    if kv_buf is None:
        att = swa_prompt(q, k, v, sink)
        new_k, new_v = k[:, -WINDOW:], v[:, -WINDOW:]
    else:
        att, new_k, new_v = swa_sample(q, k, v, kv_buf[0], kv_buf[1], sink)
        att = att.reshape(bsz, length, SWA_HEADS * SWA_HEAD_DIM)
    out = jnp.concatenate([rec, att.astype(rec.dtype)], axis=-1) @ P['even_w_out'][i]
    return out, (new_conv, h_last, new_k, new_v)


def odd_mixer(h, pos, P, i, conv_hist, s0, paged):
    bsz, length, _ = h.shape
    u = h @ P['odd_w_in'][i]
    z, xbc, dt, cq, ckv, kr = split_last(u, [SSD_INNER, SSD_CONV_DIM, SSD_HEADS, Q_LORA, KV_LORA, QK_ROPE])
    x_ext = jnp.concatenate([conv_hist.astype(xbc.dtype), xbc], axis=1)
    new_conv = x_ext[:, -(CONV_W - 1):]
    xbc = jax.nn.silu(causal_dwconv(x_ext, P['ssd_conv_w'][i], P['ssd_conv_b'][i]))
    xs, bm, cm = split_last(xbc, [SSD_INNER, SSD_GROUPS * SSD_STATE, SSD_GROUPS * SSD_STATE])
    xs = xs.reshape(bsz, length, SSD_HEADS, SSD_HEAD_DIM)
    bm = bm.reshape(bsz, length, SSD_GROUPS, SSD_STATE)
    cm = cm.reshape(bsz, length, SSD_GROUPS, SSD_STATE)
    dt = jax.nn.softplus(dt.astype(F32) + P['ssd_dt_bias'][i].astype(F32))
    a = -jnp.exp(P['ssd_a_log'][i].astype(F32))
    chunk = SSD_CHUNK if paged is None else length
    y, s_new = ssd_scan(xs, dt, a, bm, cm, s0, chunk)
    y = y + P['ssd_d'][i].astype(F32)[:, None] * xs.astype(F32)
    y = y.reshape(bsz, length, SSD_INNER) * jax.nn.silu(z.astype(F32))
    y = rmsnorm(y.reshape(bsz, length, SSD_GROUPS, SSD_INNER // SSD_GROUPS),
                P['ssd_norm'][i].reshape(SSD_GROUPS, -1)).reshape(bsz, length, SSD_INNER).astype(h.dtype)
    qf = (rmsnorm(cq, P['mla_q_norm'][i]) @ P['mla_w_uq'][i]).reshape(bsz, length, MLA_HEADS, QK_NOPE + QK_ROPE)
    q_nope = qf[..., :QK_NOPE]
    q_rope = rope(qf[..., QK_NOPE:], pos)
    ckv = rmsnorm(ckv, P['mla_kv_norm'][i])
    kr = rope(kr[:, :, None, :], pos)[:, :, 0]
    if paged is None:
        att = mla_prompt(q_nope, q_rope, ckv, kr, P['mla_w_uk'][i], P['mla_w_uv'][i])
    else:
        att = mla_sample(q_nope, q_rope, ckv, kr, P['mla_w_uk'][i], P['mla_w_uv'][i],
                         paged[0], paged[1], paged[2], i)
    out = jnp.concatenate([y, att.astype(y.dtype)], axis=-1) @ P['odd_w_out'][i]
    return out, (new_conv, s_new.astype(h.dtype), ckv, kr)


def conv_ffn(h, P, l, hist):
    g = h @ P['ffn_w_gate'][l]
    u = h @ P['ffn_w_up'][l]
    g_ext = jnp.concatenate([hist.astype(g.dtype), g], axis=1)
    new_hist = g_ext[:, -(FFN_CONV - 1):]
    g = causal_dwconv(g_ext, P['ffn_conv_w'][l], P['ffn_conv_b'][l])
    return (jax.nn.silu(g) * u) @ P['ffn_w_down'][l], new_hist


def trunk(x, pos, P, st, sample):
    bsz = x.shape[0]
    dty = x.dtype
    outs = {'lru_conv': [], 'lru_h': [], 'swa_k': [], 'swa_v': [], 'ssd_conv': [], 'ssd': [],
            'mla_latent': [], 'mla_rope': [], 'ffn_conv': []}
    for l in range(DEPTH):
        h = rmsnorm(x, P['mix_norm'][l])
        i = l // 2
        if l % 2 == 0:
            if sample:
                conv_hist, h0 = st['lru_conv'][i], st['lru_h'][i]
                kv = (st['swa_k'][i], st['swa_v'][i])
            else:
                conv_hist = jnp.zeros((bsz, CONV_W - 1, LRU_WIDTH), dty)
                h0 = jnp.zeros((bsz, LRU_WIDTH), dty)
                kv = None
            mix, (c, hl, kk, vv) = even_mixer(h, P, i, conv_hist, h0, kv)
            outs['lru_conv'].append(c)
            outs['lru_h'].append(hl)
            outs['swa_k'].append(kk)
            outs['swa_v'].append(vv)
        else:
            if sample:
                conv_hist, s0 = st['ssd_conv'][i], st['ssd'][i]
                paged = (st['mla_latent'], st['mla_rope'], st['page_table'])
            else:
                conv_hist = jnp.zeros((bsz, CONV_W - 1, SSD_CONV_DIM), dty)
                s0 = jnp.zeros((bsz, SSD_HEADS, SSD_HEAD_DIM, SSD_STATE), dty)
                paged = None
            mix, (c, sn, lat, rp) = odd_mixer(h, pos, P, i, conv_hist, s0, paged)
            outs['ssd_conv'].append(c)
            outs['ssd'].append(sn)
            outs['mla_latent'].append(lat)
            outs['mla_rope'].append(rp)
        x = x + mix
        h = rmsnorm(x, P['ffn_norm'][l])
        fh = st['ffn_conv'][l] if sample else jnp.zeros((bsz, FFN_CONV - 1, D_FF), dty)
        f, fc = conv_ffn(h, P, l, fh)
        outs['ffn_conv'].append(fc)
        x = x + f
    y = rmsnorm(x, P['final_norm'])
    return y, {name: jnp.stack(vals) for name, vals in outs.items()}


def setup_inputs(seed: int = 0) -> dict:
    key = jax.random.key(seed)
    ks = iter(jax.random.split(key, 64))

    def nrm(shape, scale):
        return jax.random.normal(next(ks), shape, F32) * scale

    def uni(shape, lo, hi):
        return jax.random.uniform(next(ks), shape, F32, lo, hi)

    n_pages = PAST_LEN // PAGE_SIZE
    n_pool = (5 * DEC_BATCH * n_pages) // 4
    page_table = jax.random.permutation(next(ks), n_pool)[:DEC_BATCH * n_pages].reshape(DEC_BATCH, n_pages).astype(jnp.int32)
    a0 = uni((N_EVEN, LRU_WIDTH), 0.9, 0.999)
    s = a0 ** (1.0 / LRU_C)
    lru_lambda = jnp.log(s) - jnp.log1p(-s)
    dt0 = jnp.exp(uni((N_ODD, SSD_HEADS), math.log(1e-3), math.log(1e-1)))
    ssd_dt_bias = dt0 + jnp.log(-jnp.expm1(-dt0))
    ssd_a_log = jnp.log(uni((N_ODD, SSD_HEADS), 1.0, 16.0))
    return {
        'x_prompt': nrm((BATCH, SEQ, D_MODEL), 1.0),
        'x_sample': nrm((DEC_BATCH, DEC_SEQ, D_MODEL), 1.0),
        'state_lru_conv': nrm((N_EVEN, DEC_BATCH, CONV_W - 1, LRU_WIDTH), 1.0),
        'state_lru_h': nrm((N_EVEN, DEC_BATCH, LRU_WIDTH), 0.5),
        'cache_swa_k': nrm((N_EVEN, DEC_BATCH, WINDOW, SWA_KV_HEADS, SWA_HEAD_DIM), 1.0),
        'cache_swa_v': nrm((N_EVEN, DEC_BATCH, WINDOW, SWA_KV_HEADS, SWA_HEAD_DIM), 1.0),
        'state_ssd_conv': nrm((N_ODD, DEC_BATCH, CONV_W - 1, SSD_CONV_DIM), 1.0),
        'state_ssd': nrm((N_ODD, DEC_BATCH, SSD_HEADS, SSD_HEAD_DIM, SSD_STATE), 0.1),
        'cache_mla_latent': nrm((N_ODD, n_pool, PAGE_SIZE, KV_LORA), 1.0),
        'cache_mla_rope': nrm((N_ODD, n_pool, PAGE_SIZE, QK_ROPE), 1.0),
        'page_table': page_table,
        'state_ffn_conv': nrm((DEPTH, DEC_BATCH, FFN_CONV - 1, D_FF), 1.0),
        'mix_norm': 1.0 + nrm((DEPTH, D_MODEL), 0.02),
        'ffn_norm': 1.0 + nrm((DEPTH, D_MODEL), 0.02),
        'final_norm': 1.0 + nrm((D_MODEL,), 0.02),
        'even_w_in': nrm((N_EVEN, D_MODEL, EVEN_IN), D_MODEL ** -0.5),
        'lru_conv_w': nrm((N_EVEN, CONV_W, LRU_WIDTH), CONV_W ** -0.5),
        'lru_conv_b': nrm((N_EVEN, LRU_WIDTH), 0.02),
        'lru_wa': nrm((N_EVEN, LRU_HEADS, LRU_BLOCK, LRU_BLOCK), LRU_BLOCK ** -0.5),
        'lru_ba': nrm((N_EVEN, LRU_WIDTH), 0.02),
        'lru_wx': nrm((N_EVEN, LRU_HEADS, LRU_BLOCK, LRU_BLOCK), LRU_BLOCK ** -0.5),
        'lru_bx': nrm((N_EVEN, LRU_WIDTH), 0.02),
        'lru_lambda': lru_lambda,
        'swa_sink': nrm((N_EVEN, SWA_HEADS), 0.5),
        'even_w_out': nrm((N_EVEN, EVEN_MIX, D_MODEL), EVEN_MIX ** -0.5),
        'odd_w_in': nrm((N_ODD, D_MODEL, ODD_IN), D_MODEL ** -0.5),
        'ssd_conv_w': nrm((N_ODD, CONV_W, SSD_CONV_DIM), CONV_W ** -0.5),
        'ssd_conv_b': nrm((N_ODD, SSD_CONV_DIM), 0.02),
        'ssd_dt_bias': ssd_dt_bias,
        'ssd_a_log': ssd_a_log,
        'ssd_d': 1.0 + nrm((N_ODD, SSD_HEADS), 0.1),
        'ssd_norm': 1.0 + nrm((N_ODD, SSD_INNER), 0.02),
        'mla_q_norm': 1.0 + nrm((N_ODD, Q_LORA), 0.02),
        'mla_w_uq': nrm((N_ODD, Q_LORA, MLA_HEADS * (QK_NOPE + QK_ROPE)), Q_LORA ** -0.5),
        'mla_kv_norm': 1.0 + nrm((N_ODD, KV_LORA), 0.02),
        'mla_w_uk': nrm((N_ODD, KV_LORA, MLA_HEADS, QK_NOPE), KV_LORA ** -0.5),
        'mla_w_uv': nrm((N_ODD, KV_LORA, MLA_HEADS, V_HEAD), KV_LORA ** -0.5),
        'odd_w_out': nrm((N_ODD, ODD_MIX, D_MODEL), ODD_MIX ** -0.5),
        'ffn_w_gate': nrm((DEPTH, D_MODEL, D_FF), D_MODEL ** -0.5),
        'ffn_w_up': nrm((DEPTH, D_MODEL, D_FF), D_MODEL ** -0.5),
        'ffn_conv_w': nrm((DEPTH, FFN_CONV, D_FF), FFN_CONV ** -0.5),
        'ffn_conv_b': nrm((DEPTH, D_FF), 0.02),
        'ffn_w_down': nrm((DEPTH, D_FF, D_MODEL), D_FF ** -0.5),
    }


def reference(x_prompt, x_sample, state_lru_conv, state_lru_h, cache_swa_k, cache_swa_v,
              state_ssd_conv, state_ssd, cache_mla_latent, cache_mla_rope, page_table, state_ffn_conv,
              mix_norm, ffn_norm, final_norm, even_w_in, lru_conv_w, lru_conv_b, lru_wa, lru_ba,
              lru_wx, lru_bx, lru_lambda, swa_sink, even_w_out, odd_w_in, ssd_conv_w, ssd_conv_b,
              ssd_dt_bias, ssd_a_log, ssd_d, ssd_norm, mla_q_norm, mla_w_uq, mla_kv_norm, mla_w_uk,
              mla_w_uv, odd_w_out, ffn_w_gate, ffn_w_up, ffn_conv_w, ffn_conv_b, ffn_w_down):
    P = dict(mix_norm=mix_norm, ffn_norm=ffn_norm, final_norm=final_norm, even_w_in=even_w_in,
             lru_conv_w=lru_conv_w, lru_conv_b=lru_conv_b, lru_wa=lru_wa, lru_ba=lru_ba, lru_wx=lru_wx,
             lru_bx=lru_bx, lru_lambda=lru_lambda, swa_sink=swa_sink, even_w_out=even_w_out,
             odd_w_in=odd_w_in, ssd_conv_w=ssd_conv_w, ssd_conv_b=ssd_conv_b, ssd_dt_bias=ssd_dt_bias,
             ssd_a_log=ssd_a_log, ssd_d=ssd_d, ssd_norm=ssd_norm, mla_q_norm=mla_q_norm,
             mla_w_uq=mla_w_uq, mla_kv_norm=mla_kv_norm, mla_w_uk=mla_w_uk, mla_w_uv=mla_w_uv,
             odd_w_out=odd_w_out, ffn_w_gate=ffn_w_gate, ffn_w_up=ffn_w_up, ffn_conv_w=ffn_conv_w,
             ffn_conv_b=ffn_conv_b, ffn_w_down=ffn_w_down)
    st = dict(lru_conv=state_lru_conv, lru_h=state_lru_h, swa_k=cache_swa_k, swa_v=cache_swa_v,
              ssd_conv=state_ssd_conv, ssd=state_ssd, mla_latent=cache_mla_latent,
              mla_rope=cache_mla_rope, page_table=page_table, ffn_conv=state_ffn_conv)
    past_len = page_table.shape[1] * PAGE_SIZE
    pos_prompt = jnp.arange(x_prompt.shape[1])
    pos_sample = past_len + jnp.arange(x_sample.shape[1])
    y_prompt, sp = trunk(x_prompt, pos_prompt, P, None, False)
    y_sample, ss = trunk(x_sample, pos_sample, P, st, True)
    return (y_prompt, y_sample,
            sp['lru_conv'], ss['lru_conv'], sp['lru_h'], ss['lru_h'],
            sp['swa_k'], ss['swa_k'], sp['swa_v'], ss['swa_v'],
            sp['ssd_conv'], ss['ssd_conv'], sp['ssd'], ss['ssd'],
            sp['mla_latent'], ss['mla_latent'], sp['mla_rope'], ss['mla_rope'],
            sp['ffn_conv'], ss['ffn_conv'])
```

```python
import functools
import math

import jax
import jax.numpy as jnp
from jax import lax
from jax.experimental import pallas as pl
from jax.experimental.pallas import tpu as pltpu

F32 = jnp.float32
BF16 = jnp.bfloat16

EPS = 1e-6
LRU_C = 8.0
LRU_HEADS = 16
CONV_W = 4
SWA_HEADS = 8
SWA_KV_HEADS = 2
SWA_HEAD_DIM = 64
WINDOW = 128
SSD_HEADS = 16
SSD_HEAD_DIM = 64
SSD_GROUPS = 2
SSD_STATE = 128
SSD_CHUNK = 128
MLA_HEADS = 8
QK_NOPE = 64
QK_ROPE = 32
V_HEAD = 64
ROPE_THETA = 10000.0
PAGE_SIZE = 128
FFN_CONV = 3
NEG = -1e30

VMEM_LIMIT = 56 * 1024 * 1024
SUBLANES = 8
LANES = 128


def _cparams(sem):
    return pltpu.CompilerParams(dimension_semantics=sem, vmem_limit_bytes=VMEM_LIMIT)


def _dot(a, b):
    return jnp.dot(a, b, preferred_element_type=F32)


def _dot_nt(a, b):
    return lax.dot_general(a, b, (((1,), (1,)), ((), ())), preferred_element_type=F32)


def _dot_tn(a, b):
    return lax.dot_general(a, b, (((0,), (0,)), ((), ())), preferred_element_type=F32)


def _sigmoid(x):
    return 1.0 / (1.0 + jnp.exp(-x))


def _silu(x):
    return x * _sigmoid(x)


def _softplus(x):
    return jnp.maximum(x, 0.0) + jnp.log1p(jnp.exp(-jnp.abs(x)))


def _gelu_tanh(x):
    return 0.5 * x * (1.0 + jnp.tanh(math.sqrt(2.0 / math.pi) * (x + 0.044715 * (x * x * x))))


def _rms(x, g):
    return x * lax.rsqrt(jnp.mean(x * x, axis=-1, keepdims=True) + EPS) * g


def _split3(x):
    x1 = x.astype(BF16)
    r1 = x - x1.astype(F32)
    x2 = r1.astype(BF16)
    x3 = (r1 - x2.astype(F32)).astype(BF16)
    return x1, x2, x3


def _const_spec(shape):
    nd = len(shape)
    return pl.BlockSpec(shape, lambda *_: (0,) * nd, pipeline_mode=pl.Buffered(1))


def _norm_proj_kernel(x_ref, g_ref, *refs, nts):
    n = len(nts)
    w_refs, o_refs = refs[:n], refs[n:]
    xn = _rms(x_ref[...], g_ref[...]).astype(BF16)
    for w_ref, o_ref, nt in zip(w_refs, o_refs, nts):
        if nt:
            o_ref[...] = _dot_nt(w_ref[...], xn)
        else:
            o_ref[...] = _dot(xn, w_ref[...])


def norm_proj(x2, g, ws, nts, tm=512):
    m, k = x2.shape
    tm = min(tm, m)
    in_specs = [pl.BlockSpec((tm, k), lambda i: (i, 0)), _const_spec((1, k))]
    out_shape, out_specs = [], []
    for w, nt in zip(ws, nts):
        in_specs.append(_const_spec(w.shape))
        if nt:
            out_shape.append(jax.ShapeDtypeStruct((w.shape[0], m), F32))
            out_specs.append(pl.BlockSpec((w.shape[0], tm), lambda i: (0, i)))
        else:
            out_shape.append(jax.ShapeDtypeStruct((m, w.shape[1]), F32))
            out_specs.append(pl.BlockSpec((tm, w.shape[1]), lambda i: (i, 0)))
    return pl.pallas_call(
        functools.partial(_norm_proj_kernel, nts=tuple(nts)),
        out_shape=out_shape, grid=(m // tm,), in_specs=in_specs, out_specs=out_specs,
        compiler_params=_cparams(("parallel",)), name="norm_proj",
    )(x2, g.reshape(1, k), *ws)


def _lru_kernel(xr_ref, gate_ref, hist_ref, h0_ref, cw_ref, cb_ref, wa_ref, ba_ref, wx_ref, bx_ref, lam_ref,
                rec_ref, conv_ref, hlast_ref, ext_ref, a_ref, b_ref, h_ref, hprev_ref, *, ns, tt, c, gw):
    t = pl.program_id(1)
    hk = CONV_W - 1

    @pl.when(t == 0)
    def _():
        ext_ref[:, SUBLANES - hk:SUBLANES, :] = hist_ref[...]
        hprev_ref[...] = h0_ref[...]

    ext_ref[:, SUBLANES:SUBLANES + tt, :] = xr_ref[...]
    xc = cb_ref[...]
    for k in range(CONV_W):
        xc = xc + ext_ref[:, SUBLANES - hk + k:SUBLANES - hk + k + tt, :] * cw_ref[k:k + 1, :]
    conv_ref[...] = ext_ref[:, SUBLANES + tt - hk:SUBLANES + tt, :]
    ext_ref[:, 0:SUBLANES, :] = ext_ref[:, tt:tt + SUBLANES, :]

    x2 = xc.reshape(ns * tt, c)
    ra, rx = [], []
    for j in range(c // gw):
        xg = x2[:, j * gw:(j + 1) * gw].astype(BF16)
        ra.append(_dot(xg, wa_ref[j]))
        rx.append(_dot(xg, wx_ref[j]))
    r = _sigmoid(jnp.concatenate(ra, axis=1) + ba_ref[...])
    ig = _sigmoid(jnp.concatenate(rx, axis=1) + bx_ref[...])
    log_a = (-LRU_C) * r * _softplus(-lam_ref[...])
    a = jnp.exp(log_a)
    b = jnp.sqrt(-jnp.tanh(log_a) * (a * a + 1.0)) * (ig * x2)

    a3 = a.reshape(ns * tt // SUBLANES, SUBLANES, c)
    b3 = b.reshape(ns * tt // SUBLANES, SUBLANES, c)
    row = lax.broadcasted_iota(jnp.int32, a3.shape, 1)
    d = 1
    while d < SUBLANES:
        a_sh = jnp.where(row >= d, pltpu.roll(a3, d, axis=1), 1.0)
        b_sh = jnp.where(row >= d, pltpu.roll(b3, d, axis=1), 0.0)
        b3 = a3 * b_sh + b3
        a3 = a3 * a_sh
        d *= 2
    a_ref[...] = a3.reshape(ns, tt, c)
    b_ref[...] = b3.reshape(ns, tt, c)

    def slab(j, hp):
        s = pl.multiple_of(j * SUBLANES, SUBLANES)
        h8 = a_ref[:, pl.ds(s, SUBLANES), :] * hp + b_ref[:, pl.ds(s, SUBLANES), :]
        h_ref[:, pl.ds(s, SUBLANES), :] = h8
        return h8[:, SUBLANES - 1:SUBLANES, :]

    hp = lax.fori_loop(0, tt // SUBLANES, slab, hprev_ref[...])
    hprev_ref[...] = hp
    hlast_ref[...] = hp
    rec_ref[...] = h_ref[...] * _gelu_tanh(gate_ref[...])


def lru(xr3, gate3, hist, h0, cw, cb, wa_bd, ba, wx_bd, bx, lam, ns, tt):
    nseq, length, c = xr3.shape
    gw = wa_bd.shape[-1]
    hk = CONV_W - 1
    grid = (nseq // ns, length // tt)
    blk = pl.BlockSpec((ns, tt, c), lambda s, t: (s, t, 0))
    vec = _const_spec((1, c))
    return pl.pallas_call(
        functools.partial(_lru_kernel, ns=ns, tt=tt, c=c, gw=gw),
        out_shape=[jax.ShapeDtypeStruct((nseq, length, c), F32),
                   jax.ShapeDtypeStruct((nseq, hk, c), F32),
                   jax.ShapeDtypeStruct((nseq, 1, c), F32)],
        grid=grid,
        in_specs=[blk, blk,
                  pl.BlockSpec((ns, hk, c), lambda s, t: (s, 0, 0)),
                  pl.BlockSpec((ns, 1, c), lambda s, t: (s, 0, 0)),
                  _const_spec((CONV_W, c)), vec,
                  _const_spec(wa_bd.shape), vec, _const_spec(wx_bd.shape), vec, vec],
        out_specs=[blk,
                   pl.BlockSpec((ns, hk, c), lambda s, t: (s, 0, 0)),
                   pl.BlockSpec((ns, 1, c), lambda s, t: (s, 0, 0))],
        scratch_shapes=[pltpu.VMEM((ns, SUBLANES + tt, c), F32),
                        pltpu.VMEM((ns, tt, c), F32), pltpu.VMEM((ns, tt, c), F32), pltpu.VMEM((ns, tt, c), F32),
                        pltpu.VMEM((ns, 1, c), F32)],
        compiler_params=_cparams(("arbitrary", "arbitrary")), name="lru",
    )(xr3, gate3, hist, h0, cw, cb.reshape(1, c), wa_bd, ba.reshape(1, c), wx_bd, bx.reshape(1, c), lam.reshape(1, c))


def _swa_prompt_kernel(sink_ref, q_ref, kc_ref, kp_ref, vc_ref, vp_ref, o_ref):
    i = pl.program_id(0)
    w, hd = WINDOW, SWA_HEAD_DIM
    grp = SWA_HEADS // SWA_KV_HEADS
    qi = lax.broadcasted_iota(jnp.int32, (w, w), 0)
    sj = lax.broadcasted_iota(jnp.int32, (w, w), 1)
    mask_cur = sj <= qi
    mask_prev = (sj >= qi) & (i > 0)
    scale = 1.0 / math.sqrt(hd)
    for kh in range(SWA_KV_HEADS):
        ls = slice(kh * hd, (kh + 1) * hd)
        kc, kp = kc_ref[:, ls].astype(BF16), kp_ref[:, ls].astype(BF16)
        vc, vp = vc_ref[:, ls].astype(BF16), vp_ref[:, ls].astype(BF16)
        for g in range(grp):
            h = kh * grp + g
            qh = (q_ref[:, h * hd:(h + 1) * hd] * scale).astype(BF16)
            s_c = jnp.where(mask_cur, _dot_nt(qh, kc), NEG)
            s_p = jnp.where(mask_prev, _dot_nt(qh, kp), NEG)
            sk = sink_ref[h]
            m = jnp.maximum(jnp.maximum(jnp.max(s_c, axis=1, keepdims=True), jnp.max(s_p, axis=1, keepdims=True)), sk)
            p_c = jnp.exp(s_c - m)
            p_p = jnp.exp(s_p - m)
            den = jnp.sum(p_c, axis=1, keepdims=True) + jnp.sum(p_p, axis=1, keepdims=True) + jnp.exp(sk - m)
            o = _dot(p_c.astype(BF16), vc) + _dot(p_p.astype(BF16), vp)
            o_ref[:, h * hd:(h + 1) * hd] = o / den


def swa_prompt(q, k, v, sink):
    length = q.shape[0]
    w = WINDOW
    kvw = SWA_KV_HEADS * SWA_HEAD_DIM
    cur = lambda i: (i, 0)
    prev = lambda i: (jnp.maximum(i - 1, 0), 0)
    return pl.pallas_call(
        _swa_prompt_kernel,
        out_shape=jax.ShapeDtypeStruct(q.shape, F32),
        grid=(length // w,),
        in_specs=[pl.BlockSpec(memory_space=pltpu.SMEM),
                  pl.BlockSpec((w, q.shape[1]), cur),
                  pl.BlockSpec((w, kvw), cur), pl.BlockSpec((w, kvw), prev),
                  pl.BlockSpec((w, kvw), cur), pl.BlockSpec((w, kvw), prev)],
        out_specs=pl.BlockSpec((w, q.shape[1]), cur),
        compiler_params=_cparams(("parallel",)), name="swa_prompt",
    )(sink, q, k, k, v, v)


def _swa_sample_kernel(sink_ref, q_ref, kn_ref, vn_ref, kb_ref, vb_ref, o_ref, ko_ref, vo_ref, kk_ref, vv_ref, *, bs, sl):
    w, hd = WINDOW, SWA_HEAD_DIM
    grp = SWA_HEADS // SWA_KV_HEADS
    nk = kk_ref.shape[0]
    scale = 1.0 / math.sqrt(hd)
    kk_ref[w + sl:nk, :] = jnp.zeros((nk - w - sl, kk_ref.shape[1]), F32)
    vv_ref[w + sl:nk, :] = jnp.zeros((nk - w - sl, vv_ref.shape[1]), F32)
    row = lax.broadcasted_iota(jnp.int32, (grp * sl, nk), 0)
    col = lax.broadcasted_iota(jnp.int32, (grp * sl, nk), 1)
    qi = row % sl
    mask = (col >= qi) & (col <= qi + w)
    rowc = lax.broadcasted_iota(jnp.int32, (grp * sl, 1), 0)

    def seq(b, carry):
        kb, kn = kb_ref[b], kn_ref[b]
        vb, vn = vb_ref[b], vn_ref[b]
        ko_ref[b, 0:w - sl, :] = kb_ref[b, sl:w, :]
        ko_ref[b, w - sl:w, :] = kn
        vo_ref[b, 0:w - sl, :] = vb_ref[b, sl:w, :]
        vo_ref[b, w - sl:w, :] = vn
        kk_ref[0:w, :] = kb
        kk_ref[w:w + sl, :] = kn
        vv_ref[0:w, :] = vb
        vv_ref[w:w + sl, :] = vn
        q = q_ref[b]
        for kh in range(SWA_KV_HEADS):
            ls = slice(kh * hd, (kh + 1) * hd)
            kkh = kk_ref[:, ls].astype(BF16)
            vvh = vv_ref[:, ls].astype(BF16)
            qs = jnp.concatenate([q[:, (kh * grp + g) * hd:(kh * grp + g + 1) * hd] for g in range(grp)], axis=0)
            s = jnp.where(mask, _dot_nt((qs * scale).astype(BF16), kkh), NEG)
            sk = jnp.zeros((grp * sl, 1), F32)
            for g in range(grp):
                sk = jnp.where(rowc // sl == g, sink_ref[kh * grp + g], sk)
            m = jnp.maximum(jnp.max(s, axis=1, keepdims=True), sk)
            p = jnp.exp(s - m)
            den = jnp.sum(p, axis=1, keepdims=True) + jnp.exp(sk - m)
            o = _dot(p.astype(BF16), vvh) / den
            for g in range(grp):
                h = kh * grp + g
                o_ref[b, :, h * hd:(h + 1) * hd] = o[g * sl:(g + 1) * sl, :]
        return carry

    lax.fori_loop(0, bs, seq, 0)


def swa_sample(q3, kn3, vn3, kbuf, vbuf, sink, bs=8):
    nseq, sl, qw = q3.shape
    w = WINDOW
    kvw = kn3.shape[-1]
    nk = 2 * w
    b3 = lambda shape: pl.BlockSpec(shape, lambda i: (i, 0, 0))
    return pl.pallas_call(
        functools.partial(_swa_sample_kernel, bs=bs, sl=sl),
        out_shape=[jax.ShapeDtypeStruct((nseq, sl, qw), F32),
                   jax.ShapeDtypeStruct((nseq, w, kvw), F32),
                   jax.ShapeDtypeStruct((nseq, w, kvw), F32)],
        grid=(nseq // bs,),
        in_specs=[pl.BlockSpec(memory_space=pltpu.SMEM),
                  b3((bs, sl, qw)), b3((bs, sl, kvw)), b3((bs, sl, kvw)), b3((bs, w, kvw)), b3((bs, w, kvw))],
        out_specs=[b3((bs, sl, qw)), b3((bs, w, kvw)), b3((bs, w, kvw))],
        scratch_shapes=[pltpu.VMEM((nk, kvw), F32), pltpu.VMEM((nk, kvw), F32)],
        compiler_params=_cparams(("arbitrary",)), name="swa_sample",
    )(sink, q3, kn3, vn3, kbuf, vbuf)


def _out_res_kernel(x_ref, a1_ref, w1_ref, a2_ref, w2_ref, o_ref):
    o_ref[...] = (x_ref[...] + _dot(a1_ref[...].astype(BF16), w1_ref[...])
                  + _dot(a2_ref[...].astype(BF16), w2_ref[...]))


def out_res(x2, a1, w1, a2, w2, tm=512):
    m, d = x2.shape
    tm = min(tm, m)
    row = lambda width: pl.BlockSpec((tm, width), lambda i: (i, 0))
    return pl.pallas_call(
        _out_res_kernel,
        out_shape=jax.ShapeDtypeStruct((m, d), F32),
        grid=(m // tm,),
        in_specs=[row(d), row(a1.shape[1]), _const_spec(w1.shape), row(a2.shape[1]), _const_spec(w2.shape)],
        out_specs=row(d),
        compiler_params=_cparams(("parallel",)), name="out_res",
    )(x2, a1, w1, a2, w2)


def _ffn_kernel(x_ref, g_ref, wg_ref, wu_ref, cw_ref, cb_ref, wd_ref, hist_ref, fg_ref,
                o_ref, hout_ref, ext_ref, acc_ref, *, ns, tt, d, f, tf, final):
    t = pl.program_id(1)
    hk = FFN_CONV - 1

    @pl.when(t == 0)
    def _():
        ext_ref[:, SUBLANES - hk:SUBLANES, :] = hist_ref[...]

    x = x_ref[...].reshape(ns * tt, d)
    xn = _rms(x, g_ref[...]).astype(BF16)
    for c in range(f // tf):
        sl = slice(c * tf, (c + 1) * tf)
        g3 = _dot(xn, wg_ref[:, sl]).reshape(ns, tt, tf)
        ext_ref[:, SUBLANES:SUBLANES + tt, sl] = g3
        gc = cb_ref[:, sl] + g3 * cw_ref[hk:hk + 1, sl]
        for k in range(hk):
            gc = gc + ext_ref[:, SUBLANES - hk + k:SUBLANES - hk + k + tt, sl] * cw_ref[k:k + 1, sl]
        u = _dot(xn, wu_ref[:, sl])
        act = (_silu(gc).reshape(ns * tt, tf) * u).astype(BF16)
        contrib = _dot(act, wd_ref[sl, :])
        if c == 0:
            acc_ref[...] = contrib
        else:
            acc_ref[...] += contrib
    y = x + acc_ref[...]
    if final:
        y = _rms(y, fg_ref[...])
    o_ref[...] = y.reshape(ns, tt, d)
    hout_ref[...] = ext_ref[:, SUBLANES + tt - hk:SUBLANES + tt, :]
    ext_ref[:, 0:SUBLANES, :] = ext_ref[:, tt:tt + SUBLANES, :]


def ffn(x3, g, wg, wu, cw, cb, wd, hist, fg, ns, tt, final, tf=256):
    nseq, length, d = x3.shape
    f = wg.shape[1]
    hk = FFN_CONV - 1
    blk = pl.BlockSpec((ns, tt, d), lambda s, t: (s, t, 0))
    hspec = pl.BlockSpec((ns, hk, f), lambda s, t: (s, 0, 0))
    return pl.pallas_call(
        functools.partial(_ffn_kernel, ns=ns, tt=tt, d=d, f=f, tf=tf, final=final),
        out_shape=[jax.ShapeDtypeStruct((nseq, length, d), F32), jax.ShapeDtypeStruct((nseq, hk, f), F32)],
        grid=(nseq // ns, length // tt),
        in_specs=[blk, _const_spec((1, d)), _const_spec(wg.shape), _const_spec(wu.shape),
                  _const_spec((FFN_CONV, f)), _const_spec((1, f)), _const_spec(wd.shape), hspec, _const_spec((1, d))],
        out_specs=[blk, hspec],
        scratch_shapes=[pltpu.VMEM((ns, SUBLANES + tt, f), F32), pltpu.VMEM((ns * tt, d), F32)],
        compiler_params=_cparams(("arbitrary", "arbitrary")), name="ffn",
    )(x3, g.reshape(1, d), wg, wu, cw, cb.reshape(1, f), wd, hist, fg.reshape(1, d))


def _ssd_kernel(z_ref, xbc_ref, dt_ref, dtt_ref, hist_ref, s0_ref, cw_ref, cb_ref, dtb_ref, dtbt_ref,
                alog_ref, alogt_ref, dvec_ref, nw_ref,
                y_ref, conv_ref, sout_ref, ext_ref, st_ref, ybuf_ref, dtp_ref, dttp_ref, *, tin, tc, inner, nst):
    c = pl.program_id(1)
    hk = CONV_W - 1
    nh, hp = SSD_HEADS, SSD_HEAD_DIM
    gh = nh // SSD_GROUPS

    @pl.when(c == 0)
    def _():
        if tin < tc:
            ext_ref[...] = jnp.zeros(ext_ref.shape, F32)
            dtp_ref[...] = jnp.zeros(dtp_ref.shape, F32)
            dttp_ref[...] = jnp.zeros(dttp_ref.shape, F32)
        ext_ref[SUBLANES - hk:SUBLANES, :] = hist_ref[0]
        st_ref[...] = s0_ref[0]

    ext_ref[SUBLANES:SUBLANES + tin, :] = xbc_ref[0]
    xc = cb_ref[...]
    for k in range(CONV_W):
        xc = xc + ext_ref[SUBLANES - hk + k:SUBLANES - hk + k + tc, :] * cw_ref[k:k + 1, :]
    conv_ref[0] = ext_ref[SUBLANES + tin - hk:SUBLANES + tin, :]
    if tin == tc:
        ext_ref[0:SUBLANES, :] = ext_ref[tc:tc + SUBLANES, :]
    xa = _silu(xc)
    xs = xa[:, :inner]
    bm = [xa[:, inner + g * nst:inner + (g + 1) * nst].astype(BF16) for g in range(SSD_GROUPS)]
    cm = [xa[:, inner + (SSD_GROUPS + g) * nst:inner + (SSD_GROUPS + g + 1) * nst].astype(BF16) for g in range(SSD_GROUPS)]

    if tin < tc:
        dtp_ref[0:tin, :] = dt_ref[0]
        dttp_ref[:, 0:tin] = dtt_ref[0]
        dt_raw, dtt_raw = dtp_ref[...], dttp_ref[...]
        valid_r = lax.broadcasted_iota(jnp.int32, (tc, nh), 0) < tin
        valid_c = lax.broadcasted_iota(jnp.int32, (nh, tc), 1) < tin
        dt = jnp.where(valid_r, _softplus(dt_raw + dtb_ref[...]), 0.0)
        dtt = jnp.where(valid_c, _softplus(dtt_raw + dtbt_ref[...]), 0.0)
    else:
        dt = _softplus(dt_ref[0] + dtb_ref[...])
        dtt = _softplus(dtt_ref[0] + dtbt_ref[...])
    da = dt * (-jnp.exp(alog_ref[...]))
    dat = dtt * (-jnp.exp(alogt_ref[...]))

    li = lax.broadcasted_iota(jnp.int32, (tc, tc), 0)
    si = lax.broadcasted_iota(jnp.int32, (tc, tc), 1)
    causal = li >= si
    tri = jnp.where(causal, 1.0, 0.0).astype(BF16)
    trit = jnp.where(li <= si, 1.0, 0.0).astype(BF16)
    cs = sum(_dot(tri, p) for p in _split3(da))
    cst = sum(_dot(p, trit) for p in _split3(dat))
    cs_last = cs[tc - 1:tc, :]
    cb_g = [_dot_nt(cm[g], bm[g]) for g in range(SSD_GROUPS)]

    for h in range(nh):
        g = h // gh
        ls = slice(h * hp, (h + 1) * hp)
        col = cs[:, h:h + 1]
        rowv = cst[h:h + 1, :]
        lmat = jnp.exp(jnp.where(causal, col - rowv, NEG))
        mh = (cb_g[g] * lmat).astype(BF16)
        xs_h = xs[:, ls]
        xd = xs_h * dt[:, h:h + 1]
        yh = _dot(mh, xd.astype(BF16))
        s_h = st_ref[h]
        yh = yh + jnp.exp(col) * _dot_nt(cm[g], s_h.astype(BF16))
        last = cs_last[:, h:h + 1]
        xdd = (xd * jnp.exp(last - col)).astype(BF16)
        st_ref[h] = jnp.exp(last) * s_h + _dot_tn(xdd, bm[g])
        ybuf_ref[:, ls] = yh + dvec_ref[:, ls] * xs_h

    y = ybuf_ref[0:tin, :] * _silu(z_ref[0])
    gwid = inner // SSD_GROUPS
    for g in range(SSD_GROUPS):
        gs = slice(g * gwid, (g + 1) * gwid)
        y_ref[0, :, gs] = _rms(y[:, gs], nw_ref[:, gs])
    sout_ref[0] = st_ref[...]


def ssd(z3, xbc3, dt3, dtt3, hist, s0, cw, cb, dtb, alog, dvec, nw, tin):
    nseq, length, inner = z3.shape
    cd = xbc3.shape[-1]
    nh, hp, nst = s0.shape[1:]
    tc = SSD_CHUNK
    hk = CONV_W - 1
    t3 = lambda width: pl.BlockSpec((1, tin, width), lambda s, c: (s, c, 0))
    per_seq = lambda shape: pl.BlockSpec((1,) + shape, lambda s, c: (s,) + (0,) * len(shape))
    return pl.pallas_call(
        functools.partial(_ssd_kernel, tin=tin, tc=tc, inner=inner, nst=nst),
        out_shape=[jax.ShapeDtypeStruct((nseq, length, inner), F32),
                   jax.ShapeDtypeStruct((nseq, hk, cd), F32),
                   jax.ShapeDtypeStruct(s0.shape, F32)],
        grid=(nseq, length // tin),
        in_specs=[t3(inner), t3(cd), t3(nh),
                  pl.BlockSpec((1, nh, tin), lambda s, c: (s, 0, c)),
                  per_seq((hk, cd)), per_seq((nh, hp, nst)),
                  _const_spec((CONV_W, cd)), _const_spec((1, cd)), _const_spec((1, nh)), _const_spec((nh, 1)),
                  _const_spec((1, nh)), _const_spec((nh, 1)), _const_spec((1, inner)), _const_spec((1, inner))],
        out_specs=[t3(inner), per_seq((hk, cd)), per_seq((nh, hp, nst))],
        scratch_shapes=[pltpu.VMEM((SUBLANES + tc, cd), F32), pltpu.VMEM((nh, hp, nst), F32),
                        pltpu.VMEM((tc, inner), F32), pltpu.VMEM((tc, nh), F32), pltpu.VMEM((nh, tc), F32)],
        compiler_params=_cparams(("arbitrary", "arbitrary")), name="ssd",
    )(z3, xbc3, dt3, dtt3, hist, s0, cw, cb.reshape(1, cd), dtb.reshape(1, nh), dtb.reshape(nh, 1),
      alog.reshape(1, nh), alog.reshape(nh, 1), dvec.reshape(1, inner), nw.reshape(1, inner))


def _rope128(t, ctab, stab):
    half = QK_ROPE // 2
    lane = lax.broadcasted_iota(jnp.int32, t.shape, 1)
    swapped = jnp.where(lane < QK_NOPE + half, pltpu.roll(t, LANES - half, axis=1), pltpu.roll(t, half, axis=1))
    return t * ctab + swapped * stab


def _mla_prep_kernel(cq_ref, ckv_ref, kr_ref, ct_ref, st_ref, qg_ref, kg_ref, wuq_ref, wk_ref, wv_ref,
                     lat_ref, rope_ref, *outs, sample):
    scale = 1.0 / math.sqrt(QK_NOPE + QK_ROPE)
    ctab, stab = ct_ref[...], st_ref[...]
    qf = _dot(_rms(cq_ref[...], qg_ref[...]).astype(BF16), wuq_ref[...])
    ckv = _rms(ckv_ref[...], kg_ref[...])
    lat_ref[...] = ckv
    krr = _rope128(kr_ref[...], ctab, stab)
    rope_ref[...] = krr[:, QK_NOPE:QK_NOPE + QK_ROPE]
    if sample:
        qlat_ref, qr_ref = outs
        for h in range(MLA_HEADS):
            qh = _rope128(qf[:, h * LANES:(h + 1) * LANES], ctab, stab) * scale
            qlat_ref[h] = _dot(qh[:, :QK_NOPE].astype(BF16), wk_ref[h])
            qr_ref[h] = qh[:, QK_NOPE:QK_NOPE + QK_ROPE]
    else:
        qc_ref, kc_ref, vc_ref = outs
        ckvb = ckv.astype(BF16)
        for h in range(MLA_HEADS):
            qh = _rope128(qf[:, h * LANES:(h + 1) * LANES], ctab, stab) * scale
            qc_ref[h] = qh.astype(BF16)
            kc_ref[h] = (_dot(ckvb, wk_ref[h]) + krr).astype(BF16)
            vc_ref[h] = _dot(ckvb, wv_ref[h]).astype(BF16)


def mla_prep(cq, ckv_raw, krp, ctab, stab, qg, kg, wuq_pad, wk, wv, sample, tm=512):
    m = cq.shape[0]
    tm = min(tm, m)
    nh = MLA_HEADS
    row = lambda width: pl.BlockSpec((tm, width), lambda i: (i, 0))
    hrow = lambda width: pl.BlockSpec((nh, tm, width), lambda i: (0, i, 0))
    lq, lkv = cq.shape[1], ckv_raw.shape[1]
    out_shape = [jax.ShapeDtypeStruct((m, lkv), F32), jax.ShapeDtypeStruct((m, QK_ROPE), F32)]
    out_specs = [row(lkv), row(QK_ROPE)]
    if sample:
        out_shape += [jax.ShapeDtypeStruct((nh, m, lkv), F32), jax.ShapeDtypeStruct((nh, m, QK_ROPE), F32)]
        out_specs += [hrow(lkv), hrow(QK_ROPE)]
    else:
        out_shape += [jax.ShapeDtypeStruct((nh, m, LANES), BF16)] * 3
        out_specs += [hrow(LANES)] * 3
    return pl.pallas_call(
        functools.partial(_mla_prep_kernel, sample=sample),
        out_shape=out_shape, grid=(m // tm,),
        in_specs=[row(lq), row(lkv), row(LANES), row(LANES), row(LANES), _const_spec((1, lq)), _const_spec((1, lkv)),
                  _const_spec(wuq_pad.shape), _const_spec(wk.shape), _const_spec(wv.shape)],
        out_specs=out_specs,
        compiler_params=_cparams(("parallel",)), name="mla_prep",
    )(cq, ckv_raw, krp, ctab, stab, qg.reshape(1, lq), kg.reshape(1, lkv), wuq_pad, wk, wv)


def _mla_flash_kernel(q_ref, k_ref, v_ref, o_ref, m_ref, l_ref, acc_ref, *, tq, tk):
    i, j = pl.program_id(0), pl.program_id(1)
    last_j = (i * tq + tq - 1) // tk

    @pl.when(j == 0)
    def _():
        m_ref[...] = jnp.full(m_ref.shape, NEG, F32)
        l_ref[...] = jnp.zeros(l_ref.shape, F32)
        acc_ref[...] = jnp.zeros(acc_ref.shape, F32)

    @pl.when(j <= last_j)
    def _():
        qpos = i * tq + lax.broadcasted_iota(jnp.int32, (tq, tk), 0)
        kpos = j * tk + lax.broadcasted_iota(jnp.int32, (tq, tk), 1)
        keep = qpos >= kpos
        for h in range(MLA_HEADS):
            s = jnp.where(keep, _dot_nt(q_ref[h], k_ref[h]), NEG)
            m_prev = m_ref[h]
            m_new = jnp.maximum(m_prev, jnp.max(s, axis=1, keepdims=True))
            alpha = jnp.exp(m_prev - m_new)
            p = jnp.exp(s - m_new)
            l_ref[h] = alpha * l_ref[h] + jnp.sum(p, axis=1, keepdims=True)
            acc_ref[h] = alpha * acc_ref[h] + _dot(p.astype(BF16), v_ref[h])
            m_ref[h] = m_new

    @pl.when(j == last_j)
    def _():
        for h in range(MLA_HEADS):
            o_ref[:, h * V_HEAD:(h + 1) * V_HEAD] = (acc_ref[h] / l_ref[h])[:, :V_HEAD]


def mla_flash(qc, kc, vc, tq=512, tk=512):
    nh, length, _ = qc.shape
    tq, tk = min(tq, length), min(tk, length)
    kv_idx = lambda i, j: (0, jnp.minimum(j, (i * tq + tq - 1) // tk), 0)
    return pl.pallas_call(
        functools.partial(_mla_flash_kernel, tq=tq, tk=tk),
        out_shape=jax.ShapeDtypeStruct((length, nh * V_HEAD), F32),
        grid=(length // tq, length // tk),
        in_specs=[pl.BlockSpec((nh, tq, LANES), lambda i, j: (0, i, 0)),
                  pl.BlockSpec((nh, tk, LANES), kv_idx), pl.BlockSpec((nh, tk, LANES), kv_idx)],
        out_specs=pl.BlockSpec((tq, nh * V_HEAD), lambda i, j: (i, 0)),
        scratch_shapes=[pltpu.VMEM((nh, tq, 1), F32), pltpu.VMEM((nh, tq, 1), F32), pltpu.VMEM((nh, tq, LANES), F32)],
        compiler_params=_cparams(("parallel", "arbitrary")), name="mla_flash",
    )(qc, kc, vc)


def _mla_sample_kernel(pt_ref, qlat_ref, qr_ref, ckv_ref, kr_ref, wuv_ref, *rest, pp, sl):
    lat_refs, rope_refs = rest[:pp], rest[pp:2 * pp]
    o_ref, m_ref, l_ref, acc_ref, kl_ref, krl_ref = rest[2 * pp:]
    j, nj = pl.program_id(1), pl.num_programs(1)
    rows = MLA_HEADS * sl

    @pl.when(j == 0)
    def _():
        m_ref[...] = jnp.full(m_ref.shape, NEG, F32)
        l_ref[...] = jnp.zeros(l_ref.shape, F32)
        acc_ref[...] = jnp.zeros(acc_ref.shape, F32)

    q = qlat_ref[...].reshape(rows, qlat_ref.shape[-1]).astype(BF16)
    qr = qr_ref[...].reshape(rows, qr_ref.shape[-1]).astype(BF16)

    def update(s_list, v_list):
        m_prev = m_ref[...]
        m_new = m_prev
        for s in s_list:
            m_new = jnp.maximum(m_new, jnp.max(s, axis=1, keepdims=True))
        alpha = jnp.exp(m_prev - m_new)
        lsum = alpha * l_ref[...]
        acc = alpha * acc_ref[...]
        for s, v in zip(s_list, v_list):
            p = jnp.exp(s - m_new)
            lsum = lsum + jnp.sum(p, axis=1, keepdims=True)
            acc = acc + _dot(p.astype(BF16), v)
        m_ref[...] = m_new
        l_ref[...] = lsum
        acc_ref[...] = acc

    s_list, v_list = [], []
    for c in range(pp):
        lat = lat_refs[c][0, 0].astype(BF16)
        rp = rope_refs[c][0, 0].astype(BF16)
        s_list.append(_dot_nt(q, lat) + _dot_nt(qr, rp))
        v_list.append(lat)
    update(s_list, v_list)

    @pl.when(j == nj - 1)
    def _():
        kl_ref[...] = jnp.zeros(kl_ref.shape, F32)
        krl_ref[...] = jnp.zeros(krl_ref.shape, F32)
        kl_ref[0:sl, :] = ckv_ref[0]
        krl_ref[0:sl, :] = kr_ref[0]
        kl = kl_ref[...].astype(BF16)
        krl = krl_ref[...].astype(BF16)
        s = _dot_nt(q, kl) + _dot_nt(qr, krl)
        row = lax.broadcasted_iota(jnp.int32, s.shape, 0)
        col = lax.broadcasted_iota(jnp.int32, s.shape, 1)
        s = jnp.where((col < sl) & (col <= row % sl), s, NEG)
        update([s], [kl])
        accn = (acc_ref[...] / l_ref[...]).astype(BF16)
        for h in range(MLA_HEADS):
            o_ref[0, :, h * V_HEAD:(h + 1) * V_HEAD] = _dot(accn[h * sl:(h + 1) * sl, :], wuv_ref[h])


def mla_sample(qlat, qr, ckv3, kr3, wuv, cache_lat, cache_rope, page_table, li, pp=8):
    nh, m, lkv = qlat.shape
    nseq, sl, _ = ckv3.shape
    npages = page_table.shape[1]
    pp = min(pp, npages)
    rd = cache_rope.shape[-1]
    psz = cache_lat.shape[2]

    def page_spec(width, c):
        return pl.BlockSpec((1, 1, psz, width), lambda b, j, pt: (li, pt[b, j * pp + c], 0, 0))

    in_specs = [pl.BlockSpec((nh, sl, lkv), lambda b, j, pt: (0, b, 0)),
                pl.BlockSpec((nh, sl, rd), lambda b, j, pt: (0, b, 0)),
                pl.BlockSpec((1, sl, lkv), lambda b, j, pt: (b, 0, 0)),
                pl.BlockSpec((1, sl, rd), lambda b, j, pt: (b, 0, 0)),
                pl.BlockSpec(wuv.shape, lambda b, j, pt: (0, 0, 0))]
    in_specs += [page_spec(lkv, c) for c in range(pp)] + [page_spec(rd, c) for c in range(pp)]
    grid_spec = pltpu.PrefetchScalarGridSpec(
        num_scalar_prefetch=1, grid=(nseq, npages // pp), in_specs=in_specs,
        out_specs=pl.BlockSpec((1, sl, nh * V_HEAD), lambda b, j, pt: (b, 0, 0)),
        scratch_shapes=[pltpu.VMEM((nh * sl, 1), F32), pltpu.VMEM((nh * sl, 1), F32), pltpu.VMEM((nh * sl, lkv), F32),
                        pltpu.VMEM((psz, lkv), F32), pltpu.VMEM((psz, rd), F32)])
    return pl.pallas_call(
        functools.partial(_mla_sample_kernel, pp=pp, sl=sl),
        out_shape=jax.ShapeDtypeStruct((nseq, sl, nh * V_HEAD), F32),
        grid_spec=grid_spec,
        compiler_params=_cparams(("arbitrary", "arbitrary")), name="mla_sample",
    )(page_table, qlat, qr, ckv3, kr3, wuv, *([cache_lat] * pp), *([cache_rope] * pp))


def _block_diag(w, per):
    nh, b, _ = w.shape
    w4 = w.reshape(nh // per, per, b, b)
    eye = jnp.eye(per, dtype=w.dtype)
    return jnp.einsum("gaij,ab->gaibj", w4, eye).reshape(nh // per, per * b, per * b)


def _rope_tables(pos):
    half = QK_ROPE // 2
    inv = jnp.exp(-(math.log(ROPE_THETA) / half) * jnp.arange(half, dtype=F32))
    ang = pos.astype(F32)[:, None] * inv[None, :]
    cos, sin = jnp.cos(ang), jnp.sin(ang)
    n = pos.shape[0]
    pad = LANES - QK_NOPE - QK_ROPE
    ctab = jnp.concatenate([jnp.ones((n, QK_NOPE), F32), cos, cos, jnp.ones((n, pad), F32)], axis=1)
    stab = jnp.concatenate([jnp.zeros((n, QK_NOPE), F32), -sin, sin, jnp.zeros((n, pad), F32)], axis=1)
    return ctab, stab


def _prep_weights(P):
    w = {}
    lw = P["lru_conv_w"].shape[-1]
    qw = SWA_HEADS * SWA_HEAD_DIM
    kvw = SWA_KV_HEADS * SWA_HEAD_DIM
    ew = P["even_w_in"][0].astype(BF16)
    b = [0, lw, 2 * lw, 2 * lw + qw, 2 * lw + qw + kvw, 2 * lw + qw + 2 * kvw]
    w["even_in"] = [ew[:, b[i]:b[i + 1]] for i in range(5)]
    per = 256 // (lw // LRU_HEADS)
    w["wa_bd"] = _block_diag(P["lru_wa"][0], per).astype(BF16)
    w["wx_bd"] = _block_diag(P["lru_wx"][0], per).astype(BF16)
    eo = P["even_w_out"][0].astype(BF16)
    w["even_out"] = (eo[:lw], eo[lw:])

    inner = P["ssd_norm"].shape[-1]
    cd = P["ssd_conv_w"].shape[-1]
    lq = P["mla_q_norm"].shape[-1]
    lkv = P["mla_kv_norm"].shape[-1]
    ow = P["odd_w_in"][0]
    b = [0, inner, inner + cd, inner + cd + SSD_HEADS, inner + cd + SSD_HEADS + lq,
         inner + cd + SSD_HEADS + lq + lkv, inner + cd + SSD_HEADS + lq + lkv + QK_ROPE]
    parts = [ow[:, b[i]:b[i + 1]] for i in range(6)]
    pad = LANES - QK_NOPE - QK_ROPE
    kr_pad = jnp.pad(parts[5], ((0, 0), (QK_NOPE, pad)))
    w["odd_in"] = [parts[0].astype(BF16), parts[1].astype(BF16), parts[2].astype(BF16), parts[2].T.astype(BF16),
                   parts[3].astype(BF16), parts[4].astype(BF16), kr_pad.astype(BF16)]
    uq = P["mla_w_uq"][0].reshape(lq, MLA_HEADS, QK_NOPE + QK_ROPE)
    w["wuq_pad"] = jnp.pad(uq, ((0, 0), (0, 0), (0, pad))).reshape(lq, MLA_HEADS * LANES).astype(BF16)
    uk = P["mla_w_uk"][0]
    uv = P["mla_w_uv"][0]
    w["wuk_pad"] = jnp.pad(jnp.transpose(uk, (1, 0, 2)), ((0, 0), (0, 0), (0, LANES - QK_NOPE))).astype(BF16)
    w["wuk_t"] = jnp.transpose(uk, (1, 2, 0)).astype(BF16)
    w["wuv_pad"] = jnp.pad(jnp.transpose(uv, (1, 0, 2)), ((0, 0), (0, 0), (0, LANES - V_HEAD))).astype(BF16)
    w["wuv"] = jnp.transpose(uv, (1, 0, 2)).astype(BF16)
    oo = P["odd_w_out"][0].astype(BF16)
    w["odd_out"] = (oo[:inner], oo[inner:])
    w["dvec"] = jnp.repeat(P["ssd_d"][0], SSD_HEAD_DIM)
    w["ffn"] = [(P["ffn_w_gate"][l].astype(BF16), P["ffn_w_up"][l].astype(BF16), P["ffn_w_down"][l].astype(BF16))
                for l in range(P["ffn_w_gate"].shape[0])]
    return w


def _trunk(x3, pos, P, W, st, sample):
    nseq, length, d = x3.shape
    m = nseq * length
    lw = P["lru_conv_w"].shape[-1]
    kvw = SWA_KV_HEADS * SWA_HEAD_DIM
    if sample:
        seq_tile, time_tile = min(nseq, 64), length
    else:
        seq_tile, time_tile = 1, min(length, 512)
    out = {}

    xr, gate, q, k, v = norm_proj(x3.reshape(m, d), P["mix_norm"][0], W["even_in"], [False] * 5)
    lru_ns, lru_tt = (min(nseq, 32), length) if sample else (1, min(length, 256))
    rec, out["lru_conv"], h_last = lru(
        xr.reshape(nseq, length, lw), gate.reshape(nseq, length, lw), st["lru_conv"], st["lru_h"].reshape(nseq, 1, lw),
        P["lru_conv_w"][0], P["lru_conv_b"][0], W["wa_bd"], P["lru_ba"][0], W["wx_bd"], P["lru_bx"][0],
        P["lru_lambda"][0], lru_ns, lru_tt)
    out["lru_h"] = h_last.reshape(nseq, lw)
    if sample:
        att, sk, sv = swa_sample(q.reshape(nseq, length, -1), k.reshape(nseq, length, kvw), v.reshape(nseq, length, kvw),
                                 st["swa_k"].reshape(nseq, WINDOW, kvw), st["swa_v"].reshape(nseq, WINDOW, kvw),
                                 P["swa_sink"][0], bs=min(nseq, 8))
        att = att.reshape(m, -1)
        out["swa_k"] = sk.reshape(nseq, WINDOW, SWA_KV_HEADS, SWA_HEAD_DIM)
        out["swa_v"] = sv.reshape(nseq, WINDOW, SWA_KV_HEADS, SWA_HEAD_DIM)
    else:
        att = swa_prompt(q, k, v, P["swa_sink"][0])
        out["swa_k"] = k[-WINDOW:].reshape(1, WINDOW, SWA_KV_HEADS, SWA_HEAD_DIM)
        out["swa_v"] = v[-WINDOW:].reshape(1, WINDOW, SWA_KV_HEADS, SWA_HEAD_DIM)
    x2 = out_res(x3.reshape(m, d), rec.reshape(m, lw), W["even_out"][0], att, W["even_out"][1])
    wg, wu, wd = W["ffn"][0]
    x3, fc0 = ffn(x2.reshape(nseq, length, d), P["ffn_norm"][0], wg, wu, P["ffn_conv_w"][0], P["ffn_conv_b"][0], wd,
                  st["ffn_conv"][0], P["final_norm"], seq_tile, time_tile, final=False)

    z, xbc, dt, dtt, cq, ckv_raw, krp = norm_proj(x3.reshape(m, d), P["mix_norm"][1], W["odd_in"],
                                                   [False, False, False, True, False, False, False])
    inner = z.shape[-1]
    cd = xbc.shape[-1]
    dtt3 = jnp.transpose(dtt.reshape(SSD_HEADS, nseq, length), (1, 0, 2))
    y, out["ssd_conv"], out["ssd"] = ssd(
        z.reshape(nseq, length, inner), xbc.reshape(nseq, length, cd), dt.reshape(nseq, length, SSD_HEADS), dtt3,
        st["ssd_conv"], st["ssd"], P["ssd_conv_w"][0], P["ssd_conv_b"][0], P["ssd_dt_bias"][0], P["ssd_a_log"][0],
        W["dvec"], P["ssd_norm"][0], tin=min(length, SSD_CHUNK))
    ctab, stab = _rope_tables(pos)
    if sample:
        lat, rp, qlat, qr = mla_prep(cq, ckv_raw, krp, ctab, stab, P["mla_q_norm"][0], P["mla_kv_norm"][0],
                                     W["wuq_pad"], W["wuk_t"], W["wuv"], sample=True)
        att = mla_sample(qlat, qr, lat.reshape(nseq, length, -1), rp.reshape(nseq, length, -1), W["wuv"],
                         st["mla_latent"], st["mla_rope"], st["page_table"], 0).reshape(m, -1)
    else:
        lat, rp, qc, kc, vc = mla_prep(cq, ckv_raw, krp, ctab, stab, P["mla_q_norm"][0], P["mla_kv_norm"][0],
                                       W["wuq_pad"], W["wuk_pad"], W["wuv_pad"], sample=False)
        att = mla_flash(qc, kc, vc)
    out["mla_latent"] = lat.reshape(nseq, length, -1)
    out["mla_rope"] = rp.reshape(nseq, length, -1)
    x2 = out_res(x3.reshape(m, d), y.reshape(m, inner), W["odd_out"][0], att, W["odd_out"][1])
    wg, wu, wd = W["ffn"][1]
    y3, fc1 = ffn(x2.reshape(nseq, length, d), P["ffn_norm"][1], wg, wu, P["ffn_conv_w"][1], P["ffn_conv_b"][1], wd,
                  st["ffn_conv"][1], P["final_norm"], seq_tile, time_tile, final=True)
    out["ffn_conv"] = jnp.stack([fc0, fc1])
    return y3, out


def kernel(x_prompt, x_sample, state_lru_conv, state_lru_h, cache_swa_k, cache_swa_v, state_ssd_conv, state_ssd,
           cache_mla_latent, cache_mla_rope, page_table, state_ffn_conv, mix_norm, ffn_norm, final_norm, even_w_in,
           lru_conv_w, lru_conv_b, lru_wa, lru_ba, lru_wx, lru_bx, lru_lambda, swa_sink, even_w_out, odd_w_in,
           ssd_conv_w, ssd_conv_b, ssd_dt_bias, ssd_a_log, ssd_d, ssd_norm, mla_q_norm, mla_w_uq, mla_kv_norm,
           mla_w_uk, mla_w_uv, odd_w_out, ffn_w_gate, ffn_w_up, ffn_conv_w, ffn_conv_b, ffn_w_down):
    P = dict(mix_norm=mix_norm, ffn_norm=ffn_norm, final_norm=final_norm, even_w_in=even_w_in,
             lru_conv_w=lru_conv_w, lru_conv_b=lru_conv_b, lru_wa=lru_wa, lru_ba=lru_ba, lru_wx=lru_wx,
             lru_bx=lru_bx, lru_lambda=lru_lambda, swa_sink=swa_sink, even_w_out=even_w_out,
             odd_w_in=odd_w_in, ssd_conv_w=ssd_conv_w, ssd_conv_b=ssd_conv_b, ssd_dt_bias=ssd_dt_bias,
             ssd_a_log=ssd_a_log, ssd_d=ssd_d, ssd_norm=ssd_norm, mla_q_norm=mla_q_norm,
             mla_w_uq=mla_w_uq, mla_kv_norm=mla_kv_norm, mla_w_uk=mla_w_uk, mla_w_uv=mla_w_uv,
             odd_w_out=odd_w_out, ffn_w_gate=ffn_w_gate, ffn_w_up=ffn_w_up, ffn_conv_w=ffn_conv_w,
             ffn_conv_b=ffn_conv_b, ffn_w_down=ffn_w_down)
    W = _prep_weights(P)
    bp, lp, d = x_prompt.shape
    bs, ls, _ = x_sample.shape
    depth = ffn_w_gate.shape[0]
    lw = lru_conv_w.shape[-1]
    cd = ssd_conv_w.shape[-1]
    f = ffn_w_gate.shape[-1]
    kvw = SWA_KV_HEADS * SWA_HEAD_DIM
    past_len = page_table.shape[1] * PAGE_SIZE

    st_p = dict(lru_conv=jnp.zeros((bp, CONV_W - 1, lw), F32), lru_h=jnp.zeros((bp, lw), F32),
                ssd_conv=jnp.zeros((bp, CONV_W - 1, cd), F32),
                ssd=jnp.zeros((bp, SSD_HEADS, SSD_HEAD_DIM, SSD_STATE), F32),
                ffn_conv=jnp.zeros((depth, bp, FFN_CONV - 1, f), F32))
    st_s = dict(lru_conv=state_lru_conv[0], lru_h=state_lru_h[0], swa_k=cache_swa_k[0], swa_v=cache_swa_v[0],
                ssd_conv=state_ssd_conv[0], ssd=state_ssd[0], mla_latent=cache_mla_latent, mla_rope=cache_mla_rope,
                page_table=page_table, ffn_conv=state_ffn_conv)
    pos_p = jnp.tile(jnp.arange(lp), bp)
    pos_s = jnp.tile(past_len + jnp.arange(ls), bs)
    y_p, sp = _trunk(x_prompt, pos_p, P, W, st_p, False)
    y_s, ss = _trunk(x_sample, pos_s, P, W, st_s, True)
    e = lambda a: a[None]
    return (y_p, y_s,
            e(sp["lru_conv"]), e(ss["lru_conv"]), e(sp["lru_h"]), e(ss["lru_h"]),
            e(sp["swa_k"]), e(ss["swa_k"]), e(sp["swa_v"]), e(ss["swa_v"]),
            e(sp["ssd_conv"]), e(ss["ssd_conv"]), e(sp["ssd"]), e(ss["ssd"]),
            e(sp["mla_latent"]), e(ss["mla_latent"]), e(sp["mla_rope"]), e(ss["mla_rope"]),
            sp["ffn_conv"], ss["ffn_conv"])
```

```python
import functools
import math

import jax
import jax.numpy as jnp
from jax import lax
from jax.experimental import pallas as pl
from jax.experimental.pallas import tpu as pltpu

F32 = jnp.float32
BF16 = jnp.bfloat16

EPS = 1e-6
LRU_C = 8.0
LRU_HEADS = 16
CONV_W = 4
SWA_HEADS = 8
SWA_KV_HEADS = 2
SWA_HEAD_DIM = 64
WINDOW = 128
SSD_HEADS = 16
SSD_HEAD_DIM = 64
SSD_GROUPS = 2
SSD_STATE = 128
SSD_CHUNK = 128
MLA_HEADS = 8
QK_NOPE = 64
QK_ROPE = 32
V_HEAD = 64
ROPE_THETA = 10000.0
PAGE_SIZE = 128
FFN_CONV = 3
NEG = -1e30
LOG2E = math.log2(math.e)
VT_ROWS = 80

VMEM_LIMIT = 56 * 1024 * 1024
SUBLANES = 8
LANES = 128


def _cparams(sem):
    return pltpu.CompilerParams(dimension_semantics=sem, vmem_limit_bytes=VMEM_LIMIT)


def _dot(a, b):
    return jnp.dot(a, b, preferred_element_type=F32)


def _dot_nt(a, b):
    return lax.dot_general(a, b, (((1,), (1,)), ((), ())), preferred_element_type=F32)


def _dot_tn(a, b):
    return lax.dot_general(a, b, (((0,), (0,)), ((), ())), preferred_element_type=F32)


def _sigmoid(x):
    return 1.0 / (1.0 + jnp.exp(-x))


def _silu(x):
    return x * _sigmoid(x)


def _softplus(x):
    return jnp.maximum(x, 0.0) + jnp.log1p(jnp.exp(-jnp.abs(x)))


def _gelu_tanh(x):
    return 0.5 * x * (1.0 + jnp.tanh(math.sqrt(2.0 / math.pi) * (x + 0.044715 * (x * x * x))))


def _rms(x, g):
    return x * lax.rsqrt(jnp.mean(x * x, axis=-1, keepdims=True) + EPS) * g


def _split3(x):
    x1 = x.astype(BF16)
    r1 = x - x1.astype(F32)
    x2 = r1.astype(BF16)
    x3 = (r1 - x2.astype(F32)).astype(BF16)
    return x1, x2, x3


def _const_spec(shape):
    nd = len(shape)
    return pl.BlockSpec(shape, lambda *_: (0,) * nd, pipeline_mode=pl.Buffered(1))


def _norm_proj_kernel(x_ref, g_ref, *refs, nts):
    n = len(nts)
    w_refs, o_refs = refs[:n], refs[n:]
    xn = _rms(x_ref[...], g_ref[...]).astype(BF16)
    for w_ref, o_ref, nt in zip(w_refs, o_refs, nts):
        if nt:
            o_ref[...] = _dot_nt(w_ref[...], xn)
        else:
            o_ref[...] = _dot(xn, w_ref[...])


def norm_proj(x2, g, ws, nts, tm=512):
    m, k = x2.shape
    tm = min(tm, m)
    in_specs = [pl.BlockSpec((tm, k), lambda i: (i, 0)), _const_spec((1, k))]
    out_shape, out_specs = [], []
    for w, nt in zip(ws, nts):
        in_specs.append(_const_spec(w.shape))
        if nt:
            out_shape.append(jax.ShapeDtypeStruct((w.shape[0], m), F32))
            out_specs.append(pl.BlockSpec((w.shape[0], tm), lambda i: (0, i)))
        else:
            out_shape.append(jax.ShapeDtypeStruct((m, w.shape[1]), F32))
            out_specs.append(pl.BlockSpec((tm, w.shape[1]), lambda i: (i, 0)))
    return pl.pallas_call(
        functools.partial(_norm_proj_kernel, nts=tuple(nts)),
        out_shape=out_shape, grid=(m // tm,), in_specs=in_specs, out_specs=out_specs,
        compiler_params=_cparams(("parallel",)), name="norm_proj",
    )(x2, g.reshape(1, k), *ws)


def _lru_kernel(xr_ref, gate_ref, hist_ref, h0_ref, cw_ref, cb_ref, wa_ref, ba_ref, wx_ref, bx_ref, lam_ref,
                rec_ref, conv_ref, hlast_ref, ext_ref, a_ref, b_ref, h_ref, hprev_ref, *, ns, tt, c, gw):
    t = pl.program_id(1)
    hk = CONV_W - 1

    @pl.when(t == 0)
    def _():
        ext_ref[:, SUBLANES - hk:SUBLANES, :] = hist_ref[...]
        hprev_ref[...] = h0_ref[...]

    ext_ref[:, SUBLANES:SUBLANES + tt, :] = xr_ref[...]
    xc = cb_ref[...]
    for k in range(CONV_W):
        xc = xc + ext_ref[:, SUBLANES - hk + k:SUBLANES - hk + k + tt, :] * cw_ref[k:k + 1, :]
    conv_ref[...] = ext_ref[:, SUBLANES + tt - hk:SUBLANES + tt, :]
    ext_ref[:, 0:SUBLANES, :] = ext_ref[:, tt:tt + SUBLANES, :]

    x2 = xc.reshape(ns * tt, c)
    ra, rx = [], []
    for j in range(c // gw):
        xg = x2[:, j * gw:(j + 1) * gw].astype(BF16)
        ra.append(_dot(xg, wa_ref[j]))
        rx.append(_dot(xg, wx_ref[j]))
    r = _sigmoid(jnp.concatenate(ra, axis=1) + ba_ref[...])
    ig = _sigmoid(jnp.concatenate(rx, axis=1) + bx_ref[...])
    log_a = (-LRU_C) * r * _softplus(-lam_ref[...])
    a = jnp.exp(log_a)
    b = jnp.sqrt(-jnp.tanh(log_a) * (a * a + 1.0)) * (ig * x2)

    a3 = a.reshape(ns * tt // SUBLANES, SUBLANES, c)
    b3 = b.reshape(ns * tt // SUBLANES, SUBLANES, c)
    row = lax.broadcasted_iota(jnp.int32, a3.shape, 1)
    d = 1
    while d < SUBLANES:
        a_sh = jnp.where(row >= d, pltpu.roll(a3, d, axis=1), 1.0)
        b_sh = jnp.where(row >= d, pltpu.roll(b3, d, axis=1), 0.0)
        b3 = a3 * b_sh + b3
        a3 = a3 * a_sh
        d *= 2
    a_ref[...] = a3.reshape(ns, tt, c)
    b_ref[...] = b3.reshape(ns, tt, c)

    def slab(j, hp):
        s = pl.multiple_of(j * SUBLANES, SUBLANES)
        h8 = a_ref[:, pl.ds(s, SUBLANES), :] * hp + b_ref[:, pl.ds(s, SUBLANES), :]
        h_ref[:, pl.ds(s, SUBLANES), :] = h8
        return h8[:, SUBLANES - 1:SUBLANES, :]

    hp = lax.fori_loop(0, tt // SUBLANES, slab, hprev_ref[...])
    hprev_ref[...] = hp
    hlast_ref[...] = hp
    rec_ref[...] = h_ref[...] * _gelu_tanh(gate_ref[...])


def lru(xr3, gate3, hist, h0, cw, cb, wa_bd, ba, wx_bd, bx, lam, ns, tt):
    nseq, length, c = xr3.shape
    gw = wa_bd.shape[-1]
    hk = CONV_W - 1
    grid = (nseq // ns, length // tt)
    blk = pl.BlockSpec((ns, tt, c), lambda s, t: (s, t, 0))
    vec = _const_spec((1, c))
    return pl.pallas_call(
        functools.partial(_lru_kernel, ns=ns, tt=tt, c=c, gw=gw),
        out_shape=[jax.ShapeDtypeStruct((nseq, length, c), F32),
                   jax.ShapeDtypeStruct((nseq, hk, c), F32),
                   jax.ShapeDtypeStruct((nseq, 1, c), F32)],
        grid=grid,
        in_specs=[blk, blk,
                  pl.BlockSpec((ns, hk, c), lambda s, t: (s, 0, 0)),
                  pl.BlockSpec((ns, 1, c), lambda s, t: (s, 0, 0)),
                  _const_spec((CONV_W, c)), vec,
                  _const_spec(wa_bd.shape), vec, _const_spec(wx_bd.shape), vec, vec],
        out_specs=[blk,
                   pl.BlockSpec((ns, hk, c), lambda s, t: (s, 0, 0)),
                   pl.BlockSpec((ns, 1, c), lambda s, t: (s, 0, 0))],
        scratch_shapes=[pltpu.VMEM((ns, SUBLANES + tt, c), F32),
                        pltpu.VMEM((ns, tt, c), F32), pltpu.VMEM((ns, tt, c), F32), pltpu.VMEM((ns, tt, c), F32),
                        pltpu.VMEM((ns, 1, c), F32)],
        compiler_params=_cparams(("arbitrary", "arbitrary")), name="lru",
    )(xr3, gate3, hist, h0, cw, cb.reshape(1, c), wa_bd, ba.reshape(1, c), wx_bd, bx.reshape(1, c), lam.reshape(1, c))


def _swa_prompt_kernel(sink_ref, q_ref, kc_ref, kp_ref, vc_ref, vp_ref, o_ref):
    i = pl.program_id(0)
    w, hd = WINDOW, SWA_HEAD_DIM
    grp = SWA_HEADS // SWA_KV_HEADS
    qi = lax.broadcasted_iota(jnp.int32, (w, w), 0)
    sj = lax.broadcasted_iota(jnp.int32, (w, w), 1)
    mask_cur = sj <= qi
    mask_prev = (sj >= qi) & (i > 0)
    scale = 1.0 / math.sqrt(hd)
    for kh in range(SWA_KV_HEADS):
        ls = slice(kh * hd, (kh + 1) * hd)
        kc, kp = kc_ref[:, ls].astype(BF16), kp_ref[:, ls].astype(BF16)
        vc, vp = vc_ref[:, ls].astype(BF16), vp_ref[:, ls].astype(BF16)
        for g in range(grp):
            h = kh * grp + g
            qh = (q_ref[:, h * hd:(h + 1) * hd] * scale).astype(BF16)
            s_c = jnp.where(mask_cur, _dot_nt(qh, kc), NEG)
            s_p = jnp.where(mask_prev, _dot_nt(qh, kp), NEG)
            sk = sink_ref[h]
            m = jnp.maximum(jnp.maximum(jnp.max(s_c, axis=1, keepdims=True), jnp.max(s_p, axis=1, keepdims=True)), sk)
            p_c = jnp.exp(s_c - m)
            p_p = jnp.exp(s_p - m)
            den = jnp.sum(p_c, axis=1, keepdims=True) + jnp.sum(p_p, axis=1, keepdims=True) + jnp.exp(sk - m)
            o = _dot(p_c.astype(BF16), vc) + _dot(p_p.astype(BF16), vp)
            o_ref[:, h * hd:(h + 1) * hd] = o / den


def swa_prompt(q, k, v, sink):
    length = q.shape[0]
    w = WINDOW
    kvw = SWA_KV_HEADS * SWA_HEAD_DIM
    cur = lambda i: (i, 0)
    prev = lambda i: (jnp.maximum(i - 1, 0), 0)
    return pl.pallas_call(
        _swa_prompt_kernel,
        out_shape=jax.ShapeDtypeStruct(q.shape, F32),
        grid=(length // w,),
        in_specs=[pl.BlockSpec(memory_space=pltpu.SMEM),
                  pl.BlockSpec((w, q.shape[1]), cur),
                  pl.BlockSpec((w, kvw), cur), pl.BlockSpec((w, kvw), prev),
                  pl.BlockSpec((w, kvw), cur), pl.BlockSpec((w, kvw), prev)],
        out_specs=pl.BlockSpec((w, q.shape[1]), cur),
        compiler_params=_cparams(("parallel",)), name="swa_prompt",
    )(sink, q, k, k, v, v)


def _swa_sample_kernel(sink_ref, q_ref, kn_ref, vn_ref, kb_ref, vb_ref, o_ref, ko_ref, vo_ref, kk_ref, vv_ref, *, bs, sl):
    w, hd = WINDOW, SWA_HEAD_DIM
    grp = SWA_HEADS // SWA_KV_HEADS
    nk = kk_ref.shape[0]
    scale = 1.0 / math.sqrt(hd)
    kk_ref[w + sl:nk, :] = jnp.zeros((nk - w - sl, kk_ref.shape[1]), F32)
    vv_ref[w + sl:nk, :] = jnp.zeros((nk - w - sl, vv_ref.shape[1]), F32)
    row = lax.broadcasted_iota(jnp.int32, (grp * sl, nk), 0)
    col = lax.broadcasted_iota(jnp.int32, (grp * sl, nk), 1)
    qi = row % sl
    mask = (col >= qi) & (col <= qi + w)
    rowc = lax.broadcasted_iota(jnp.int32, (grp * sl, 1), 0)

    def seq(b, carry):
        kb, kn = kb_ref[b], kn_ref[b]
        vb, vn = vb_ref[b], vn_ref[b]
        ko_ref[b, 0:w - sl, :] = kb_ref[b, sl:w, :]
        ko_ref[b, w - sl:w, :] = kn
        vo_ref[b, 0:w - sl, :] = vb_ref[b, sl:w, :]
        vo_ref[b, w - sl:w, :] = vn
        kk_ref[0:w, :] = kb
        kk_ref[w:w + sl, :] = kn
        vv_ref[0:w, :] = vb
        vv_ref[w:w + sl, :] = vn
        q = q_ref[b]
        for kh in range(SWA_KV_HEADS):
            ls = slice(kh * hd, (kh + 1) * hd)
            kkh = kk_ref[:, ls].astype(BF16)
            vvh = vv_ref[:, ls].astype(BF16)
            qs = jnp.concatenate([q[:, (kh * grp + g) * hd:(kh * grp + g + 1) * hd] for g in range(grp)], axis=0)
            s = jnp.where(mask, _dot_nt((qs * scale).astype(BF16), kkh), NEG)
            sk = jnp.zeros((grp * sl, 1), F32)
            for g in range(grp):
                sk = jnp.where(rowc // sl == g, sink_ref[kh * grp + g], sk)
            m = jnp.maximum(jnp.max(s, axis=1, keepdims=True), sk)
            p = jnp.exp(s - m)
            den = jnp.sum(p, axis=1, keepdims=True) + jnp.exp(sk - m)
            o = _dot(p.astype(BF16), vvh) / den
            for g in range(grp):
                h = kh * grp + g
                o_ref[b, :, h * hd:(h + 1) * hd] = o[g * sl:(g + 1) * sl, :]
        return carry

    lax.fori_loop(0, bs, seq, 0)


def swa_sample(q3, kn3, vn3, kbuf, vbuf, sink, bs=8):
    nseq, sl, qw = q3.shape
    w = WINDOW
    kvw = kn3.shape[-1]
    nk = 2 * w
    b3 = lambda shape: pl.BlockSpec(shape, lambda i: (i, 0, 0))
    return pl.pallas_call(
        functools.partial(_swa_sample_kernel, bs=bs, sl=sl),
        out_shape=[jax.ShapeDtypeStruct((nseq, sl, qw), F32),
                   jax.ShapeDtypeStruct((nseq, w, kvw), F32),
                   jax.ShapeDtypeStruct((nseq, w, kvw), F32)],
        grid=(nseq // bs,),
        in_specs=[pl.BlockSpec(memory_space=pltpu.SMEM),
                  b3((bs, sl, qw)), b3((bs, sl, kvw)), b3((bs, sl, kvw)), b3((bs, w, kvw)), b3((bs, w, kvw))],
        out_specs=[b3((bs, sl, qw)), b3((bs, w, kvw)), b3((bs, w, kvw))],
        scratch_shapes=[pltpu.VMEM((nk, kvw), F32), pltpu.VMEM((nk, kvw), F32)],
        compiler_params=_cparams(("arbitrary",)), name="swa_sample",
    )(sink, q3, kn3, vn3, kbuf, vbuf)


def _out_res_kernel(x_ref, a1_ref, w1_ref, a2_ref, w2_ref, o_ref, *, a2_t):
    a2 = a2_ref[...].astype(BF16)
    second = _dot_tn(a2, w2_ref[...]) if a2_t else _dot(a2, w2_ref[...])
    o_ref[...] = x_ref[...] + _dot(a1_ref[...].astype(BF16), w1_ref[...]) + second


def out_res(x2, a1, w1, a2, w2, a2_t=False, tm=512):
    m, d = x2.shape
    tm = min(tm, m)
    row = lambda width: pl.BlockSpec((tm, width), lambda i: (i, 0))
    a2_spec = pl.BlockSpec((a2.shape[0], tm), lambda i: (0, i)) if a2_t else row(a2.shape[1])
    return pl.pallas_call(
        functools.partial(_out_res_kernel, a2_t=a2_t),
        out_shape=jax.ShapeDtypeStruct((m, d), F32),
        grid=(m // tm,),
        in_specs=[row(d), row(a1.shape[1]), _const_spec(w1.shape), a2_spec, _const_spec(w2.shape)],
        out_specs=row(d),
        compiler_params=_cparams(("parallel",)), name="out_res",
    )(x2, a1, w1, a2, w2)


def _ffn_kernel(x_ref, g_ref, wg_ref, wu_ref, cw_ref, cb_ref, wd_ref, hist_ref, fg_ref,
                o_ref, hout_ref, ext_ref, acc_ref, *, ns, tt, d, f, tf, final):
    t = pl.program_id(1)
    hk = FFN_CONV - 1

    @pl.when(t == 0)
    def _():
        ext_ref[:, SUBLANES - hk:SUBLANES, :] = hist_ref[...]

    x = x_ref[...].reshape(ns * tt, d)
    xn = _rms(x, g_ref[...]).astype(BF16)
    for c in range(f // tf):
        sl = slice(c * tf, (c + 1) * tf)
        g3 = _dot(xn, wg_ref[:, sl]).reshape(ns, tt, tf)
        ext_ref[:, SUBLANES:SUBLANES + tt, sl] = g3
        gc = cb_ref[:, sl] + g3 * cw_ref[hk:hk + 1, sl]
        for k in range(hk):
            gc = gc + ext_ref[:, SUBLANES - hk + k:SUBLANES - hk + k + tt, sl] * cw_ref[k:k + 1, sl]
        u = _dot(xn, wu_ref[:, sl])
        act = (_silu(gc).reshape(ns * tt, tf) * u).astype(BF16)
        contrib = _dot(act, wd_ref[sl, :])
        if c == 0:
            acc_ref[...] = contrib
        else:
            acc_ref[...] += contrib
    y = x + acc_ref[...]
    if final:
        y = _rms(y, fg_ref[...])
    o_ref[...] = y.reshape(ns, tt, d)
    hout_ref[...] = ext_ref[:, SUBLANES + tt - hk:SUBLANES + tt, :]
    ext_ref[:, 0:SUBLANES, :] = ext_ref[:, tt:tt + SUBLANES, :]


def ffn(x3, g, wg, wu, cw, cb, wd, hist, fg, ns, tt, final, tf=256):
    nseq, length, d = x3.shape
    f = wg.shape[1]
    hk = FFN_CONV - 1
    blk = pl.BlockSpec((ns, tt, d), lambda s, t: (s, t, 0))
    hspec = pl.BlockSpec((ns, hk, f), lambda s, t: (s, 0, 0))
    return pl.pallas_call(
        functools.partial(_ffn_kernel, ns=ns, tt=tt, d=d, f=f, tf=tf, final=final),
        out_shape=[jax.ShapeDtypeStruct((nseq, length, d), F32), jax.ShapeDtypeStruct((nseq, hk, f), F32)],
        grid=(nseq // ns, length // tt),
        in_specs=[blk, _const_spec((1, d)), _const_spec(wg.shape), _const_spec(wu.shape),
                  _const_spec((FFN_CONV, f)), _const_spec((1, f)), _const_spec(wd.shape), hspec, _const_spec((1, d))],
        out_specs=[blk, hspec],
        scratch_shapes=[pltpu.VMEM((ns, SUBLANES + tt, f), F32), pltpu.VMEM((ns * tt, d), F32)],
        compiler_params=_cparams(("arbitrary", "arbitrary")), name="ffn",
    )(x3, g.reshape(1, d), wg, wu, cw, cb.reshape(1, f), wd, hist, fg.reshape(1, d))


def _ssd_kernel(z_ref, xbc_ref, dt_ref, dtt_ref, hist_ref, s0_ref, cw_ref, cb_ref, dtb_ref, dtbt_ref,
                alog_ref, alogt_ref, dvec_ref, nw_ref,
                y_ref, conv_ref, sout_ref, ext_ref, st_ref, ybuf_ref, dtp_ref, dttp_ref, *, tin, tc, inner, nst):
    c = pl.program_id(1)
    hk = CONV_W - 1
    nh, hp = SSD_HEADS, SSD_HEAD_DIM
    gh = nh // SSD_GROUPS

    @pl.when(c == 0)
    def _():
        if tin < tc:
            ext_ref[...] = jnp.zeros(ext_ref.shape, F32)
            dtp_ref[...] = jnp.zeros(dtp_ref.shape, F32)
            dttp_ref[...] = jnp.zeros(dttp_ref.shape, F32)
        ext_ref[SUBLANES - hk:SUBLANES, :] = hist_ref[0]
        st_ref[...] = s0_ref[0]

    ext_ref[SUBLANES:SUBLANES + tin, :] = xbc_ref[0]
    xc = cb_ref[...]
    for k in range(CONV_W):
        xc = xc + ext_ref[SUBLANES - hk + k:SUBLANES - hk + k + tc, :] * cw_ref[k:k + 1, :]
    conv_ref[0] = ext_ref[SUBLANES + tin - hk:SUBLANES + tin, :]
    if tin == tc:
        ext_ref[0:SUBLANES, :] = ext_ref[tc:tc + SUBLANES, :]
    xa = _silu(xc)
    xs = xa[:, :inner]
    bm = [xa[:, inner + g * nst:inner + (g + 1) * nst].astype(BF16) for g in range(SSD_GROUPS)]
    cm = [xa[:, inner + (SSD_GROUPS + g) * nst:inner + (SSD_GROUPS + g + 1) * nst].astype(BF16) for g in range(SSD_GROUPS)]

    if tin < tc:
        dtp_ref[0:tin, :] = dt_ref[0]
        dttp_ref[:, 0:tin] = dtt_ref[0]
        dt_raw, dtt_raw = dtp_ref[...], dttp_ref[...]
        valid_r = lax.broadcasted_iota(jnp.int32, (tc, nh), 0) < tin
        valid_c = lax.broadcasted_iota(jnp.int32, (nh, tc), 1) < tin
        dt = jnp.where(valid_r, _softplus(dt_raw + dtb_ref[...]), 0.0)
        dtt = jnp.where(valid_c, _softplus(dtt_raw + dtbt_ref[...]), 0.0)
    else:
        dt = _softplus(dt_ref[0] + dtb_ref[...])
        dtt = _softplus(dtt_ref[0] + dtbt_ref[...])
    da = dt * (-jnp.exp(alog_ref[...]))
    dat = dtt * (-jnp.exp(alogt_ref[...]))

    li = lax.broadcasted_iota(jnp.int32, (tc, tc), 0)
    si = lax.broadcasted_iota(jnp.int32, (tc, tc), 1)
    causal = li >= si
    tri = jnp.where(causal, 1.0, 0.0).astype(BF16)
    trit = jnp.where(li <= si, 1.0, 0.0).astype(BF16)
    cs = sum(_dot(tri, p) for p in _split3(da))
    cst = sum(_dot(p, trit) for p in _split3(dat))
    cs_last = cs[tc - 1:tc, :]
    cb_g = [_dot_nt(cm[g], bm[g]) for g in range(SSD_GROUPS)]

    for h in range(nh):
        g = h // gh
        ls = slice(h * hp, (h + 1) * hp)
        col = cs[:, h:h + 1]
        rowv = cst[h:h + 1, :]
        lmat = jnp.exp(jnp.where(causal, col - rowv, NEG))
        mh = (cb_g[g] * lmat).astype(BF16)
        xs_h = xs[:, ls]
        xd = xs_h * dt[:, h:h + 1]
        yh = _dot(mh, xd.astype(BF16))
        s_h = st_ref[h]
        yh = yh + jnp.exp(col) * _dot_nt(cm[g], s_h.astype(BF16))
        last = cs_last[:, h:h + 1]
        xdd = (xd * jnp.exp(last - col)).astype(BF16)
        st_ref[h] = jnp.exp(last) * s_h + _dot_tn(xdd, bm[g])
        ybuf_ref[:, ls] = yh + dvec_ref[:, ls] * xs_h

    y = ybuf_ref[0:tin, :] * _silu(z_ref[0])
    gwid = inner // SSD_GROUPS
    for g in range(SSD_GROUPS):
        gs = slice(g * gwid, (g + 1) * gwid)
        y_ref[0, :, gs] = _rms(y[:, gs], nw_ref[:, gs])
    sout_ref[0] = st_ref[...]


def ssd(z3, xbc3, dt3, dtt3, hist, s0, cw, cb, dtb, alog, dvec, nw, tin):
    nseq, length, inner = z3.shape
    cd = xbc3.shape[-1]
    nh, hp, nst = s0.shape[1:]
    tc = SSD_CHUNK
    hk = CONV_W - 1
    t3 = lambda width: pl.BlockSpec((1, tin, width), lambda s, c: (s, c, 0))
    per_seq = lambda shape: pl.BlockSpec((1,) + shape, lambda s, c: (s,) + (0,) * len(shape))
    return pl.pallas_call(
        functools.partial(_ssd_kernel, tin=tin, tc=tc, inner=inner, nst=nst),
        out_shape=[jax.ShapeDtypeStruct((nseq, length, inner), F32),
                   jax.ShapeDtypeStruct((nseq, hk, cd), F32),
                   jax.ShapeDtypeStruct(s0.shape, F32)],
        grid=(nseq, length // tin),
        in_specs=[t3(inner), t3(cd), t3(nh),
                  pl.BlockSpec((1, nh, tin), lambda s, c: (s, 0, c)),
                  per_seq((hk, cd)), per_seq((nh, hp, nst)),
                  _const_spec((CONV_W, cd)), _const_spec((1, cd)), _const_spec((1, nh)), _const_spec((nh, 1)),
                  _const_spec((1, nh)), _const_spec((nh, 1)), _const_spec((1, inner)), _const_spec((1, inner))],
        out_specs=[t3(inner), per_seq((hk, cd)), per_seq((nh, hp, nst))],
        scratch_shapes=[pltpu.VMEM((SUBLANES + tc, cd), F32), pltpu.VMEM((nh, hp, nst), F32),
                        pltpu.VMEM((tc, inner), F32), pltpu.VMEM((tc, nh), F32), pltpu.VMEM((nh, tc), F32)],
        compiler_params=_cparams(("arbitrary", "arbitrary")), name="ssd",
    )(z3, xbc3, dt3, dtt3, hist, s0, cw, cb.reshape(1, cd), dtb.reshape(1, nh), dtb.reshape(nh, 1),
      alog.reshape(1, nh), alog.reshape(nh, 1), dvec.reshape(1, inner), nw.reshape(1, inner))


def _rope128(t, ctab, stab):
    half = QK_ROPE // 2
    lane = lax.broadcasted_iota(jnp.int32, t.shape, 1)
    swapped = jnp.where(lane < QK_NOPE + half, pltpu.roll(t, LANES - half, axis=1), pltpu.roll(t, half, axis=1))
    return t * ctab + swapped * stab


def _mla_prep_kernel(cq_ref, ckv_ref, kr_ref, ct_ref, st_ref, qg_ref, kg_ref, wuq_ref, wk_ref, wv_ref,
                     lat_ref, rope_ref, *outs, sample):
    scale = 1.0 / math.sqrt(QK_NOPE + QK_ROPE)
    ctab, stab = ct_ref[...], st_ref[...]
    qf = _dot(_rms(cq_ref[...], qg_ref[...]).astype(BF16), wuq_ref[...])
    ckv = _rms(ckv_ref[...], kg_ref[...])
    lat_ref[...] = ckv
    krr = _rope128(kr_ref[...], ctab, stab)
    rope_ref[...] = krr[:, QK_NOPE:QK_NOPE + QK_ROPE]
    if sample:
        qlat_ref, qr_ref = outs
        for h in range(MLA_HEADS):
            qh = _rope128(qf[:, h * LANES:(h + 1) * LANES], ctab, stab) * scale
            qlat_ref[h] = _dot(qh[:, :QK_NOPE].astype(BF16), wk_ref[h])
            qr_ref[h] = qh[:, QK_NOPE:QK_NOPE + QK_ROPE]
    else:
        qc_ref, kc_ref, vt_ref = outs
        ckvb = ckv.astype(BF16)
        vrows = vt_ref.shape[1]
        ones_row = jnp.where(lax.broadcasted_iota(jnp.int32, (vrows, ckvb.shape[0]), 0) == V_HEAD, 1.0, 0.0)
        for h in range(MLA_HEADS):
            qh = _rope128(qf[:, h * LANES:(h + 1) * LANES], ctab, stab) * (scale * LOG2E)
            qc_ref[h] = qh.astype(BF16)
            kc_ref[h] = (_dot(ckvb, wk_ref[h]) + krr).astype(BF16)
            vt_ref[h] = (_dot_nt(wv_ref[h], ckvb) + ones_row).astype(BF16)


def mla_prep(cq, ckv_raw, krp, ctab, stab, qg, kg, wuq_pad, wk, wv, sample, tm=512):
    m = cq.shape[0]
    tm = min(tm, m)
    nh = MLA_HEADS
    row = lambda width: pl.BlockSpec((tm, width), lambda i: (i, 0))
    hrow = lambda width: pl.BlockSpec((nh, tm, width), lambda i: (0, i, 0))
    lq, lkv = cq.shape[1], ckv_raw.shape[1]
    out_shape = [jax.ShapeDtypeStruct((m, lkv), F32), jax.ShapeDtypeStruct((m, QK_ROPE), F32)]
    out_specs = [row(lkv), row(QK_ROPE)]
    if sample:
        out_shape += [jax.ShapeDtypeStruct((nh, m, lkv), F32), jax.ShapeDtypeStruct((nh, m, QK_ROPE), F32)]
        out_specs += [hrow(lkv), hrow(QK_ROPE)]
    else:
        out_shape += [jax.ShapeDtypeStruct((nh, m, LANES), BF16)] * 2 + [jax.ShapeDtypeStruct((nh, VT_ROWS, m), BF16)]
        out_specs += [hrow(LANES)] * 2 + [pl.BlockSpec((nh, VT_ROWS, tm), lambda i: (0, 0, i))]
    return pl.pallas_call(
        functools.partial(_mla_prep_kernel, sample=sample),
        out_shape=out_shape, grid=(m // tm,),
        in_specs=[row(lq), row(lkv), row(LANES), row(LANES), row(LANES), _const_spec((1, lq)), _const_spec((1, lkv)),
                  _const_spec(wuq_pad.shape), _const_spec(wk.shape), _const_spec(wv.shape)],
        out_specs=out_specs,
        compiler_params=_cparams(("parallel",)), name="mla_prep",
    )(cq, ckv_raw, krp, ctab, stab, qg.reshape(1, lq), kg.reshape(1, lkv), wuq_pad, wk, wv)


def _mla_flash_kernel(qi_ref, kj_ref, q_ref, k_ref, vt_ref, o_ref, m_ref, acc_ref, *, t):
    n = pl.program_id(0)
    i, j = qi_ref[n], kj_ref[n]

    @pl.when(j == 0)
    def _():
        m_ref[...] = jnp.full(m_ref.shape, NEG, F32)
        acc_ref[...] = jnp.zeros(acc_ref.shape, F32)

    def tile(masked):
        if masked:
            keep = lax.broadcasted_iota(jnp.int32, (t, t), 0) <= lax.broadcasted_iota(jnp.int32, (t, t), 1)
        nxt = _dot_nt(k_ref[0], q_ref[0])
        pending = None
        for h in range(MLA_HEADS + 1):
            if h < MLA_HEADS:
                st = nxt
                if h + 1 < MLA_HEADS:
                    nxt = _dot_nt(k_ref[h + 1], q_ref[h + 1])
                if masked:
                    st = jnp.where(keep, st, NEG)
                m_prev = m_ref[h]
                m_new = jnp.maximum(m_prev, jnp.max(st, axis=0, keepdims=True))
                alpha = jnp.exp2(m_prev - m_new)
                p = jnp.exp2(st - m_new).astype(BF16)
                m_ref[h] = m_new
            if pending is not None:
                hp, alpha_p, p_p = pending
                acc_ref[hp] = alpha_p * acc_ref[hp] + _dot(vt_ref[hp], p_p)
            pending = (h, alpha, p) if h < MLA_HEADS else None

    @pl.when(j < i)
    def _():
        tile(False)

    @pl.when(j == i)
    def _():
        tile(True)
        for h in range(MLA_HEADS):
            a = acc_ref[h]
            o_ref[h * V_HEAD:(h + 1) * V_HEAD, :] = a[0:V_HEAD, :] / a[V_HEAD:V_HEAD + 1, :]


def mla_flash(qc, kc, vt, t=512):
    nh, length, _ = qc.shape
    t = min(t, length)
    nb = length // t
    qi = [i for i in range(nb) for _ in range(i + 1)]
    kj = [j for i in range(nb) for j in range(i + 1)]
    grid_spec = pltpu.PrefetchScalarGridSpec(
        num_scalar_prefetch=2, grid=(len(qi),),
        in_specs=[pl.BlockSpec((nh, t, LANES), lambda n, qi, kj: (0, qi[n], 0)),
                  pl.BlockSpec((nh, t, LANES), lambda n, qi, kj: (0, kj[n], 0)),
                  pl.BlockSpec((nh, VT_ROWS, t), lambda n, qi, kj: (0, 0, kj[n]))],
        out_specs=pl.BlockSpec((nh * V_HEAD, t), lambda n, qi, kj: (0, qi[n])),
        scratch_shapes=[pltpu.VMEM((nh, 1, t), F32), pltpu.VMEM((nh, VT_ROWS, t), F32)])
    return pl.pallas_call(
        functools.partial(_mla_flash_kernel, t=t),
        out_shape=jax.ShapeDtypeStruct((nh * V_HEAD, length), F32),
        grid_spec=grid_spec,
        compiler_params=_cparams(("arbitrary",)), name="mla_flash",
    )(jnp.asarray(qi, jnp.int32), jnp.asarray(kj, jnp.int32), qc, kc, vt)


def _mla_sample_kernel(pt_ref, qlat_ref, qr_ref, ckv_ref, kr_ref, wuv_ref, *rest, pp, sl, npages):
    lat_refs, rope_refs = rest[:pp], rest[pp:2 * pp]
    o_ref, kbuf_ref, sbuf_ref, pbuf_ref, kl_ref, krl_ref = rest[2 * pp:]
    j, nj = pl.program_id(1), pl.num_programs(1)
    rows = MLA_HEADS * sl
    psz = lat_refs[0].shape[2]
    pair = 2 * psz
    npairs = npages // 2 + 1

    q = qlat_ref[...].reshape(rows, qlat_ref.shape[-1]).astype(BF16)
    qr = qr_ref[...].reshape(rows, qr_ref.shape[-1]).astype(BF16)

    for c2 in range(pp // 2):
        base = pl.multiple_of((j * pp + 2 * c2) * psz, pair)
        kbuf_ref[pl.ds(base, psz), :] = lat_refs[2 * c2][0, 0].astype(BF16)
        kbuf_ref[pl.ds(base + psz, psz), :] = lat_refs[2 * c2 + 1][0, 0].astype(BF16)
        rt = jnp.concatenate([rope_refs[2 * c2][0, 0], rope_refs[2 * c2 + 1][0, 0]], axis=1).astype(BF16)
        sbuf_ref[:, pl.ds(base, pair)] = _dot_nt(q, kbuf_ref[pl.ds(base, pair), :]) + _dot(qr, rt)

    @pl.when(j == nj - 1)
    def _():
        kl_ref[...] = jnp.zeros(kl_ref.shape, F32)
        krl_ref[...] = jnp.zeros(krl_ref.shape, F32)
        kl_ref[0:sl, :] = ckv_ref[0]
        krl_ref[0:sl, :] = kr_ref[0]
        kl = kl_ref[...].astype(BF16)
        kbuf_ref[npages * psz:npages * psz + pair, :] = kl
        s = _dot_nt(q, kl) + _dot_nt(qr, krl_ref[...].astype(BF16))
        row = lax.broadcasted_iota(jnp.int32, s.shape, 0)
        col = lax.broadcasted_iota(jnp.int32, s.shape, 1)
        sbuf_ref[:, npages * psz:npages * psz + pair] = jnp.where((col < sl) & (col <= row % sl), s, NEG)

        sall = sbuf_ref[...]
        m = jnp.max(sall, axis=1, keepdims=True)
        p = jnp.exp(sall - m)
        lsum = jnp.sum(p, axis=1, keepdims=True)
        pbuf_ref[...] = p.astype(BF16)
        acc = _dot(pbuf_ref[...], kbuf_ref[...])
        accn = (acc / lsum).astype(BF16)
        for h in range(MLA_HEADS):
            o_ref[0, :, h * V_HEAD:(h + 1) * V_HEAD] = _dot(accn[h * sl:(h + 1) * sl, :], wuv_ref[h])


def mla_sample(qlat, qr, ckv3, kr3, wuv, cache_lat, cache_rope_t, page_table, li, pp=16):
    nh, m, lkv = qlat.shape
    nseq, sl, rd = kr3.shape
    npages = page_table.shape[1]
    pp = min(pp, npages)
    psz = cache_lat.shape[2]
    npairs = npages // 2 + 1
    rows = nh * sl

    def lat_spec(c):
        return pl.BlockSpec((1, 1, psz, lkv), lambda b, j, pt: (li, pt[b, j * pp + c], 0, 0))

    def rope_spec(c):
        return pl.BlockSpec((1, 1, rd, psz), lambda b, j, pt: (li, pt[b, j * pp + c], 0, 0))

    in_specs = [pl.BlockSpec((nh, sl, lkv), lambda b, j, pt: (0, b, 0)),
                pl.BlockSpec((nh, sl, rd), lambda b, j, pt: (0, b, 0)),
                pl.BlockSpec((1, sl, lkv), lambda b, j, pt: (b, 0, 0)),
                pl.BlockSpec((1, sl, rd), lambda b, j, pt: (b, 0, 0)),
                pl.BlockSpec(wuv.shape, lambda b, j, pt: (0, 0, 0))]
    in_specs += [lat_spec(c) for c in range(pp)] + [rope_spec(c) for c in range(pp)]
    grid_spec = pltpu.PrefetchScalarGridSpec(
        num_scalar_prefetch=1, grid=(nseq, npages // pp), in_specs=in_specs,
        out_specs=pl.BlockSpec((1, sl, nh * V_HEAD), lambda b, j, pt: (b, 0, 0)),
        scratch_shapes=[pltpu.VMEM((npairs * 2 * psz, lkv), BF16), pltpu.VMEM((rows, npairs * 2 * psz), F32),
                        pltpu.VMEM((rows, npairs * 2 * psz), BF16),
                        pltpu.VMEM((2 * psz, lkv), F32), pltpu.VMEM((2 * psz, rd), F32)])
    return pl.pallas_call(
        functools.partial(_mla_sample_kernel, pp=pp, sl=sl, npages=npages),
        out_shape=jax.ShapeDtypeStruct((nseq, sl, nh * V_HEAD), F32),
        grid_spec=grid_spec,
        compiler_params=_cparams(("arbitrary", "arbitrary")), name="mla_sample",
    )(page_table, qlat, qr, ckv3, kr3, wuv, *([cache_lat] * pp), *([cache_rope_t] * pp))


def _block_diag(w, per):
    nh, b, _ = w.shape
    w4 = w.reshape(nh // per, per, b, b)
    eye = jnp.eye(per, dtype=w.dtype)
    return jnp.einsum("gaij,ab->gaibj", w4, eye).reshape(nh // per, per * b, per * b)


def _rope_tables(pos):
    half = QK_ROPE // 2
    inv = jnp.exp(-(math.log(ROPE_THETA) / half) * jnp.arange(half, dtype=F32))
    ang = pos.astype(F32)[:, None] * inv[None, :]
    cos, sin = jnp.cos(ang), jnp.sin(ang)
    n = pos.shape[0]
    pad = LANES - QK_NOPE - QK_ROPE
    ctab = jnp.concatenate([jnp.ones((n, QK_NOPE), F32), cos, cos, jnp.ones((n, pad), F32)], axis=1)
    stab = jnp.concatenate([jnp.zeros((n, QK_NOPE), F32), -sin, sin, jnp.zeros((n, pad), F32)], axis=1)
    return ctab, stab


def _prep_weights(P):
    w = {}
    lw = P["lru_conv_w"].shape[-1]
    qw = SWA_HEADS * SWA_HEAD_DIM
    kvw = SWA_KV_HEADS * SWA_HEAD_DIM
    ew = P["even_w_in"][0].astype(BF16)
    b = [0, lw, 2 * lw, 2 * lw + qw, 2 * lw + qw + kvw, 2 * lw + qw + 2 * kvw]
    w["even_in"] = [ew[:, b[i]:b[i + 1]] for i in range(5)]
    per = 256 // (lw // LRU_HEADS)
    w["wa_bd"] = _block_diag(P["lru_wa"][0], per).astype(BF16)
    w["wx_bd"] = _block_diag(P["lru_wx"][0], per).astype(BF16)
    eo = P["even_w_out"][0].astype(BF16)
    w["even_out"] = (eo[:lw], eo[lw:])

    inner = P["ssd_norm"].shape[-1]
    cd = P["ssd_conv_w"].shape[-1]
    lq = P["mla_q_norm"].shape[-1]
    lkv = P["mla_kv_norm"].shape[-1]
    ow = P["odd_w_in"][0]
    b = [0, inner, inner + cd, inner + cd + SSD_HEADS, inner + cd + SSD_HEADS + lq,
         inner + cd + SSD_HEADS + lq + lkv, inner + cd + SSD_HEADS + lq + lkv + QK_ROPE]
    parts = [ow[:, b[i]:b[i + 1]] for i in range(6)]
    pad = LANES - QK_NOPE - QK_ROPE
    kr_pad = jnp.pad(parts[5], ((0, 0), (QK_NOPE, pad)))
    w["odd_in"] = [parts[0].astype(BF16), parts[1].astype(BF16), parts[2].astype(BF16), parts[2].T.astype(BF16),
                   parts[3].astype(BF16), parts[4].astype(BF16), kr_pad.astype(BF16)]
    uq = P["mla_w_uq"][0].reshape(lq, MLA_HEADS, QK_NOPE + QK_ROPE)
    w["wuq_pad"] = jnp.pad(uq, ((0, 0), (0, 0), (0, pad))).reshape(lq, MLA_HEADS * LANES).astype(BF16)
    uk = P["mla_w_uk"][0]
    uv = P["mla_w_uv"][0]
    w["wuk_pad"] = jnp.pad(jnp.transpose(uk, (1, 0, 2)), ((0, 0), (0, 0), (0, LANES - QK_NOPE))).astype(BF16)
    w["wuk_t"] = jnp.transpose(uk, (1, 2, 0)).astype(BF16)
    w["wuv_t_pad"] = jnp.pad(jnp.transpose(uv, (1, 2, 0)), ((0, 0), (0, VT_ROWS - V_HEAD), (0, 0))).astype(BF16)
    w["wuv"] = jnp.transpose(uv, (1, 0, 2)).astype(BF16)
    oo = P["odd_w_out"][0].astype(BF16)
    w["odd_out"] = (oo[:inner], oo[inner:])
    w["dvec"] = jnp.repeat(P["ssd_d"][0], SSD_HEAD_DIM)
    w["ffn"] = [(P["ffn_w_gate"][l].astype(BF16), P["ffn_w_up"][l].astype(BF16), P["ffn_w_down"][l].astype(BF16))
                for l in range(P["ffn_w_gate"].shape[0])]
    return w


def _trunk(x3, pos, P, W, st, sample):
    nseq, length, d = x3.shape
    m = nseq * length
    lw = P["lru_conv_w"].shape[-1]
    kvw = SWA_KV_HEADS * SWA_HEAD_DIM
    if sample:
        seq_tile, time_tile = min(nseq, 64), length
    else:
        seq_tile, time_tile = 1, min(length, 512)
    out = {}

    xr, gate, q, k, v = norm_proj(x3.reshape(m, d), P["mix_norm"][0], W["even_in"], [False] * 5)
    lru_ns, lru_tt = (min(nseq, 32), length) if sample else (1, min(length, 256))
    rec, out["lru_conv"], h_last = lru(
        xr.reshape(nseq, length, lw), gate.reshape(nseq, length, lw), st["lru_conv"], st["lru_h"].reshape(nseq, 1, lw),
        P["lru_conv_w"][0], P["lru_conv_b"][0], W["wa_bd"], P["lru_ba"][0], W["wx_bd"], P["lru_bx"][0],
        P["lru_lambda"][0], lru_ns, lru_tt)
    out["lru_h"] = h_last.reshape(nseq, lw)
    if sample:
        att, sk, sv = swa_sample(q.reshape(nseq, length, -1), k.reshape(nseq, length, kvw), v.reshape(nseq, length, kvw),
                                 st["swa_k"].reshape(nseq, WINDOW, kvw), st["swa_v"].reshape(nseq, WINDOW, kvw),
                                 P["swa_sink"][0], bs=min(nseq, 8))
        att = att.reshape(m, -1)
        out["swa_k"] = sk.reshape(nseq, WINDOW, SWA_KV_HEADS, SWA_HEAD_DIM)
        out["swa_v"] = sv.reshape(nseq, WINDOW, SWA_KV_HEADS, SWA_HEAD_DIM)
    else:
        att = swa_prompt(q, k, v, P["swa_sink"][0])
        out["swa_k"] = k[-WINDOW:].reshape(1, WINDOW, SWA_KV_HEADS, SWA_HEAD_DIM)
        out["swa_v"] = v[-WINDOW:].reshape(1, WINDOW, SWA_KV_HEADS, SWA_HEAD_DIM)
    x2 = out_res(x3.reshape(m, d), rec.reshape(m, lw), W["even_out"][0], att, W["even_out"][1])
    wg, wu, wd = W["ffn"][0]
    x3, fc0 = ffn(x2.reshape(nseq, length, d), P["ffn_norm"][0], wg, wu, P["ffn_conv_w"][0], P["ffn_conv_b"][0], wd,
                  st["ffn_conv"][0], P["final_norm"], seq_tile, time_tile, final=False)

    z, xbc, dt, dtt, cq, ckv_raw, krp = norm_proj(x3.reshape(m, d), P["mix_norm"][1], W["odd_in"],
                                                   [False, False, False, True, False, False, False])
    inner = z.shape[-1]
    cd = xbc.shape[-1]
    dtt3 = jnp.transpose(dtt.reshape(SSD_HEADS, nseq, length), (1, 0, 2))
    y, out["ssd_conv"], out["ssd"] = ssd(
        z.reshape(nseq, length, inner), xbc.reshape(nseq, length, cd), dt.reshape(nseq, length, SSD_HEADS), dtt3,
        st["ssd_conv"], st["ssd"], P["ssd_conv_w"][0], P["ssd_conv_b"][0], P["ssd_dt_bias"][0], P["ssd_a_log"][0],
        W["dvec"], P["ssd_norm"][0], tin=min(length, SSD_CHUNK))
    ctab, stab = _rope_tables(pos)
    if sample:
        lat, rp, qlat, qr = mla_prep(cq, ckv_raw, krp, ctab, stab, P["mla_q_norm"][0], P["mla_kv_norm"][0],
                                     W["wuq_pad"], W["wuk_t"], W["wuv"], sample=True)
        att = mla_sample(qlat, qr, lat.reshape(nseq, length, -1), rp.reshape(nseq, length, -1), W["wuv"],
                         st["mla_latent"], jnp.swapaxes(st["mla_rope"], 2, 3), st["page_table"], 0).reshape(m, -1)
    else:
        lat, rp, qc, kc, vt = mla_prep(cq, ckv_raw, krp, ctab, stab, P["mla_q_norm"][0], P["mla_kv_norm"][0],
                                       W["wuq_pad"], W["wuk_pad"], W["wuv_t_pad"], sample=False)
        att = mla_flash(qc, kc, vt)
    out["mla_latent"] = lat.reshape(nseq, length, -1)
    out["mla_rope"] = rp.reshape(nseq, length, -1)
    x2 = out_res(x3.reshape(m, d), y.reshape(m, inner), W["odd_out"][0], att, W["odd_out"][1], a2_t=not sample)
    wg, wu, wd = W["ffn"][1]
    y3, fc1 = ffn(x2.reshape(nseq, length, d), P["ffn_norm"][1], wg, wu, P["ffn_conv_w"][1], P["ffn_conv_b"][1], wd,
                  st["ffn_conv"][1], P["final_norm"], seq_tile, time_tile, final=True)
    out["ffn_conv"] = jnp.stack([fc0, fc1])
    return y3, out


def kernel(x_prompt, x_sample, state_lru_conv, state_lru_h, cache_swa_k, cache_swa_v, state_ssd_conv, state_ssd,
           cache_mla_latent, cache_mla_rope, page_table, state_ffn_conv, mix_norm, ffn_norm, final_norm, even_w_in,
           lru_conv_w, lru_conv_b, lru_wa, lru_ba, lru_wx, lru_bx, lru_lambda, swa_sink, even_w_out, odd_w_in,
           ssd_conv_w, ssd_conv_b, ssd_dt_bias, ssd_a_log, ssd_d, ssd_norm, mla_q_norm, mla_w_uq, mla_kv_norm,
           mla_w_uk, mla_w_uv, odd_w_out, ffn_w_gate, ffn_w_up, ffn_conv_w, ffn_conv_b, ffn_w_down):
    P = dict(mix_norm=mix_norm, ffn_norm=ffn_norm, final_norm=final_norm, even_w_in=even_w_in,
             lru_conv_w=lru_conv_w, lru_conv_b=lru_conv_b, lru_wa=lru_wa, lru_ba=lru_ba, lru_wx=lru_wx,
             lru_bx=lru_bx, lru_lambda=lru_lambda, swa_sink=swa_sink, even_w_out=even_w_out,
             odd_w_in=odd_w_in, ssd_conv_w=ssd_conv_w, ssd_conv_b=ssd_conv_b, ssd_dt_bias=ssd_dt_bias,
             ssd_a_log=ssd_a_log, ssd_d=ssd_d, ssd_norm=ssd_norm, mla_q_norm=mla_q_norm,
             mla_w_uq=mla_w_uq, mla_kv_norm=mla_kv_norm, mla_w_uk=mla_w_uk, mla_w_uv=mla_w_uv,
             odd_w_out=odd_w_out, ffn_w_gate=ffn_w_gate, ffn_w_up=ffn_w_up, ffn_conv_w=ffn_conv_w,
             ffn_conv_b=ffn_conv_b, ffn_w_down=ffn_w_down)
    W = _prep_weights(P)
    bp, lp, d = x_prompt.shape
    bs, ls, _ = x_sample.shape
    depth = ffn_w_gate.shape[0]
    lw = lru_conv_w.shape[-1]
    cd = ssd_conv_w.shape[-1]
    f = ffn_w_gate.shape[-1]
    kvw = SWA_KV_HEADS * SWA_HEAD_DIM
    past_len = page_table.shape[1] * PAGE_SIZE

    st_p = dict(lru_conv=jnp.zeros((bp, CONV_W - 1, lw), F32), lru_h=jnp.zeros((bp, lw), F32),
                ssd_conv=jnp.zeros((bp, CONV_W - 1, cd), F32),
                ssd=jnp.zeros((bp, SSD_HEADS, SSD_HEAD_DIM, SSD_STATE), F32),
                ffn_conv=jnp.zeros((depth, bp, FFN_CONV - 1, f), F32))
    st_s = dict(lru_conv=state_lru_conv[0], lru_h=state_lru_h[0], swa_k=cache_swa_k[0], swa_v=cache_swa_v[0],
                ssd_conv=state_ssd_conv[0], ssd=state_ssd[0], mla_latent=cache_mla_latent, mla_rope=cache_mla_rope,
                page_table=page_table, ffn_conv=state_ffn_conv)
    pos_p = jnp.tile(jnp.arange(lp), bp)
    pos_s = jnp.tile(past_len + jnp.arange(ls), bs)
    y_p, sp = _trunk(x_prompt, pos_p, P, W, st_p, False)
    y_s, ss = _trunk(x_sample, pos_s, P, W, st_s, True)
    e = lambda a: a[None]
    return (y_p, y_s,
            e(sp["lru_conv"]), e(ss["lru_conv"]), e(sp["lru_h"]), e(ss["lru_h"]),
            e(sp["swa_k"]), e(ss["swa_k"]), e(sp["swa_v"]), e(ss["swa_v"]),
            e(sp["ssd_conv"]), e(ss["ssd_conv"]), e(sp["ssd"]), e(ss["ssd"]),
            e(sp["mla_latent"]), e(ss["mla_latent"]), e(sp["mla_rope"]), e(ss["mla_rope"]),
            sp["ffn_conv"], ss["ffn_conv"])
```

```python
import functools
import math

import jax
import jax.numpy as jnp
from jax import lax
from jax.experimental import pallas as pl
from jax.experimental.pallas import tpu as pltpu

F32 = jnp.float32
BF16 = jnp.bfloat16

EPS = 1e-6
LRU_C = 8.0
LRU_HEADS = 16
CONV_W = 4
SWA_HEADS = 8
SWA_KV_HEADS = 2
SWA_HEAD_DIM = 64
WINDOW = 128
SSD_HEADS = 16
SSD_HEAD_DIM = 64
SSD_GROUPS = 2
SSD_STATE = 128
SSD_CHUNK = 128
MLA_HEADS = 8
QK_NOPE = 64
QK_ROPE = 32
V_HEAD = 64
ROPE_THETA = 10000.0
PAGE_SIZE = 128
FFN_CONV = 3
NEG = -1e30
LOG2E = math.log2(math.e)
VT_ROWS = 80

VMEM_LIMIT = 56 * 1024 * 1024
SUBLANES = 8
LANES = 128


def _cparams(sem):
    return pltpu.CompilerParams(dimension_semantics=sem, vmem_limit_bytes=VMEM_LIMIT)


def _dot(a, b):
    return jnp.dot(a, b, preferred_element_type=F32)


def _dot_nt(a, b):
    return lax.dot_general(a, b, (((1,), (1,)), ((), ())), preferred_element_type=F32)


def _dot_tn(a, b):
    return lax.dot_general(a, b, (((0,), (0,)), ((), ())), preferred_element_type=F32)


def _sigmoid(x):
    return 1.0 / (1.0 + jnp.exp(-x))


def _silu(x):
    return x * _sigmoid(x)


def _softplus(x):
    return jnp.maximum(x, 0.0) + jnp.log1p(jnp.exp(-jnp.abs(x)))


def _gelu_tanh(x):
    return 0.5 * x * (1.0 + jnp.tanh(math.sqrt(2.0 / math.pi) * (x + 0.044715 * (x * x * x))))


def _rms(x, g):
    return x * lax.rsqrt(jnp.mean(x * x, axis=-1, keepdims=True) + EPS) * g


def _split3(x):
    x1 = x.astype(BF16)
    r1 = x - x1.astype(F32)
    x2 = r1.astype(BF16)
    x3 = (r1 - x2.astype(F32)).astype(BF16)
    return x1, x2, x3


def _const_spec(shape):
    nd = len(shape)
    return pl.BlockSpec(shape, lambda *_: (0,) * nd, pipeline_mode=pl.Buffered(1))


def _norm_proj_kernel(x_ref, g_ref, *refs, nts):
    n = len(nts)
    w_refs, o_refs = refs[:n], refs[n:]
    xn = _rms(x_ref[...], g_ref[...]).astype(BF16)
    for w_ref, o_ref, nt in zip(w_refs, o_refs, nts):
        if nt:
            o_ref[...] = _dot_nt(w_ref[...], xn)
        else:
            o_ref[...] = _dot(xn, w_ref[...])


def norm_proj(x2, g, ws, nts, tm=512):
    m, k = x2.shape
    tm = min(tm, m)
    in_specs = [pl.BlockSpec((tm, k), lambda i: (i, 0)), _const_spec((1, k))]
    out_shape, out_specs = [], []
    for w, nt in zip(ws, nts):
        in_specs.append(_const_spec(w.shape))
        if nt:
            out_shape.append(jax.ShapeDtypeStruct((w.shape[0], m), F32))
            out_specs.append(pl.BlockSpec((w.shape[0], tm), lambda i: (0, i)))
        else:
            out_shape.append(jax.ShapeDtypeStruct((m, w.shape[1]), F32))
            out_specs.append(pl.BlockSpec((tm, w.shape[1]), lambda i: (i, 0)))
    return pl.pallas_call(
        functools.partial(_norm_proj_kernel, nts=tuple(nts)),
        out_shape=out_shape, grid=(m // tm,), in_specs=in_specs, out_specs=out_specs,
        compiler_params=_cparams(("parallel",)), name="norm_proj",
    )(x2, g.reshape(1, k), *ws)


def _lru_kernel(xr_ref, gate_ref, hist_ref, h0_ref, cw_ref, cb_ref, wa_ref, ba_ref, wx_ref, bx_ref, lam_ref,
                rec_ref, conv_ref, hlast_ref, ext_ref, a_ref, b_ref, h_ref, hprev_ref, *, ns, tt, c, gw):
    t = pl.program_id(1)
    hk = CONV_W - 1

    @pl.when(t == 0)
    def _():
        ext_ref[:, SUBLANES - hk:SUBLANES, :] = hist_ref[...]
        hprev_ref[...] = h0_ref[...]

    ext_ref[:, SUBLANES:SUBLANES + tt, :] = xr_ref[...]
    xc = cb_ref[...]
    for k in range(CONV_W):
        xc = xc + ext_ref[:, SUBLANES - hk + k:SUBLANES - hk + k + tt, :] * cw_ref[k:k + 1, :]
    conv_ref[...] = ext_ref[:, SUBLANES + tt - hk:SUBLANES + tt, :]
    ext_ref[:, 0:SUBLANES, :] = ext_ref[:, tt:tt + SUBLANES, :]

    x2 = xc.reshape(ns * tt, c)
    ra, rx = [], []
    for j in range(c // gw):
        xg = x2[:, j * gw:(j + 1) * gw].astype(BF16)
        ra.append(_dot(xg, wa_ref[j]))
        rx.append(_dot(xg, wx_ref[j]))
    r = _sigmoid(jnp.concatenate(ra, axis=1) + ba_ref[...])
    ig = _sigmoid(jnp.concatenate(rx, axis=1) + bx_ref[...])
    log_a = (-LRU_C) * r * _softplus(-lam_ref[...])
    a = jnp.exp(log_a)
    b = jnp.sqrt(-jnp.tanh(log_a) * (a * a + 1.0)) * (ig * x2)

    a3 = a.reshape(ns * tt // SUBLANES, SUBLANES, c)
    b3 = b.reshape(ns * tt // SUBLANES, SUBLANES, c)
    row = lax.broadcasted_iota(jnp.int32, a3.shape, 1)
    d = 1
    while d < SUBLANES:
        a_sh = jnp.where(row >= d, pltpu.roll(a3, d, axis=1), 1.0)
        b_sh = jnp.where(row >= d, pltpu.roll(b3, d, axis=1), 0.0)
        b3 = a3 * b_sh + b3
        a3 = a3 * a_sh
        d *= 2
    a_ref[...] = a3.reshape(ns, tt, c)
    b_ref[...] = b3.reshape(ns, tt, c)

    def slab(j, hp):
        s = pl.multiple_of(j * SUBLANES, SUBLANES)
        h8 = a_ref[:, pl.ds(s, SUBLANES), :] * hp + b_ref[:, pl.ds(s, SUBLANES), :]
        h_ref[:, pl.ds(s, SUBLANES), :] = h8
        return h8[:, SUBLANES - 1:SUBLANES, :]

    hp = lax.fori_loop(0, tt // SUBLANES, slab, hprev_ref[...])
    hprev_ref[...] = hp
    hlast_ref[...] = hp
    rec_ref[...] = h_ref[...] * _gelu_tanh(gate_ref[...])


def lru(xr3, gate3, hist, h0, cw, cb, wa_bd, ba, wx_bd, bx, lam, ns, tt):
    nseq, length, c = xr3.shape
    gw = wa_bd.shape[-1]
    hk = CONV_W - 1
    grid = (nseq // ns, length // tt)
    blk = pl.BlockSpec((ns, tt, c), lambda s, t: (s, t, 0))
    vec = _const_spec((1, c))
    return pl.pallas_call(
        functools.partial(_lru_kernel, ns=ns, tt=tt, c=c, gw=gw),
        out_shape=[jax.ShapeDtypeStruct((nseq, length, c), F32),
                   jax.ShapeDtypeStruct((nseq, hk, c), F32),
                   jax.ShapeDtypeStruct((nseq, 1, c), F32)],
        grid=grid,
        in_specs=[blk, blk,
                  pl.BlockSpec((ns, hk, c), lambda s, t: (s, 0, 0)),
                  pl.BlockSpec((ns, 1, c), lambda s, t: (s, 0, 0)),
                  _const_spec((CONV_W, c)), vec,
                  _const_spec(wa_bd.shape), vec, _const_spec(wx_bd.shape), vec, vec],
        out_specs=[blk,
                   pl.BlockSpec((ns, hk, c), lambda s, t: (s, 0, 0)),
                   pl.BlockSpec((ns, 1, c), lambda s, t: (s, 0, 0))],
        scratch_shapes=[pltpu.VMEM((ns, SUBLANES + tt, c), F32),
                        pltpu.VMEM((ns, tt, c), F32), pltpu.VMEM((ns, tt, c), F32), pltpu.VMEM((ns, tt, c), F32),
                        pltpu.VMEM((ns, 1, c), F32)],
        compiler_params=_cparams(("arbitrary", "arbitrary")), name="lru",
    )(xr3, gate3, hist, h0, cw, cb.reshape(1, c), wa_bd, ba.reshape(1, c), wx_bd, bx.reshape(1, c), lam.reshape(1, c))


def _swa_prompt_kernel(sink_ref, q_ref, kc_ref, kp_ref, vtc_ref, vtp_ref, o_ref):
    i = pl.program_id(0)
    w, hd = WINDOW, SWA_HEAD_DIM
    grp = SWA_HEADS // SWA_KV_HEADS
    nq = grp * w
    key = lax.broadcasted_iota(jnp.int32, (2 * w, nq), 0)
    qpos = lax.broadcasted_iota(jnp.int32, (2 * w, nq), 1) % w
    keep = ((key < w) & (key >= qpos) & (i > 0)) | ((key >= w) & ((key - w) <= qpos))
    colh = lax.broadcasted_iota(jnp.int32, (1, nq), 1) // w
    scale = 1.0 / math.sqrt(hd)
    for kh in range(SWA_KV_HEADS):
        ls = slice(kh * hd, (kh + 1) * hd)
        kk = jnp.concatenate([kp_ref[:, ls], kc_ref[:, ls]], axis=0).astype(BF16)
        q4 = jnp.concatenate([q_ref[:, (kh * grp + g) * hd:(kh * grp + g + 1) * hd] for g in range(grp)], axis=0)
        st = jnp.where(keep, _dot_nt(kk, (q4 * scale).astype(BF16)), NEG)
        sk = jnp.zeros((1, nq), F32)
        for g in range(grp):
            sk = jnp.where(colh == g, sink_ref[kh * grp + g], sk)
        m = jnp.maximum(jnp.max(st, axis=0, keepdims=True), sk)
        p = jnp.exp(st - m)
        den = jnp.sum(p, axis=0, keepdims=True) + jnp.exp(sk - m)
        vt = jnp.concatenate([vtp_ref[ls, :], vtc_ref[ls, :]], axis=1).astype(BF16)
        ot = _dot(vt, p.astype(BF16)) / den
        for g in range(grp):
            h = kh * grp + g
            o_ref[h * hd:(h + 1) * hd, :] = ot[:, g * w:(g + 1) * w]


def swa_prompt(q, k, vt, sink):
    length, qw = q.shape
    w = WINDOW
    kvw = SWA_KV_HEADS * SWA_HEAD_DIM
    cur = lambda i: (i, 0)
    prev = lambda i: (jnp.maximum(i - 1, 0), 0)
    cur_t = lambda i: (0, i)
    prev_t = lambda i: (0, jnp.maximum(i - 1, 0))
    return pl.pallas_call(
        _swa_prompt_kernel,
        out_shape=jax.ShapeDtypeStruct((qw, length), F32),
        grid=(length // w,),
        in_specs=[pl.BlockSpec(memory_space=pltpu.SMEM),
                  pl.BlockSpec((w, qw), cur),
                  pl.BlockSpec((w, kvw), cur), pl.BlockSpec((w, kvw), prev),
                  pl.BlockSpec((kvw, w), cur_t), pl.BlockSpec((kvw, w), prev_t)],
        out_specs=pl.BlockSpec((qw, w), cur_t),
        compiler_params=_cparams(("parallel",)), name="swa_prompt",
    )(sink, q, k, k, vt, vt)


def _swa_sample_kernel(sink_ref, q_ref, kn_ref, vn_ref, kb_ref, vb_ref, o_ref, ko_ref, vo_ref, kk_ref, vv_ref, *, bs, sl):
    w, hd = WINDOW, SWA_HEAD_DIM
    grp = SWA_HEADS // SWA_KV_HEADS
    nk = kk_ref.shape[0]
    scale = 1.0 / math.sqrt(hd)
    kk_ref[w + sl:nk, :] = jnp.zeros((nk - w - sl, kk_ref.shape[1]), F32)
    vv_ref[w + sl:nk, :] = jnp.zeros((nk - w - sl, vv_ref.shape[1]), F32)
    row = lax.broadcasted_iota(jnp.int32, (grp * sl, nk), 0)
    col = lax.broadcasted_iota(jnp.int32, (grp * sl, nk), 1)
    qi = row % sl
    mask = (col >= qi) & (col <= qi + w)
    rowc = lax.broadcasted_iota(jnp.int32, (grp * sl, 1), 0)

    def seq(b, carry):
        kb, kn = kb_ref[b], kn_ref[b]
        vb, vn = vb_ref[b], vn_ref[b]
        ko_ref[b, 0:w - sl, :] = kb_ref[b, sl:w, :]
        ko_ref[b, w - sl:w, :] = kn
        vo_ref[b, 0:w - sl, :] = vb_ref[b, sl:w, :]
        vo_ref[b, w - sl:w, :] = vn
        kk_ref[0:w, :] = kb
        kk_ref[w:w + sl, :] = kn
        vv_ref[0:w, :] = vb
        vv_ref[w:w + sl, :] = vn
        q = q_ref[b]
        for kh in range(SWA_KV_HEADS):
            ls = slice(kh * hd, (kh + 1) * hd)
            kkh = kk_ref[:, ls].astype(BF16)
            vvh = vv_ref[:, ls].astype(BF16)
            qs = jnp.concatenate([q[:, (kh * grp + g) * hd:(kh * grp + g + 1) * hd] for g in range(grp)], axis=0)
            s = jnp.where(mask, _dot_nt((qs * scale).astype(BF16), kkh), NEG)
            sk = jnp.zeros((grp * sl, 1), F32)
            for g in range(grp):
                sk = jnp.where(rowc // sl == g, sink_ref[kh * grp + g], sk)
            m = jnp.maximum(jnp.max(s, axis=1, keepdims=True), sk)
            p = jnp.exp(s - m)
            den = jnp.sum(p, axis=1, keepdims=True) + jnp.exp(sk - m)
            o = _dot(p.astype(BF16), vvh) / den
            for g in range(grp):
                h = kh * grp + g
                o_ref[b, :, h * hd:(h + 1) * hd] = o[g * sl:(g + 1) * sl, :]
        return carry

    lax.fori_loop(0, bs, seq, 0)


def swa_sample(q3, kn3, vn3, kbuf, vbuf, sink, bs=8):
    nseq, sl, qw = q3.shape
    w = WINDOW
    kvw = kn3.shape[-1]
    nk = 2 * w
    b3 = lambda shape: pl.BlockSpec(shape, lambda i: (i, 0, 0))
    return pl.pallas_call(
        functools.partial(_swa_sample_kernel, bs=bs, sl=sl),
        out_shape=[jax.ShapeDtypeStruct((nseq, sl, qw), F32),
                   jax.ShapeDtypeStruct((nseq, w, kvw), F32),
                   jax.ShapeDtypeStruct((nseq, w, kvw), F32)],
        grid=(nseq // bs,),
        in_specs=[pl.BlockSpec(memory_space=pltpu.SMEM),
                  b3((bs, sl, qw)), b3((bs, sl, kvw)), b3((bs, sl, kvw)), b3((bs, w, kvw)), b3((bs, w, kvw))],
        out_specs=[b3((bs, sl, qw)), b3((bs, w, kvw)), b3((bs, w, kvw))],
        scratch_shapes=[pltpu.VMEM((nk, kvw), F32), pltpu.VMEM((nk, kvw), F32)],
        compiler_params=_cparams(("arbitrary",)), name="swa_sample",
    )(sink, q3, kn3, vn3, kbuf, vbuf)


def _out_res_kernel(x_ref, a1_ref, w1_ref, a2_ref, w2_ref, o_ref, *, a2_t):
    a2 = a2_ref[...].astype(BF16)
    second = _dot_tn(a2, w2_ref[...]) if a2_t else _dot(a2, w2_ref[...])
    o_ref[...] = x_ref[...] + _dot(a1_ref[...].astype(BF16), w1_ref[...]) + second


def out_res(x2, a1, w1, a2, w2, a2_t=False, tm=512):
    m, d = x2.shape
    tm = min(tm, m)
    row = lambda width: pl.BlockSpec((tm, width), lambda i: (i, 0))
    a2_spec = pl.BlockSpec((a2.shape[0], tm), lambda i: (0, i)) if a2_t else row(a2.shape[1])
    return pl.pallas_call(
        functools.partial(_out_res_kernel, a2_t=a2_t),
        out_shape=jax.ShapeDtypeStruct((m, d), F32),
        grid=(m // tm,),
        in_specs=[row(d), row(a1.shape[1]), _const_spec(w1.shape), a2_spec, _const_spec(w2.shape)],
        out_specs=row(d),
        compiler_params=_cparams(("parallel",)), name="out_res",
    )(x2, a1, w1, a2, w2)


def _ffn_kernel(x_ref, g_ref, wg_ref, wu_ref, cw_ref, cb_ref, wd_ref, hist_ref, fg_ref,
                o_ref, hout_ref, ext_ref, act_ref, *, ns, tt, d, f, tf, final):
    t = pl.program_id(1)
    hk = FFN_CONV - 1

    @pl.when(t == 0)
    def _():
        ext_ref[:, SUBLANES - hk:SUBLANES, :] = hist_ref[...]

    x = x_ref[...].reshape(ns * tt, d)
    xn = _rms(x, g_ref[...]).astype(BF16)
    for c in range(f // tf):
        sl = slice(c * tf, (c + 1) * tf)
        g3 = _dot(xn, wg_ref[:, sl]).reshape(ns, tt, tf)
        ext_ref[:, SUBLANES:SUBLANES + tt, sl] = g3
        gc = cb_ref[:, sl] + g3 * cw_ref[hk:hk + 1, sl]
        for k in range(hk):
            gc = gc + ext_ref[:, SUBLANES - hk + k:SUBLANES - hk + k + tt, sl] * cw_ref[k:k + 1, sl]
        u = _dot(xn, wu_ref[:, sl])
        act_ref[:, sl] = (_silu(gc).reshape(ns * tt, tf) * u).astype(BF16)
    y = x + _dot(act_ref[...], wd_ref[...])
    if final:
        y = _rms(y, fg_ref[...])
    o_ref[...] = y.reshape(ns, tt, d)
    hout_ref[...] = ext_ref[:, SUBLANES + tt - hk:SUBLANES + tt, :]
    ext_ref[:, 0:SUBLANES, :] = ext_ref[:, tt:tt + SUBLANES, :]


def ffn(x3, g, wg, wu, cw, cb, wd, hist, fg, ns, tt, final, tf=256):
    nseq, length, d = x3.shape
    f = wg.shape[1]
    hk = FFN_CONV - 1
    blk = pl.BlockSpec((ns, tt, d), lambda s, t: (s, t, 0))
    hspec = pl.BlockSpec((ns, hk, f), lambda s, t: (s, 0, 0))
    return pl.pallas_call(
        functools.partial(_ffn_kernel, ns=ns, tt=tt, d=d, f=f, tf=tf, final=final),
        out_shape=[jax.ShapeDtypeStruct((nseq, length, d), F32), jax.ShapeDtypeStruct((nseq, hk, f), F32)],
        grid=(nseq // ns, length // tt),
        in_specs=[blk, _const_spec((1, d)), _const_spec(wg.shape), _const_spec(wu.shape),
                  _const_spec((FFN_CONV, f)), _const_spec((1, f)), _const_spec(wd.shape), hspec, _const_spec((1, d))],
        out_specs=[blk, hspec],
        scratch_shapes=[pltpu.VMEM((ns, SUBLANES + tt, f), F32), pltpu.VMEM((ns * tt, f), BF16)],
        compiler_params=_cparams(("arbitrary", "arbitrary")), name="ffn",
    )(x3, g.reshape(1, d), wg, wu, cw, cb.reshape(1, f), wd, hist, fg.reshape(1, d))


def _ssd_kernel(z_ref, xbc_ref, dt_ref, dtt_ref, hist_ref, s0_ref, cw_ref, cb_ref, dtb_ref, dtbt_ref,
                alog_ref, alogt_ref, dvec_ref, nw_ref,
                y_ref, conv_ref, sout_ref, ext_ref, st_ref, ybuf_ref, dtp_ref, dttp_ref, *, tin, tc, inner, nst):
    c = pl.program_id(1)
    hk = CONV_W - 1
    nh, hp = SSD_HEADS, SSD_HEAD_DIM
    gh = nh // SSD_GROUPS

    @pl.when(c == 0)
    def _():
        if tin < tc:
            ext_ref[...] = jnp.zeros(ext_ref.shape, F32)
            dtp_ref[...] = jnp.zeros(dtp_ref.shape, F32)
            dttp_ref[...] = jnp.zeros(dttp_ref.shape, F32)
        ext_ref[SUBLANES - hk:SUBLANES, :] = hist_ref[0]
        st_ref[...] = s0_ref[0]

    ext_ref[SUBLANES:SUBLANES + tin, :] = xbc_ref[0]
    xc = cb_ref[...]
    for k in range(CONV_W):
        xc = xc + ext_ref[SUBLANES - hk + k:SUBLANES - hk + k + tc, :] * cw_ref[k:k + 1, :]
    conv_ref[0] = ext_ref[SUBLANES + tin - hk:SUBLANES + tin, :]
    if tin == tc:
        ext_ref[0:SUBLANES, :] = ext_ref[tc:tc + SUBLANES, :]
    xa = _silu(xc)
    xs = xa[:, :inner]
    bm = [xa[:, inner + g * nst:inner + (g + 1) * nst].astype(BF16) for g in range(SSD_GROUPS)]
    cm = [xa[:, inner + (SSD_GROUPS + g) * nst:inner + (SSD_GROUPS + g + 1) * nst].astype(BF16) for g in range(SSD_GROUPS)]

    if tin < tc:
        dtp_ref[0:tin, :] = dt_ref[0]
        dttp_ref[:, 0:tin] = dtt_ref[0]
        dt_raw, dtt_raw = dtp_ref[...], dttp_ref[...]
        valid_r = lax.broadcasted_iota(jnp.int32, (tc, nh), 0) < tin
        valid_c = lax.broadcasted_iota(jnp.int32, (nh, tc), 1) < tin
        dt = jnp.where(valid_r, _softplus(dt_raw + dtb_ref[...]), 0.0)
        dtt = jnp.where(valid_c, _softplus(dtt_raw + dtbt_ref[...]), 0.0)
    else:
        dt = _softplus(dt_ref[0] + dtb_ref[...])
        dtt = _softplus(dtt_ref[0] + dtbt_ref[...])
    da = dt * (-jnp.exp(alog_ref[...]))
    dat = dtt * (-jnp.exp(alogt_ref[...]))

    li = lax.broadcasted_iota(jnp.int32, (tc, tc), 0)
    si = lax.broadcasted_iota(jnp.int32, (tc, tc), 1)
    causal = li >= si
    tri = jnp.where(causal, 1.0, 0.0).astype(BF16)
    trit = jnp.where(li <= si, 1.0, 0.0).astype(BF16)
    cs = sum(_dot(tri, p) for p in _split3(da))
    cst = sum(_dot(p, trit) for p in _split3(dat))
    cs_last = cs[tc - 1:tc, :]
    cb_g = [_dot_nt(cm[g], bm[g]) for g in range(SSD_GROUPS)]

    for h in range(nh):
        g = h // gh
        ls = slice(h * hp, (h + 1) * hp)
        col = cs[:, h:h + 1]
        rowv = cst[h:h + 1, :]
        lmat = jnp.exp(jnp.where(causal, col - rowv, NEG))
        mh = (cb_g[g] * lmat).astype(BF16)
        xs_h = xs[:, ls]
        xd = xs_h * dt[:, h:h + 1]
        yh = _dot(mh, xd.astype(BF16))
        s_h = st_ref[h]
        yh = yh + jnp.exp(col) * _dot_nt(cm[g], s_h.astype(BF16))
        last = cs_last[:, h:h + 1]
        xdd = (xd * jnp.exp(last - col)).astype(BF16)
        st_ref[h] = jnp.exp(last) * s_h + _dot_tn(xdd, bm[g])
        ybuf_ref[:, ls] = yh + dvec_ref[:, ls] * xs_h

    y = ybuf_ref[0:tin, :] * _silu(z_ref[0])
    gwid = inner // SSD_GROUPS
    for g in range(SSD_GROUPS):
        gs = slice(g * gwid, (g + 1) * gwid)
        y_ref[0, :, gs] = _rms(y[:, gs], nw_ref[:, gs])
    sout_ref[0] = st_ref[...]


def ssd(z3, xbc3, dt3, dtt3, hist, s0, cw, cb, dtb, alog, dvec, nw, tin):
    nseq, length, inner = z3.shape
    cd = xbc3.shape[-1]
    nh, hp, nst = s0.shape[1:]
    tc = SSD_CHUNK
    hk = CONV_W - 1
    t3 = lambda width: pl.BlockSpec((1, tin, width), lambda s, c: (s, c, 0))
    per_seq = lambda shape: pl.BlockSpec((1,) + shape, lambda s, c: (s,) + (0,) * len(shape))
    return pl.pallas_call(
        functools.partial(_ssd_kernel, tin=tin, tc=tc, inner=inner, nst=nst),
        out_shape=[jax.ShapeDtypeStruct((nseq, length, inner), F32),
                   jax.ShapeDtypeStruct((nseq, hk, cd), F32),
                   jax.ShapeDtypeStruct(s0.shape, F32)],
        grid=(nseq, length // tin),
        in_specs=[t3(inner), t3(cd), t3(nh),
                  pl.BlockSpec((1, nh, tin), lambda s, c: (s, 0, c)),
                  per_seq((hk, cd)), per_seq((nh, hp, nst)),
                  _const_spec((CONV_W, cd)), _const_spec((1, cd)), _const_spec((1, nh)), _const_spec((nh, 1)),
                  _const_spec((1, nh)), _const_spec((nh, 1)), _const_spec((1, inner)), _const_spec((1, inner))],
        out_specs=[t3(inner), per_seq((hk, cd)), per_seq((nh, hp, nst))],
        scratch_shapes=[pltpu.VMEM((SUBLANES + tc, cd), F32), pltpu.VMEM((nh, hp, nst), F32),
                        pltpu.VMEM((tc, inner), F32), pltpu.VMEM((tc, nh), F32), pltpu.VMEM((nh, tc), F32)],
        compiler_params=_cparams(("arbitrary", "arbitrary")), name="ssd",
    )(z3, xbc3, dt3, dtt3, hist, s0, cw, cb.reshape(1, cd), dtb.reshape(1, nh), dtb.reshape(nh, 1),
      alog.reshape(1, nh), alog.reshape(nh, 1), dvec.reshape(1, inner), nw.reshape(1, inner))


def _rope128(t, ctab, stab):
    half = QK_ROPE // 2
    lane = lax.broadcasted_iota(jnp.int32, t.shape, 1)
    swapped = jnp.where(lane < QK_NOPE + half, pltpu.roll(t, LANES - half, axis=1), pltpu.roll(t, half, axis=1))
    return t * ctab + swapped * stab


def _mla_prep_kernel(cq_ref, ckv_ref, kr_ref, ct_ref, st_ref, qg_ref, kg_ref, wuq_ref, wk_ref, wv_ref,
                     lat_ref, rope_ref, *outs, sample):
    scale = 1.0 / math.sqrt(QK_NOPE + QK_ROPE)
    ctab, stab = ct_ref[...], st_ref[...]
    qf = _dot(_rms(cq_ref[...], qg_ref[...]).astype(BF16), wuq_ref[...])
    ckv = _rms(ckv_ref[...], kg_ref[...])
    lat_ref[...] = ckv
    krr = _rope128(kr_ref[...], ctab, stab)
    rope_ref[...] = krr[:, QK_NOPE:QK_NOPE + QK_ROPE]
    if sample:
        qlat_ref, qr_ref = outs
        for h in range(MLA_HEADS):
            qh = _rope128(qf[:, h * LANES:(h + 1) * LANES], ctab, stab) * scale
            qlat_ref[h] = _dot(qh[:, :QK_NOPE].astype(BF16), wk_ref[h])
            qr_ref[h] = qh[:, QK_NOPE:QK_NOPE + QK_ROPE]
    else:
        qc_ref, kc_ref, vt_ref = outs
        ckvb = ckv.astype(BF16)
        vrows = vt_ref.shape[1]
        ones_row = jnp.where(lax.broadcasted_iota(jnp.int32, (vrows, ckvb.shape[0]), 0) == V_HEAD, 1.0, 0.0)
        for h in range(MLA_HEADS):
            qh = _rope128(qf[:, h * LANES:(h + 1) * LANES], ctab, stab) * (scale * LOG2E)
            qc_ref[h] = qh.astype(BF16)
            kc_ref[h] = (_dot(ckvb, wk_ref[h]) + krr).astype(BF16)
            vt_ref[h] = (_dot_nt(wv_ref[h], ckvb) + ones_row).astype(BF16)


def mla_prep(cq, ckv_raw, krp, ctab, stab, qg, kg, wuq_pad, wk, wv, sample, tm=512):
    m = cq.shape[0]
    tm = min(tm, m)
    nh = MLA_HEADS
    row = lambda width: pl.BlockSpec((tm, width), lambda i: (i, 0))
    hrow = lambda width: pl.BlockSpec((nh, tm, width), lambda i: (0, i, 0))
    lq, lkv = cq.shape[1], ckv_raw.shape[1]
    out_shape = [jax.ShapeDtypeStruct((m, lkv), F32), jax.ShapeDtypeStruct((m, QK_ROPE), F32)]
    out_specs = [row(lkv), row(QK_ROPE)]
    if sample:
        out_shape += [jax.ShapeDtypeStruct((nh, m, lkv), F32), jax.ShapeDtypeStruct((nh, m, QK_ROPE), F32)]
        out_specs += [hrow(lkv), hrow(QK_ROPE)]
    else:
        out_shape += [jax.ShapeDtypeStruct((nh, m, LANES), BF16)] * 2 + [jax.ShapeDtypeStruct((nh, VT_ROWS, m), BF16)]
        out_specs += [hrow(LANES)] * 2 + [pl.BlockSpec((nh, VT_ROWS, tm), lambda i: (0, 0, i))]
    return pl.pallas_call(
        functools.partial(_mla_prep_kernel, sample=sample),
        out_shape=out_shape, grid=(m // tm,),
        in_specs=[row(lq), row(lkv), row(LANES), row(LANES), row(LANES), _const_spec((1, lq)), _const_spec((1, lkv)),
                  _const_spec(wuq_pad.shape), _const_spec(wk.shape), _const_spec(wv.shape)],
        out_specs=out_specs,
        compiler_params=_cparams(("parallel",)), name="mla_prep",
    )(cq, ckv_raw, krp, ctab, stab, qg.reshape(1, lq), kg.reshape(1, lkv), wuq_pad, wk, wv)


def _mla_flash_kernel(qi_ref, kj_ref, q_ref, k_ref, vt_ref, o_ref, m_ref, acc_ref, *, t):
    n = pl.program_id(0)
    i, j = qi_ref[n], kj_ref[n]

    @pl.when(j == 0)
    def _():
        m_ref[...] = jnp.full(m_ref.shape, NEG, F32)
        acc_ref[...] = jnp.zeros(acc_ref.shape, F32)

    def tile(masked):
        if masked:
            keep = lax.broadcasted_iota(jnp.int32, (t, t), 0) <= lax.broadcasted_iota(jnp.int32, (t, t), 1)
        nxt = _dot_nt(k_ref[0], q_ref[0])
        pending = None
        for h in range(MLA_HEADS + 1):
            if h < MLA_HEADS:
                st = nxt
                if h + 1 < MLA_HEADS:
                    nxt = _dot_nt(k_ref[h + 1], q_ref[h + 1])
                if masked:
                    st = jnp.where(keep, st, NEG)
                m_prev = m_ref[h]
                m_new = jnp.maximum(m_prev, jnp.max(st, axis=0, keepdims=True))
                alpha = jnp.exp2(m_prev - m_new)
                p = jnp.exp2(st - m_new).astype(BF16)
                m_ref[h] = m_new
            if pending is not None:
                hp, alpha_p, p_p = pending
                acc_ref[hp] = alpha_p * acc_ref[hp] + _dot(vt_ref[hp], p_p)
            pending = (h, alpha, p) if h < MLA_HEADS else None

    @pl.when(j < i)
    def _():
        tile(False)

    @pl.when(j == i)
    def _():
        tile(True)
        for h in range(MLA_HEADS):
            a = acc_ref[h]
            o_ref[h * V_HEAD:(h + 1) * V_HEAD, :] = a[0:V_HEAD, :] / a[V_HEAD:V_HEAD + 1, :]


def mla_flash(qc, kc, vt, t=512):
    nh, length, _ = qc.shape
    t = min(t, length)
    nb = length // t
    qi = [i for i in range(nb) for _ in range(i + 1)]
    kj = [j for i in range(nb) for j in range(i + 1)]
    grid_spec = pltpu.PrefetchScalarGridSpec(
        num_scalar_prefetch=2, grid=(len(qi),),
        in_specs=[pl.BlockSpec((nh, t, LANES), lambda n, qi, kj: (0, qi[n], 0)),
                  pl.BlockSpec((nh, t, LANES), lambda n, qi, kj: (0, kj[n], 0)),
                  pl.BlockSpec((nh, VT_ROWS, t), lambda n, qi, kj: (0, 0, kj[n]))],
        out_specs=pl.BlockSpec((nh * V_HEAD, t), lambda n, qi, kj: (0, qi[n])),
        scratch_shapes=[pltpu.VMEM((nh, 1, t), F32), pltpu.VMEM((nh, VT_ROWS, t), F32)])
    return pl.pallas_call(
        functools.partial(_mla_flash_kernel, t=t),
        out_shape=jax.ShapeDtypeStruct((nh * V_HEAD, length), F32),
        grid_spec=grid_spec,
        compiler_params=_cparams(("arbitrary",)), name="mla_flash",
    )(jnp.asarray(qi, jnp.int32), jnp.asarray(kj, jnp.int32), qc, kc, vt)


def _mla_sample_kernel(pt_ref, qlat_ref, qr_ref, ckv_ref, kr_ref, wuv_ref, lat_hbm, rope_hbm,
                       o_ref, latbuf, ropebuf, sem, kbuf_ref, sbuf_ref, pbuf_ref, kl_ref, krl_ref,
                       *, cp, sl, npages, li):
    b, nb = pl.program_id(0), pl.num_programs(0)
    rows = MLA_HEADS * sl
    psz = latbuf.shape[2]
    pair = 2 * psz
    nch = npages // cp

    def chunk_copies(seq, ch, slot):
        out = []
        for c in range(cp):
            pid = pt_ref[seq, ch * cp + c]
            out.append(pltpu.make_async_copy(lat_hbm.at[li, pid], latbuf.at[slot, c], sem.at[0, slot]))
            out.append(pltpu.make_async_copy(rope_hbm.at[li, pid], ropebuf.at[slot, c], sem.at[1, slot]))
        return out

    def start(seq, ch, slot):
        for cpy in chunk_copies(seq, ch, slot):
            cpy.start()

    def wait(seq, ch, slot):
        for cpy in chunk_copies(seq, ch, slot):
            cpy.wait()

    @pl.when(b == 0)
    def _():
        start(0, 0, 0)

    q = qlat_ref[...].reshape(rows, qlat_ref.shape[-1]).astype(BF16)
    qr = qr_ref[...].reshape(rows, qr_ref.shape[-1]).astype(BF16)

    for ch in range(nch):
        slot = ch % 2
        wait(b, ch, slot)
        if ch + 1 < nch:
            start(b, ch + 1, 1 - slot)
        else:
            @pl.when(b + 1 < nb)
            def _():
                start(b + 1, 0, 1 - slot)
        for c2 in range(cp // 2):
            base = (ch * cp + 2 * c2) * psz
            kbuf_ref[base:base + psz, :] = latbuf[slot, 2 * c2].astype(BF16)
            kbuf_ref[base + psz:base + pair, :] = latbuf[slot, 2 * c2 + 1].astype(BF16)
            rt = jnp.concatenate([ropebuf[slot, 2 * c2], ropebuf[slot, 2 * c2 + 1]], axis=1).astype(BF16)
            sbuf_ref[:, base:base + pair] = _dot_nt(q, kbuf_ref[base:base + pair, :]) + _dot(qr, rt)

    kl_ref[...] = jnp.zeros(kl_ref.shape, F32)
    krl_ref[...] = jnp.zeros(krl_ref.shape, F32)
    kl_ref[0:sl, :] = ckv_ref[0]
    krl_ref[0:sl, :] = kr_ref[0]
    kl = kl_ref[...].astype(BF16)
    kbuf_ref[npages * psz:npages * psz + pair, :] = kl
    s = _dot_nt(q, kl) + _dot_nt(qr, krl_ref[...].astype(BF16))
    row = lax.broadcasted_iota(jnp.int32, s.shape, 0)
    col = lax.broadcasted_iota(jnp.int32, s.shape, 1)
    sbuf_ref[:, npages * psz:npages * psz + pair] = jnp.where((col < sl) & (col <= row % sl), s, NEG)

    sall = sbuf_ref[...]
    m = jnp.max(sall, axis=1, keepdims=True)
    p = jnp.exp(sall - m)
    lsum = jnp.sum(p, axis=1, keepdims=True)
    pbuf_ref[...] = p.astype(BF16)
    acc = _dot(pbuf_ref[...], kbuf_ref[...])
    accn = (acc / lsum).astype(BF16)
    for h in range(MLA_HEADS):
        o_ref[0, :, h * V_HEAD:(h + 1) * V_HEAD] = _dot(accn[h * sl:(h + 1) * sl, :], wuv_ref[h])


def mla_sample(qlat, qr, ckv3, kr3, wuv, cache_lat, cache_rope_t, page_table, li, cp=16):
    nh, m, lkv = qlat.shape
    nseq, sl, rd = kr3.shape
    npages = page_table.shape[1]
    cp = min(cp, npages // 2)
    assert npages % (2 * cp) == 0 and cp % 2 == 0
    psz = cache_lat.shape[2]
    nkeys = npages * psz + 2 * psz
    rows = nh * sl
    in_specs = [pl.BlockSpec((nh, sl, lkv), lambda b, pt: (0, b, 0)),
                pl.BlockSpec((nh, sl, rd), lambda b, pt: (0, b, 0)),
                pl.BlockSpec((1, sl, lkv), lambda b, pt: (b, 0, 0)),
                pl.BlockSpec((1, sl, rd), lambda b, pt: (b, 0, 0)),
                pl.BlockSpec(wuv.shape, lambda b, pt: (0, 0, 0)),
                pl.BlockSpec(memory_space=pl.ANY), pl.BlockSpec(memory_space=pl.ANY)]
    grid_spec = pltpu.PrefetchScalarGridSpec(
        num_scalar_prefetch=1, grid=(nseq,), in_specs=in_specs,
        out_specs=pl.BlockSpec((1, sl, nh * V_HEAD), lambda b, pt: (b, 0, 0)),
        scratch_shapes=[pltpu.VMEM((2, cp, psz, lkv), F32), pltpu.VMEM((2, cp, rd, psz), F32),
                        pltpu.SemaphoreType.DMA((2, 2)),
                        pltpu.VMEM((nkeys, lkv), BF16), pltpu.VMEM((rows, nkeys), F32), pltpu.VMEM((rows, nkeys), BF16),
                        pltpu.VMEM((2 * psz, lkv), F32), pltpu.VMEM((2 * psz, rd), F32)])
    return pl.pallas_call(
        functools.partial(_mla_sample_kernel, cp=cp, sl=sl, npages=npages, li=li),
        out_shape=jax.ShapeDtypeStruct((nseq, sl, nh * V_HEAD), F32),
        grid_spec=grid_spec,
        compiler_params=_cparams(("arbitrary",)), name="mla_sample",
    )(page_table, qlat, qr, ckv3, kr3, wuv, cache_lat, cache_rope_t)


def _block_diag(w, per):
    nh, b, _ = w.shape
    w4 = w.reshape(nh // per, per, b, b)
    eye = jnp.eye(per, dtype=w.dtype)
    return jnp.einsum("gaij,ab->gaibj", w4, eye).reshape(nh // per, per * b, per * b)


def _rope_tables(pos):
    half = QK_ROPE // 2
    inv = jnp.exp(-(math.log(ROPE_THETA) / half) * jnp.arange(half, dtype=F32))
    ang = pos.astype(F32)[:, None] * inv[None, :]
    cos, sin = jnp.cos(ang), jnp.sin(ang)
    n = pos.shape[0]
    pad = LANES - QK_NOPE - QK_ROPE
    ctab = jnp.concatenate([jnp.ones((n, QK_NOPE), F32), cos, cos, jnp.ones((n, pad), F32)], axis=1)
    stab = jnp.concatenate([jnp.zeros((n, QK_NOPE), F32), -sin, sin, jnp.zeros((n, pad), F32)], axis=1)
    return ctab, stab


def _prep_weights(P):
    w = {}
    lw = P["lru_conv_w"].shape[-1]
    qw = SWA_HEADS * SWA_HEAD_DIM
    kvw = SWA_KV_HEADS * SWA_HEAD_DIM
    ew = P["even_w_in"][0].astype(BF16)
    b = [0, lw, 2 * lw, 2 * lw + qw, 2 * lw + qw + kvw, 2 * lw + qw + 2 * kvw]
    w["even_in"] = [ew[:, b[i]:b[i + 1]] for i in range(5)]
    per = 256 // (lw // LRU_HEADS)
    w["wa_bd"] = _block_diag(P["lru_wa"][0], per).astype(BF16)
    w["wx_bd"] = _block_diag(P["lru_wx"][0], per).astype(BF16)
    eo = P["even_w_out"][0].astype(BF16)
    w["even_out"] = (eo[:lw], eo[lw:])

    inner = P["ssd_norm"].shape[-1]
    cd = P["ssd_conv_w"].shape[-1]
    lq = P["mla_q_norm"].shape[-1]
    lkv = P["mla_kv_norm"].shape[-1]
    ow = P["odd_w_in"][0]
    b = [0, inner, inner + cd, inner + cd + SSD_HEADS, inner + cd + SSD_HEADS + lq,
         inner + cd + SSD_HEADS + lq + lkv, inner + cd + SSD_HEADS + lq + lkv + QK_ROPE]
    parts = [ow[:, b[i]:b[i + 1]] for i in range(6)]
    pad = LANES - QK_NOPE - QK_ROPE
    kr_pad = jnp.pad(parts[5], ((0, 0), (QK_NOPE, pad)))
    w["odd_in"] = [parts[0].astype(BF16), parts[1].astype(BF16), parts[2].astype(BF16), parts[2].T.astype(BF16),
                   parts[3].astype(BF16), parts[4].astype(BF16), kr_pad.astype(BF16)]
    uq = P["mla_w_uq"][0].reshape(lq, MLA_HEADS, QK_NOPE + QK_ROPE)
    w["wuq_pad"] = jnp.pad(uq, ((0, 0), (0, 0), (0, pad))).reshape(lq, MLA_HEADS * LANES).astype(BF16)
    uk = P["mla_w_uk"][0]
    uv = P["mla_w_uv"][0]
    w["wuk_pad"] = jnp.pad(jnp.transpose(uk, (1, 0, 2)), ((0, 0), (0, 0), (0, LANES - QK_NOPE))).astype(BF16)
    w["wuk_t"] = jnp.transpose(uk, (1, 2, 0)).astype(BF16)
    w["wuv_t_pad"] = jnp.pad(jnp.transpose(uv, (1, 2, 0)), ((0, 0), (0, VT_ROWS - V_HEAD), (0, 0))).astype(BF16)
    w["wuv"] = jnp.transpose(uv, (1, 0, 2)).astype(BF16)
    oo = P["odd_w_out"][0].astype(BF16)
    w["odd_out"] = (oo[:inner], oo[inner:])
    w["dvec"] = jnp.repeat(P["ssd_d"][0], SSD_HEAD_DIM)
    w["ffn"] = [(P["ffn_w_gate"][l].astype(BF16), P["ffn_w_up"][l].astype(BF16), P["ffn_w_down"][l].astype(BF16))
                for l in range(P["ffn_w_gate"].shape[0])]
    return w


def _trunk(x3, pos, P, W, st, sample):
    nseq, length, d = x3.shape
    m = nseq * length
    lw = P["lru_conv_w"].shape[-1]
    kvw = SWA_KV_HEADS * SWA_HEAD_DIM
    if sample:
        seq_tile, time_tile = min(nseq, 64), length
    else:
        seq_tile, time_tile = 1, min(length, 512)
    out = {}

    if sample:
        xr, gate, q, k, v = norm_proj(x3.reshape(m, d), P["mix_norm"][0], W["even_in"], [False] * 5)
    else:
        xr, gate, q, k, v, vt = norm_proj(x3.reshape(m, d), P["mix_norm"][0], W["even_in"] + [W["even_in"][4].T],
                                          [False] * 5 + [True])
    lru_ns, lru_tt = (min(nseq, 32), length) if sample else (1, min(length, 256))
    rec, out["lru_conv"], h_last = lru(
        xr.reshape(nseq, length, lw), gate.reshape(nseq, length, lw), st["lru_conv"], st["lru_h"].reshape(nseq, 1, lw),
        P["lru_conv_w"][0], P["lru_conv_b"][0], W["wa_bd"], P["lru_ba"][0], W["wx_bd"], P["lru_bx"][0],
        P["lru_lambda"][0], lru_ns, lru_tt)
    out["lru_h"] = h_last.reshape(nseq, lw)
    if sample:
        att, sk, sv = swa_sample(q.reshape(nseq, length, -1), k.reshape(nseq, length, kvw), v.reshape(nseq, length, kvw),
                                 st["swa_k"].reshape(nseq, WINDOW, kvw), st["swa_v"].reshape(nseq, WINDOW, kvw),
                                 P["swa_sink"][0], bs=min(nseq, 8))
        att = att.reshape(m, -1)
        out["swa_k"] = sk.reshape(nseq, WINDOW, SWA_KV_HEADS, SWA_HEAD_DIM)
        out["swa_v"] = sv.reshape(nseq, WINDOW, SWA_KV_HEADS, SWA_HEAD_DIM)
    else:
        att = swa_prompt(q, k, vt, P["swa_sink"][0])
        out["swa_k"] = k[-WINDOW:].reshape(1, WINDOW, SWA_KV_HEADS, SWA_HEAD_DIM)
        out["swa_v"] = v[-WINDOW:].reshape(1, WINDOW, SWA_KV_HEADS, SWA_HEAD_DIM)
    x2 = out_res(x3.reshape(m, d), rec.reshape(m, lw), W["even_out"][0], att, W["even_out"][1], a2_t=not sample)
    wg, wu, wd = W["ffn"][0]
    x3, fc0 = ffn(x2.reshape(nseq, length, d), P["ffn_norm"][0], wg, wu, P["ffn_conv_w"][0], P["ffn_conv_b"][0], wd,
                  st["ffn_conv"][0], P["final_norm"], seq_tile, time_tile, final=False)

    z, xbc, dt, dtt, cq, ckv_raw, krp = norm_proj(x3.reshape(m, d), P["mix_norm"][1], W["odd_in"],
                                                   [False, False, False, True, False, False, False])
    inner = z.shape[-1]
    cd = xbc.shape[-1]
    dtt3 = jnp.transpose(dtt.reshape(SSD_HEADS, nseq, length), (1, 0, 2))
    y, out["ssd_conv"], out["ssd"] = ssd(
        z.reshape(nseq, length, inner), xbc.reshape(nseq, length, cd), dt.reshape(nseq, length, SSD_HEADS), dtt3,
        st["ssd_conv"], st["ssd"], P["ssd_conv_w"][0], P["ssd_conv_b"][0], P["ssd_dt_bias"][0], P["ssd_a_log"][0],
        W["dvec"], P["ssd_norm"][0], tin=min(length, SSD_CHUNK))
    ctab, stab = _rope_tables(pos)
    if sample:
        lat, rp, qlat, qr = mla_prep(cq, ckv_raw, krp, ctab, stab, P["mla_q_norm"][0], P["mla_kv_norm"][0],
                                     W["wuq_pad"], W["wuk_t"], W["wuv"], sample=True)
        att = mla_sample(qlat, qr, lat.reshape(nseq, length, -1), rp.reshape(nseq, length, -1), W["wuv"],
                         st["mla_latent"], jnp.swapaxes(st["mla_rope"], 2, 3), st["page_table"], 0).reshape(m, -1)
    else:
        lat, rp, qc, kc, vt = mla_prep(cq, ckv_raw, krp, ctab, stab, P["mla_q_norm"][0], P["mla_kv_norm"][0],
                                       W["wuq_pad"], W["wuk_pad"], W["wuv_t_pad"], sample=False)
        att = mla_flash(qc, kc, vt)
    out["mla_latent"] = lat.reshape(nseq, length, -1)
    out["mla_rope"] = rp.reshape(nseq, length, -1)
    x2 = out_res(x3.reshape(m, d), y.reshape(m, inner), W["odd_out"][0], att, W["odd_out"][1], a2_t=not sample)
    wg, wu, wd = W["ffn"][1]
    y3, fc1 = ffn(x2.reshape(nseq, length, d), P["ffn_norm"][1], wg, wu, P["ffn_conv_w"][1], P["ffn_conv_b"][1], wd,
                  st["ffn_conv"][1], P["final_norm"], seq_tile, time_tile, final=True)
    out["ffn_conv"] = jnp.stack([fc0, fc1])
    return y3, out


def kernel(x_prompt, x_sample, state_lru_conv, state_lru_h, cache_swa_k, cache_swa_v, state_ssd_conv, state_ssd,
           cache_mla_latent, cache_mla_rope, page_table, state_ffn_conv, mix_norm, ffn_norm, final_norm, even_w_in,
           lru_conv_w, lru_conv_b, lru_wa, lru_ba, lru_wx, lru_bx, lru_lambda, swa_sink, even_w_out, odd_w_in,
           ssd_conv_w, ssd_conv_b, ssd_dt_bias, ssd_a_log, ssd_d, ssd_norm, mla_q_norm, mla_w_uq, mla_kv_norm,
           mla_w_uk, mla_w_uv, odd_w_out, ffn_w_gate, ffn_w_up, ffn_conv_w, ffn_conv_b, ffn_w_down):
    P = dict(mix_norm=mix_norm, ffn_norm=ffn_norm, final_norm=final_norm, even_w_in=even_w_in,
             lru_conv_w=lru_conv_w, lru_conv_b=lru_conv_b, lru_wa=lru_wa, lru_ba=lru_ba, lru_wx=lru_wx,
             lru_bx=lru_bx, lru_lambda=lru_lambda, swa_sink=swa_sink, even_w_out=even_w_out,
             odd_w_in=odd_w_in, ssd_conv_w=ssd_conv_w, ssd_conv_b=ssd_conv_b, ssd_dt_bias=ssd_dt_bias,
             ssd_a_log=ssd_a_log, ssd_d=ssd_d, ssd_norm=ssd_norm, mla_q_norm=mla_q_norm,
             mla_w_uq=mla_w_uq, mla_kv_norm=mla_kv_norm, mla_w_uk=mla_w_uk, mla_w_uv=mla_w_uv,
             odd_w_out=odd_w_out, ffn_w_gate=ffn_w_gate, ffn_w_up=ffn_w_up, ffn_conv_w=ffn_conv_w,
             ffn_conv_b=ffn_conv_b, ffn_w_down=ffn_w_down)
    W = _prep_weights(P)
    bp, lp, d = x_prompt.shape
    bs, ls, _ = x_sample.shape
    depth = ffn_w_gate.shape[0]
    lw = lru_conv_w.shape[-1]
    cd = ssd_conv_w.shape[-1]
    f = ffn_w_gate.shape[-1]
    kvw = SWA_KV_HEADS * SWA_HEAD_DIM
    past_len = page_table.shape[1] * PAGE_SIZE

    st_p = dict(lru_conv=jnp.zeros((bp, CONV_W - 1, lw), F32), lru_h=jnp.zeros((bp, lw), F32),
                ssd_conv=jnp.zeros((bp, CONV_W - 1, cd), F32),
                ssd=jnp.zeros((bp, SSD_HEADS, SSD_HEAD_DIM, SSD_STATE), F32),
                ffn_conv=jnp.zeros((depth, bp, FFN_CONV - 1, f), F32))
    st_s = dict(lru_conv=state_lru_conv[0], lru_h=state_lru_h[0], swa_k=cache_swa_k[0], swa_v=cache_swa_v[0],
                ssd_conv=state_ssd_conv[0], ssd=state_ssd[0], mla_latent=cache_mla_latent, mla_rope=cache_mla_rope,
                page_table=page_table, ffn_conv=state_ffn_conv)
    pos_p = jnp.tile(jnp.arange(lp), bp)
    pos_s = jnp.tile(past_len + jnp.arange(ls), bs)
    y_p, sp = _trunk(x_prompt, pos_p, P, W, st_p, False)
    y_s, ss = _trunk(x_sample, pos_s, P, W, st_s, True)
    e = lambda a: a[None]
    return (y_p, y_s,
            e(sp["lru_conv"]), e(ss["lru_conv"]), e(sp["lru_h"]), e(ss["lru_h"]),
            e(sp["swa_k"]), e(ss["swa_k"]), e(sp["swa_v"]), e(ss["swa_v"]),
            e(sp["ssd_conv"]), e(ss["ssd_conv"]), e(sp["ssd"]), e(ss["ssd"]),
            e(sp["mla_latent"]), e(ss["mla_latent"]), e(sp["mla_rope"]), e(ss["mla_rope"]),
            sp["ffn_conv"], ss["ffn_conv"])
```

```python
import functools
import math

import jax
import jax.numpy as jnp
from jax import lax
from jax.experimental import pallas as pl
from jax.experimental.pallas import tpu as pltpu

F32 = jnp.float32
BF16 = jnp.bfloat16

EPS = 1e-6
LRU_C = 8.0
LRU_HEADS = 16
CONV_W = 4
SWA_HEADS = 8
SWA_KV_HEADS = 2
SWA_HEAD_DIM = 64
WINDOW = 128
SSD_HEADS = 16
SSD_HEAD_DIM = 64
SSD_GROUPS = 2
SSD_STATE = 128
SSD_CHUNK = 128
MLA_HEADS = 8
QK_NOPE = 64
QK_ROPE = 32
V_HEAD = 64
ROPE_THETA = 10000.0
PAGE_SIZE = 128
FFN_CONV = 3
NEG = -1e30
LOG2E = math.log2(math.e)
VT_ROWS = 80
MLA_PAGE_SLOTS = 4

VMEM_LIMIT = 56 * 1024 * 1024
SUBLANES = 8
BF16_SUBLANES = 16
LANES = 128


def _cparams(sem):
    return pltpu.CompilerParams(dimension_semantics=sem, vmem_limit_bytes=VMEM_LIMIT)


def _dot(a, b):
    return jnp.dot(a, b, preferred_element_type=F32)


def _dot_nt(a, b):
    return lax.dot_general(a, b, (((1,), (1,)), ((), ())), preferred_element_type=F32)


def _dot_tn(a, b):
    return lax.dot_general(a, b, (((0,), (0,)), ((), ())), preferred_element_type=F32)


def _sigmoid(x):
    return 1.0 / (1.0 + jnp.exp(-x))


def _silu(x):
    return x * _sigmoid(x)


def _softplus(x):
    return jnp.maximum(x, 0.0) + jnp.log1p(jnp.exp(-jnp.abs(x)))


def _gelu_tanh(x):
    return 0.5 * x * (1.0 + jnp.tanh(math.sqrt(2.0 / math.pi) * (x + 0.044715 * (x * x * x))))


def _rms(x, g):
    return x * lax.rsqrt(jnp.mean(x * x, axis=-1, keepdims=True) + EPS) * g


def _split3(x):
    x1 = x.astype(BF16)
    r1 = x - x1.astype(F32)
    x2 = r1.astype(BF16)
    x3 = (r1 - x2.astype(F32)).astype(BF16)
    return x1, x2, x3


def _const_spec(shape):
    nd = len(shape)
    return pl.BlockSpec(shape, lambda *_: (0,) * nd, pipeline_mode=pl.Buffered(1))


def _norm_proj_kernel(x_ref, g_ref, *refs, nts):
    n = len(nts)
    w_refs, o_refs = refs[:n], refs[n:]
    xn = _rms(x_ref[...], g_ref[...]).astype(BF16)
    for w_ref, o_ref, nt in zip(w_refs, o_refs, nts):
        if nt:
            o_ref[...] = _dot_nt(w_ref[...], xn)
        else:
            o_ref[...] = _dot(xn, w_ref[...])


def norm_proj(x2, g, ws, nts, tm=512):
    m, k = x2.shape
    tm = min(tm, m)
    in_specs = [pl.BlockSpec((tm, k), lambda i: (i, 0)), _const_spec((1, k))]
    out_shape, out_specs = [], []
    for w, nt in zip(ws, nts):
        in_specs.append(_const_spec(w.shape))
        if nt:
            out_shape.append(jax.ShapeDtypeStruct((w.shape[0], m), F32))
            out_specs.append(pl.BlockSpec((w.shape[0], tm), lambda i: (0, i)))
        else:
            out_shape.append(jax.ShapeDtypeStruct((m, w.shape[1]), F32))
            out_specs.append(pl.BlockSpec((tm, w.shape[1]), lambda i: (i, 0)))
    return pl.pallas_call(
        functools.partial(_norm_proj_kernel, nts=tuple(nts)),
        out_shape=out_shape, grid=(m // tm,), in_specs=in_specs, out_specs=out_specs,
        compiler_params=_cparams(("parallel",)), name="norm_proj",
    )(x2, g.reshape(1, k), *ws)


def _lru_kernel(xr_ref, gate_ref, hist_ref, h0_ref, cw_ref, cb_ref, wa_ref, ba_ref, wx_ref, bx_ref, lam_ref,
                rec_ref, conv_ref, hlast_ref, ext_ref, a_ref, b_ref, h_ref, hprev_ref, *, ns, tt, c, gw):
    t = pl.program_id(1)
    hk = CONV_W - 1

    @pl.when(t == 0)
    def _():
        ext_ref[:, SUBLANES - hk:SUBLANES, :] = hist_ref[...]
        hprev_ref[...] = h0_ref[...]

    ext_ref[:, SUBLANES:SUBLANES + tt, :] = xr_ref[...]
    xc = cb_ref[...]
    for k in range(CONV_W):
        xc = xc + ext_ref[:, SUBLANES - hk + k:SUBLANES - hk + k + tt, :] * cw_ref[k:k + 1, :]
    conv_ref[...] = ext_ref[:, SUBLANES + tt - hk:SUBLANES + tt, :]
    ext_ref[:, 0:SUBLANES, :] = ext_ref[:, tt:tt + SUBLANES, :]

    x2 = xc.reshape(ns * tt, c)
    ra, rx = [], []
    for j in range(c // gw):
        xg = x2[:, j * gw:(j + 1) * gw].astype(BF16)
        ra.append(_dot(xg, wa_ref[j]))
        rx.append(_dot(xg, wx_ref[j]))
    r = _sigmoid(jnp.concatenate(ra, axis=1) + ba_ref[...])
    ig = _sigmoid(jnp.concatenate(rx, axis=1) + bx_ref[...])
    log_a = (-LRU_C) * r * _softplus(-lam_ref[...])
    a = jnp.exp(log_a)
    b = jnp.sqrt(-jnp.tanh(log_a) * (a * a + 1.0)) * (ig * x2)

    a3 = a.reshape(ns * tt // SUBLANES, SUBLANES, c)
    b3 = b.reshape(ns * tt // SUBLANES, SUBLANES, c)
    row = lax.broadcasted_iota(jnp.int32, a3.shape, 1)
    d = 1
    while d < SUBLANES:
        a_sh = jnp.where(row >= d, pltpu.roll(a3, d, axis=1), 1.0)
        b_sh = jnp.where(row >= d, pltpu.roll(b3, d, axis=1), 0.0)
        b3 = a3 * b_sh + b3
        a3 = a3 * a_sh
        d *= 2
    a_ref[...] = a3.reshape(ns, tt, c)
    b_ref[...] = b3.reshape(ns, tt, c)

    def slab(j, hp):
        s = pl.multiple_of(j * SUBLANES, SUBLANES)
        h8 = a_ref[:, pl.ds(s, SUBLANES), :] * hp + b_ref[:, pl.ds(s, SUBLANES), :]
        h_ref[:, pl.ds(s, SUBLANES), :] = h8
        return h8[:, SUBLANES - 1:SUBLANES, :]

    hp = lax.fori_loop(0, tt // SUBLANES, slab, hprev_ref[...])
    hprev_ref[...] = hp
    hlast_ref[...] = hp
    rec_ref[...] = h_ref[...] * _gelu_tanh(gate_ref[...])


def lru(xr3, gate3, hist, h0, cw, cb, wa_bd, ba, wx_bd, bx, lam, ns, tt):
    nseq, length, c = xr3.shape
    gw = wa_bd.shape[-1]
    hk = CONV_W - 1
    grid = (nseq // ns, length // tt)
    blk = pl.BlockSpec((ns, tt, c), lambda s, t: (s, t, 0))
    vec = _const_spec((1, c))
    return pl.pallas_call(
        functools.partial(_lru_kernel, ns=ns, tt=tt, c=c, gw=gw),
        out_shape=[jax.ShapeDtypeStruct((nseq, length, c), F32),
                   jax.ShapeDtypeStruct((nseq, hk, c), F32),
                   jax.ShapeDtypeStruct((nseq, 1, c), F32)],
        grid=grid,
        in_specs=[blk, blk,
                  pl.BlockSpec((ns, hk, c), lambda s, t: (s, 0, 0)),
                  pl.BlockSpec((ns, 1, c), lambda s, t: (s, 0, 0)),
                  _const_spec((CONV_W, c)), vec,
                  _const_spec(wa_bd.shape), vec, _const_spec(wx_bd.shape), vec, vec],
        out_specs=[blk,
                   pl.BlockSpec((ns, hk, c), lambda s, t: (s, 0, 0)),
                   pl.BlockSpec((ns, 1, c), lambda s, t: (s, 0, 0))],
        scratch_shapes=[pltpu.VMEM((ns, SUBLANES + tt, c), F32),
                        pltpu.VMEM((ns, tt, c), F32), pltpu.VMEM((ns, tt, c), F32), pltpu.VMEM((ns, tt, c), F32),
                        pltpu.VMEM((ns, 1, c), F32)],
        compiler_params=_cparams(("arbitrary", "arbitrary")), name="lru",
    )(xr3, gate3, hist, h0, cw, cb.reshape(1, c), wa_bd, ba.reshape(1, c), wx_bd, bx.reshape(1, c), lam.reshape(1, c))


def _swa_prompt_kernel(sink_ref, q_ref, kc_ref, kp_ref, vtc_ref, vtp_ref, o_ref):
    i = pl.program_id(0)
    w, hd = WINDOW, SWA_HEAD_DIM
    grp = SWA_HEADS // SWA_KV_HEADS
    nq = grp * w
    key = lax.broadcasted_iota(jnp.int32, (2 * w, nq), 0)
    qpos = lax.broadcasted_iota(jnp.int32, (2 * w, nq), 1) % w
    keep = ((key < w) & (key >= qpos) & (i > 0)) | ((key >= w) & ((key - w) <= qpos))
    colh = lax.broadcasted_iota(jnp.int32, (1, nq), 1) // w
    scale = 1.0 / math.sqrt(hd)
    for kh in range(SWA_KV_HEADS):
        ls = slice(kh * hd, (kh + 1) * hd)
        kk = jnp.concatenate([kp_ref[:, ls], kc_ref[:, ls]], axis=0).astype(BF16)
        q4 = jnp.concatenate([q_ref[:, (kh * grp + g) * hd:(kh * grp + g + 1) * hd] for g in range(grp)], axis=0)
        st = jnp.where(keep, _dot_nt(kk, (q4 * scale).astype(BF16)), NEG)
        sk = jnp.zeros((1, nq), F32)
        for g in range(grp):
            sk = jnp.where(colh == g, sink_ref[kh * grp + g], sk)
        m = jnp.maximum(jnp.max(st, axis=0, keepdims=True), sk)
        p = jnp.exp(st - m)
        den = jnp.sum(p, axis=0, keepdims=True) + jnp.exp(sk - m)
        vt = jnp.concatenate([vtp_ref[ls, :], vtc_ref[ls, :]], axis=1).astype(BF16)
        ot = _dot(vt, p.astype(BF16)) / den
        for g in range(grp):
            h = kh * grp + g
            o_ref[h * hd:(h + 1) * hd, :] = ot[:, g * w:(g + 1) * w]


def swa_prompt(q, k, vt, sink):
    length, qw = q.shape
    w = WINDOW
    kvw = SWA_KV_HEADS * SWA_HEAD_DIM
    cur = lambda i: (i, 0)
    prev = lambda i: (jnp.maximum(i - 1, 0), 0)
    cur_t = lambda i: (0, i)
    prev_t = lambda i: (0, jnp.maximum(i - 1, 0))
    return pl.pallas_call(
        _swa_prompt_kernel,
        out_shape=jax.ShapeDtypeStruct((qw, length), F32),
        grid=(length // w,),
        in_specs=[pl.BlockSpec(memory_space=pltpu.SMEM),
                  pl.BlockSpec((w, qw), cur),
                  pl.BlockSpec((w, kvw), cur), pl.BlockSpec((w, kvw), prev),
                  pl.BlockSpec((kvw, w), cur_t), pl.BlockSpec((kvw, w), prev_t)],
        out_specs=pl.BlockSpec((qw, w), cur_t),
        compiler_params=_cparams(("parallel",)), name="swa_prompt",
    )(sink, q, k, k, vt, vt)


def _swa_sample_kernel(sink_ref, q_ref, kn_ref, vn_ref, kb_ref, vb_ref, o_ref, ko_ref, vo_ref, kk_ref, vv_ref, *, bs, sl):
    w, hd = WINDOW, SWA_HEAD_DIM
    grp = SWA_HEADS // SWA_KV_HEADS
    nk = kk_ref.shape[0]
    scale = 1.0 / math.sqrt(hd)
    kk_ref[w + sl:nk, :] = jnp.zeros((nk - w - sl, kk_ref.shape[1]), F32)
    vv_ref[w + sl:nk, :] = jnp.zeros((nk - w - sl, vv_ref.shape[1]), F32)
    row = lax.broadcasted_iota(jnp.int32, (grp * sl, nk), 0)
    col = lax.broadcasted_iota(jnp.int32, (grp * sl, nk), 1)
    qi = row % sl
    mask = (col >= qi) & (col <= qi + w)
    rowc = lax.broadcasted_iota(jnp.int32, (grp * sl, 1), 0)

    def seq(b, carry):
        kb, kn = kb_ref[b], kn_ref[b]
        vb, vn = vb_ref[b], vn_ref[b]
        ko_ref[b, 0:w - sl, :] = kb_ref[b, sl:w, :]
        ko_ref[b, w - sl:w, :] = kn
        vo_ref[b, 0:w - sl, :] = vb_ref[b, sl:w, :]
        vo_ref[b, w - sl:w, :] = vn
        kk_ref[0:w, :] = kb
        kk_ref[w:w + sl, :] = kn
        vv_ref[0:w, :] = vb
        vv_ref[w:w + sl, :] = vn
        q = q_ref[b]
        for kh in range(SWA_KV_HEADS):
            ls = slice(kh * hd, (kh + 1) * hd)
            kkh = kk_ref[:, ls].astype(BF16)
            vvh = vv_ref[:, ls].astype(BF16)
            qs = jnp.concatenate([q[:, (kh * grp + g) * hd:(kh * grp + g + 1) * hd] for g in range(grp)], axis=0)
            s = jnp.where(mask, _dot_nt((qs * scale).astype(BF16), kkh), NEG)
            sk = jnp.zeros((grp * sl, 1), F32)
            for g in range(grp):
                sk = jnp.where(rowc // sl == g, sink_ref[kh * grp + g], sk)
            m = jnp.maximum(jnp.max(s, axis=1, keepdims=True), sk)
            p = jnp.exp(s - m)
            den = jnp.sum(p, axis=1, keepdims=True) + jnp.exp(sk - m)
            o = _dot(p.astype(BF16), vvh) / den
            for g in range(grp):
                h = kh * grp + g
                o_ref[b, :, h * hd:(h + 1) * hd] = o[g * sl:(g + 1) * sl, :]
        return carry

    lax.fori_loop(0, bs, seq, 0)


def swa_sample(q3, kn3, vn3, kbuf, vbuf, sink, bs=8):
    nseq, sl, qw = q3.shape
    w = WINDOW
    kvw = kn3.shape[-1]
    nk = 2 * w
    b3 = lambda shape: pl.BlockSpec(shape, lambda i: (i, 0, 0))
    return pl.pallas_call(
        functools.partial(_swa_sample_kernel, bs=bs, sl=sl),
        out_shape=[jax.ShapeDtypeStruct((nseq, sl, qw), F32),
                   jax.ShapeDtypeStruct((nseq, w, kvw), F32),
                   jax.ShapeDtypeStruct((nseq, w, kvw), F32)],
        grid=(nseq // bs,),
        in_specs=[pl.BlockSpec(memory_space=pltpu.SMEM),
                  b3((bs, sl, qw)), b3((bs, sl, kvw)), b3((bs, sl, kvw)), b3((bs, w, kvw)), b3((bs, w, kvw))],
        out_specs=[b3((bs, sl, qw)), b3((bs, w, kvw)), b3((bs, w, kvw))],
        scratch_shapes=[pltpu.VMEM((nk, kvw), F32), pltpu.VMEM((nk, kvw), F32)],
        compiler_params=_cparams(("arbitrary",)), name="swa_sample",
    )(sink, q3, kn3, vn3, kbuf, vbuf)


def _out_res_kernel(x_ref, a1_ref, w1_ref, a2_ref, w2_ref, o_ref, *, a2_t):
    a2 = a2_ref[...].astype(BF16)
    second = _dot_tn(a2, w2_ref[...]) if a2_t else _dot(a2, w2_ref[...])
    o_ref[...] = x_ref[...] + _dot(a1_ref[...].astype(BF16), w1_ref[...]) + second


def out_res(x2, a1, w1, a2, w2, a2_t=False, tm=512):
    m, d = x2.shape
    tm = min(tm, m)
    row = lambda width: pl.BlockSpec((tm, width), lambda i: (i, 0))
    a2_spec = pl.BlockSpec((a2.shape[0], tm), lambda i: (0, i)) if a2_t else row(a2.shape[1])
    return pl.pallas_call(
        functools.partial(_out_res_kernel, a2_t=a2_t),
        out_shape=jax.ShapeDtypeStruct((m, d), F32),
        grid=(m // tm,),
        in_specs=[row(d), row(a1.shape[1]), _const_spec(w1.shape), a2_spec, _const_spec(w2.shape)],
        out_specs=row(d),
        compiler_params=_cparams(("parallel",)), name="out_res",
    )(x2, a1, w1, a2, w2)


def _ffn_kernel(x_ref, g_ref, wg_ref, wu_ref, cw_ref, cb_ref, wd_ref, hist_ref, fg_ref,
                o_ref, hout_ref, ext_ref, act_ref, *, ns, tt, d, f, tf, final):
    t = pl.program_id(1)
    hk = FFN_CONV - 1

    @pl.when(t == 0)
    def _():
        ext_ref[:, SUBLANES - hk:SUBLANES, :] = hist_ref[...]

    x = x_ref[...].reshape(ns * tt, d)
    xn = _rms(x, g_ref[...]).astype(BF16)
    for c in range(f // tf):
        sl = slice(c * tf, (c + 1) * tf)
        g3 = _dot(xn, wg_ref[:, sl]).reshape(ns, tt, tf)
        ext_ref[:, SUBLANES:SUBLANES + tt, sl] = g3
        gc = cb_ref[:, sl] + g3 * cw_ref[hk:hk + 1, sl]
        for k in range(hk):
            gc = gc + ext_ref[:, SUBLANES - hk + k:SUBLANES - hk + k + tt, sl] * cw_ref[k:k + 1, sl]
        u = _dot(xn, wu_ref[:, sl])
        act_ref[:, sl] = (_silu(gc).reshape(ns * tt, tf) * u).astype(BF16)
    y = x + _dot(act_ref[...], wd_ref[...])
    if final:
        y = _rms(y, fg_ref[...])
    o_ref[...] = y.reshape(ns, tt, d)
    hout_ref[...] = ext_ref[:, SUBLANES + tt - hk:SUBLANES + tt, :]
    ext_ref[:, 0:SUBLANES, :] = ext_ref[:, tt:tt + SUBLANES, :]


def ffn(x3, g, wg, wu, cw, cb, wd, hist, fg, ns, tt, final, tf=256):
    nseq, length, d = x3.shape
    f = wg.shape[1]
    hk = FFN_CONV - 1
    blk = pl.BlockSpec((ns, tt, d), lambda s, t: (s, t, 0))
    hspec = pl.BlockSpec((ns, hk, f), lambda s, t: (s, 0, 0))
    return pl.pallas_call(
        functools.partial(_ffn_kernel, ns=ns, tt=tt, d=d, f=f, tf=tf, final=final),
        out_shape=[jax.ShapeDtypeStruct((nseq, length, d), F32), jax.ShapeDtypeStruct((nseq, hk, f), F32)],
        grid=(nseq // ns, length // tt),
        in_specs=[blk, _const_spec((1, d)), _const_spec(wg.shape), _const_spec(wu.shape),
                  _const_spec((FFN_CONV, f)), _const_spec((1, f)), _const_spec(wd.shape), hspec, _const_spec((1, d))],
        out_specs=[blk, hspec],
        scratch_shapes=[pltpu.VMEM((ns, SUBLANES + tt, f), F32), pltpu.VMEM((ns * tt, f), BF16)],
        compiler_params=_cparams(("arbitrary", "arbitrary")), name="ffn",
    )(x3, g.reshape(1, d), wg, wu, cw, cb.reshape(1, f), wd, hist, fg.reshape(1, d))


def _ssd_kernel(z_ref, xbc_ref, dt_ref, dtt_ref, hist_ref, s0_ref, cw_ref, cb_ref, dtb_ref, dtbt_ref,
                alog_ref, alogt_ref, dvec_ref, nw_ref,
                y_ref, conv_ref, sout_ref, ext_ref, st_ref, ybuf_ref, dtp_ref, dttp_ref, *, tin, tc, inner, nst):
    c = pl.program_id(1)
    carried = tin == tc
    hk = CONV_W - 1
    nh, hp = SSD_HEADS, SSD_HEAD_DIM
    gh = nh // SSD_GROUPS

    @pl.when(c == 0)
    def _():
        if tin < tc:
            ext_ref[...] = jnp.zeros(ext_ref.shape, F32)
            dtp_ref[...] = jnp.zeros(dtp_ref.shape, F32)
            dttp_ref[...] = jnp.zeros(dttp_ref.shape, F32)
        ext_ref[SUBLANES - hk:SUBLANES, :] = hist_ref[0]
        if carried:
            for h in range(nh):
                st_ref[:, h * hp:(h + 1) * hp] = s0_ref[0, h].T

    ext_ref[SUBLANES:SUBLANES + tin, :] = xbc_ref[0]
    xc = cb_ref[...]
    for k in range(CONV_W):
        xc = xc + ext_ref[SUBLANES - hk + k:SUBLANES - hk + k + tc, :] * cw_ref[k:k + 1, :]
    conv_ref[0] = ext_ref[SUBLANES + tin - hk:SUBLANES + tin, :]
    if tin == tc:
        ext_ref[0:SUBLANES, :] = ext_ref[tc:tc + SUBLANES, :]
    xa = _silu(xc)
    xs = xa[:, :inner]
    bm = [xa[:, inner + g * nst:inner + (g + 1) * nst].astype(BF16) for g in range(SSD_GROUPS)]
    cm = [xa[:, inner + (SSD_GROUPS + g) * nst:inner + (SSD_GROUPS + g + 1) * nst].astype(BF16) for g in range(SSD_GROUPS)]

    if tin < tc:
        dtp_ref[0:tin, :] = dt_ref[0]
        dttp_ref[:, 0:tin] = dtt_ref[0]
        dt_raw, dtt_raw = dtp_ref[...], dttp_ref[...]
        valid_r = lax.broadcasted_iota(jnp.int32, (tc, nh), 0) < tin
        valid_c = lax.broadcasted_iota(jnp.int32, (nh, tc), 1) < tin
        dt = jnp.where(valid_r, _softplus(dt_raw + dtb_ref[...]), 0.0)
        dtt = jnp.where(valid_c, _softplus(dtt_raw + dtbt_ref[...]), 0.0)
    else:
        dt = _softplus(dt_ref[0] + dtb_ref[...])
        dtt = _softplus(dtt_ref[0] + dtbt_ref[...])
    da = dt * (-jnp.exp(alog_ref[...]))
    dat = dtt * (-jnp.exp(alogt_ref[...]))

    li = lax.broadcasted_iota(jnp.int32, (tc, tc), 0)
    si = lax.broadcasted_iota(jnp.int32, (tc, tc), 1)
    causal = li >= si
    tri = jnp.where(causal, 1.0, 0.0).astype(BF16)
    trit = jnp.where(li <= si, 1.0, 0.0).astype(BF16)
    cs = sum(_dot(tri, p) for p in _split3(da))
    cst = sum(_dot(p, trit) for p in _split3(dat))
    cb_g =[_dot_nt(cm[g], bm[g]) for g in range(SSD_GROUPS)]

    gwid = inner // SSD_GROUPS
    if carried:
        erow = lax.broadcasted_iota(jnp.int32, (nh, inner), 0)
        ehead = lax.broadcasted_iota(jnp.int32, (nh, inner), 1) // hp
        expand = jnp.where(erow == ehead, 1.0, 0.0).astype(BF16)
        dt_e = sum(_dot(p, expand) for p in _split3(dt))
        cs_e = sum(_dot(p, expand) for p in _split3(cs))
        last_e = cs_e[tc - 1:tc, :]
        xd = xs * dt_e
        xdb = xd.astype(BF16)
        xdd = (xd * jnp.exp(last_e - cs_e)).astype(BF16)
        y_off = []
        for g in range(SSD_GROUPS):
            gs = slice(g * gwid, (g + 1) * gwid)
            st_g = st_ref[:, gs]
            y_off.append(_dot(cm[g], st_g.astype(BF16)))
            st_ref[:, gs] = jnp.exp(last_e[:, gs]) * st_g + _dot_tn(bm[g], xdd[:, gs])
        for h in range(nh):
            ls = slice(h * hp, (h + 1) * hp)
            lmat = jnp.exp(jnp.where(causal, cs[:, h:h + 1] - cst[h:h + 1, :], NEG))
            ybuf_ref[:, ls] = _dot((cb_g[h // gh] * lmat).astype(BF16), xdb[:, ls])
        y_all = ybuf_ref[...] + jnp.exp(cs_e) * jnp.concatenate(y_off, axis=1) + dvec_ref[...] * xs

        @pl.when(c == pl.num_programs(1) - 1)
        def _():
            for h in range(nh):
                sout_ref[0, h] = st_ref[:, h * hp:(h + 1) * hp].T
    else:
        for h in range(nh):
            g = h // gh
            ls = slice(h * hp, (h + 1) * hp)
            col = cs[:, h:h + 1]
            last = cs[tc - 1:tc, h:h + 1]
            lmat = jnp.exp(jnp.where(causal, col - cst[h:h + 1, :], NEG))
            xs_h = xs[:, ls]
            xd = xs_h * dt[:, h:h + 1]
            s_h = s0_ref[0, h]
            yh = _dot((cb_g[g] * lmat).astype(BF16), xd.astype(BF16))
            yh = yh + jnp.exp(col) * _dot_nt(cm[g], s_h.astype(BF16))
            sout_ref[0, h] = jnp.exp(last) * s_h + _dot_tn((xd * jnp.exp(last - col)).astype(BF16), bm[g])
            ybuf_ref[:, ls] = yh + dvec_ref[:, ls] * xs_h
        y_all = ybuf_ref[...]

    y = y_all[0:tin, :] * _silu(z_ref[0])
    for g in range(SSD_GROUPS):
        gs = slice(g * gwid, (g + 1) * gwid)
        y_ref[0, :, gs] = _rms(y[:, gs], nw_ref[:, gs])


def ssd(z3, xbc3, dt3, dtt3, hist, s0, cw, cb, dtb, alog, dvec, nw, tin):
    nseq, length, inner = z3.shape
    cd = xbc3.shape[-1]
    nh, hp, nst = s0.shape[1:]
    tc = SSD_CHUNK if tin == SSD_CHUNK else -(-tin // BF16_SUBLANES) * BF16_SUBLANES
    assert tin == SSD_CHUNK or tin == length
    hk = CONV_W - 1
    t3 = lambda width: pl.BlockSpec((1, tin, width), lambda s, c: (s, c, 0))
    per_seq = lambda shape: pl.BlockSpec((1,) + shape, lambda s, c: (s,) + (0,) * len(shape))
    return pl.pallas_call(
        functools.partial(_ssd_kernel, tin=tin, tc=tc, inner=inner, nst=nst),
        out_shape=[jax.ShapeDtypeStruct((nseq, length, inner), F32),
                   jax.ShapeDtypeStruct((nseq, hk, cd), F32),
                   jax.ShapeDtypeStruct(s0.shape, F32)],
        grid=(nseq, length // tin),
        in_specs=[t3(inner), t3(cd), t3(nh),
                  pl.BlockSpec((1, nh, tin), lambda s, c: (s, 0, c)),
                  per_seq((hk, cd)), per_seq((nh, hp, nst)),
                  _const_spec((CONV_W, cd)), _const_spec((1, cd)), _const_spec((1, nh)), _const_spec((nh, 1)),
                  _const_spec((1, nh)), _const_spec((nh, 1)), _const_spec((1, inner)), _const_spec((1, inner))],
        out_specs=[t3(inner), per_seq((hk, cd)), per_seq((nh, hp, nst))],
        scratch_shapes=[pltpu.VMEM((SUBLANES + tc, cd), F32), pltpu.VMEM((nst, inner), F32),
                        pltpu.VMEM((tc, inner), F32), pltpu.VMEM((tc, nh), F32), pltpu.VMEM((nh, tc), F32)],
        compiler_params=_cparams(("arbitrary", "arbitrary")), name="ssd",
    )(z3, xbc3, dt3, dtt3, hist, s0, cw, cb.reshape(1, cd), dtb.reshape(1, nh), dtb.reshape(nh, 1),
      alog.reshape(1, nh), alog.reshape(nh, 1), dvec.reshape(1, inner), nw.reshape(1, inner))


def _rope128(t, ctab, stab):
    half = QK_ROPE // 2
    lane = lax.broadcasted_iota(jnp.int32, t.shape, 1)
    swapped = jnp.where(lane < QK_NOPE + half, pltpu.roll(t, LANES - half, axis=1), pltpu.roll(t, half, axis=1))
    return t * ctab + swapped * stab


def _mla_prep_kernel(cq_ref, ckv_ref, kr_ref, ct_ref, st_ref, qg_ref, kg_ref, wuq_ref, wk_ref, wv_ref,
                     lat_ref, rope_ref, *outs, sample):
    scale = 1.0 / math.sqrt(QK_NOPE + QK_ROPE)
    ctab, stab = ct_ref[...], st_ref[...]
    qf = _dot(_rms(cq_ref[...], qg_ref[...]).astype(BF16), wuq_ref[...])
    ckv = _rms(ckv_ref[...], kg_ref[...])
    lat_ref[...] = ckv
    krr = _rope128(kr_ref[...], ctab, stab)
    rope_ref[...] = krr[:, QK_NOPE:QK_NOPE + QK_ROPE]
    if sample:
        qlat_ref, qr_ref = outs
        for h in range(MLA_HEADS):
            qh = _rope128(qf[:, h * LANES:(h + 1) * LANES], ctab, stab) * scale
            qlat_ref[h] = _dot(qh[:, :QK_NOPE].astype(BF16), wk_ref[h])
            qr_ref[h] = qh[:, QK_NOPE:QK_NOPE + QK_ROPE]
    else:
        qc_ref, kc_ref, vt_ref = outs
        ckvb = ckv.astype(BF16)
        vrows = vt_ref.shape[1]
        ones_row = jnp.where(lax.broadcasted_iota(jnp.int32, (vrows, ckvb.shape[0]), 0) == V_HEAD, 1.0, 0.0)
        for h in range(MLA_HEADS):
            qh = _rope128(qf[:, h * LANES:(h + 1) * LANES], ctab, stab) * (scale * LOG2E)
            qc_ref[h] = qh.astype(BF16)
            kc_ref[h] = (_dot(ckvb, wk_ref[h]) + krr).astype(BF16)
            vt_ref[h] = (_dot_nt(wv_ref[h], ckvb) + ones_row).astype(BF16)


def mla_prep(cq, ckv_raw, krp, ctab, stab, qg, kg, wuq_pad, wk, wv, sample, tm=512):
    m = cq.shape[0]
    tm = min(tm, m)
    nh = MLA_HEADS
    row = lambda width: pl.BlockSpec((tm, width), lambda i: (i, 0))
    hrow = lambda width: pl.BlockSpec((nh, tm, width), lambda i: (0, i, 0))
    lq, lkv = cq.shape[1], ckv_raw.shape[1]
    out_shape = [jax.ShapeDtypeStruct((m, lkv), F32), jax.ShapeDtypeStruct((m, QK_ROPE), F32)]
    out_specs = [row(lkv), row(QK_ROPE)]
    if sample:
        out_shape += [jax.ShapeDtypeStruct((nh, m, lkv), F32), jax.ShapeDtypeStruct((nh, m, QK_ROPE), F32)]
        out_specs += [hrow(lkv), hrow(QK_ROPE)]
    else:
        out_shape += [jax.ShapeDtypeStruct((nh, m, LANES), BF16)] * 2 + [jax.ShapeDtypeStruct((nh, VT_ROWS, m), BF16)]
        out_specs += [hrow(LANES)] * 2 + [pl.BlockSpec((nh, VT_ROWS, tm), lambda i: (0, 0, i))]
    return pl.pallas_call(
        functools.partial(_mla_prep_kernel, sample=sample),
        out_shape=out_shape, grid=(m // tm,),
        in_specs=[row(lq), row(lkv), row(LANES), row(LANES), row(LANES), _const_spec((1, lq)), _const_spec((1, lkv)),
                  _const_spec(wuq_pad.shape), _const_spec(wk.shape), _const_spec(wv.shape)],
        out_specs=out_specs,
        compiler_params=_cparams(("parallel",)), name="mla_prep",
    )(cq, ckv_raw, krp, ctab, stab, qg.reshape(1, lq), kg.reshape(1, lkv), wuq_pad, wk, wv)


def _mla_flash_kernel(qi_ref, kj_ref, q_ref, k_ref, vt_ref, o_ref, m_ref, acc_ref, *, t):
    n = pl.program_id(0)
    i, j = qi_ref[n], kj_ref[n]

    @pl.when(j == 0)
    def _():
        m_ref[...] = jnp.full(m_ref.shape, NEG, F32)
        acc_ref[...] = jnp.zeros(acc_ref.shape, F32)

    def tile(masked):
        if masked:
            keep = lax.broadcasted_iota(jnp.int32, (t, t), 0) <= lax.broadcasted_iota(jnp.int32, (t, t), 1)
        nxt = _dot_nt(k_ref[0], q_ref[0])
        pending = None
        for h in range(MLA_HEADS + 1):
            if h < MLA_HEADS:
                st = nxt
                if h + 1 < MLA_HEADS:
                    nxt = _dot_nt(k_ref[h + 1], q_ref[h + 1])
                if masked:
                    st = jnp.where(keep, st, NEG)
                m_prev = m_ref[h]
                m_new = jnp.maximum(m_prev, jnp.max(st, axis=0, keepdims=True))
                alpha = jnp.exp2(m_prev - m_new)
                p = jnp.exp2(st - m_new).astype(BF16)
                m_ref[h] = m_new
            if pending is not None:
                hp, alpha_p, p_p = pending
                acc_ref[hp] = alpha_p * acc_ref[hp] + _dot(vt_ref[hp], p_p)
            pending = (h, alpha, p) if h < MLA_HEADS else None

    @pl.when(j < i)
    def _():
        tile(False)

    @pl.when(j == i)
    def _():
        tile(True)
        for h in range(MLA_HEADS):
            a = acc_ref[h]
            o_ref[h * V_HEAD:(h + 1) * V_HEAD, :] = a[0:V_HEAD, :] / a[V_HEAD:V_HEAD + 1, :]


def mla_flash(qc, kc, vt, t=512):
    nh, length, _ = qc.shape
    t = min(t, length)
    nb = length // t
    qi = [i for i in range(nb) for _ in range(i + 1)]
    kj = [j for i in range(nb) for j in range(i + 1)]
    grid_spec = pltpu.PrefetchScalarGridSpec(
        num_scalar_prefetch=2, grid=(len(qi),),
        in_specs=[pl.BlockSpec((nh, t, LANES), lambda n, qi, kj: (0, qi[n], 0)),
                  pl.BlockSpec((nh, t, LANES), lambda n, qi, kj: (0, kj[n], 0)),
                  pl.BlockSpec((nh, VT_ROWS, t), lambda n, qi, kj: (0, 0, kj[n]))],
        out_specs=pl.BlockSpec((nh * V_HEAD, t), lambda n, qi, kj: (0, qi[n])),
        scratch_shapes=[pltpu.VMEM((nh, 1, t), F32), pltpu.VMEM((nh, VT_ROWS, t), F32)])
    return pl.pallas_call(
        functools.partial(_mla_flash_kernel, t=t),
        out_shape=jax.ShapeDtypeStruct((nh * V_HEAD, length), F32),
        grid_spec=grid_spec,
        compiler_params=_cparams(("arbitrary",)), name="mla_flash",
    )(jnp.asarray(qi, jnp.int32), jnp.asarray(kj, jnp.int32), qc, kc, vt)


def _mla_sample_kernel(pt_ref, qlat_ref, qr_ref, ckv_ref, kr_ref, wuv_ref, lat_hbm, rope_hbm,
                       o_ref, latbuf, ropebuf, sem, kbuf_ref, sbuf_ref, pbuf_ref, kl_ref, krl_ref,
                       *, cp, sl, npages, li):
    b, nb = pl.program_id(0), pl.num_programs(0)
    rows = MLA_HEADS * sl
    psz = latbuf.shape[2]
    pair = 2 * psz
    nch = npages // cp

    def chunk_copies(seq, ch, slot):
        out = []
        for c in range(cp):
            pid = pt_ref[seq, ch * cp + c]
            out.append(pltpu.make_async_copy(lat_hbm.at[li, pid], latbuf.at[slot, c], sem.at[0, slot]))
            out.append(pltpu.make_async_copy(rope_hbm.at[li, pid], ropebuf.at[slot, c], sem.at[1, slot]))
        return out

    def start(seq, ch, slot):
        for cpy in chunk_copies(seq, ch, slot):
            cpy.start()

    def wait(seq, ch, slot):
        for cpy in chunk_copies(seq, ch, slot):
            cpy.wait()

    nslots = latbuf.shape[0]
    ahead = nslots - 1

    @pl.when(b == 0)
    def _():
        for ch in range(ahead):
            start(0, ch, ch % nslots)

    q = qlat_ref[...].reshape(rows, qlat_ref.shape[-1]).astype(BF16)
    qr = qr_ref[...].reshape(rows, qr_ref.shape[-1]).astype(BF16)

    for ch in range(nch):
        slot = ch % nslots
        wait(b, ch, slot)
        nxt = ch + ahead
        if nxt < nch:
            start(b, nxt, nxt % nslots)
        else:
            @pl.when(b + 1 < nb)
            def _():
                start(b + 1, nxt - nch, nxt % nslots)
        for c2 in range(cp // 2):
            base = (ch * cp + 2 * c2) * psz
            kbuf_ref[base:base + psz, :] = latbuf[slot, 2 * c2].astype(BF16)
            kbuf_ref[base + psz:base + pair, :] = latbuf[slot, 2 * c2 + 1].astype(BF16)
            rt = jnp.concatenate([ropebuf[slot, 2 * c2], ropebuf[slot, 2 * c2 + 1]], axis=1).astype(BF16)
            sbuf_ref[:, base:base + pair] = _dot_nt(q, kbuf_ref[base:base + pair, :]) + _dot(qr, rt)

    kl_ref[...] = jnp.zeros(kl_ref.shape, F32)
    krl_ref[...] = jnp.zeros(krl_ref.shape, F32)
    kl_ref[0:sl, :] = ckv_ref[0]
    krl_ref[0:sl, :] = kr_ref[0]
    kl = kl_ref[...].astype(BF16)
    kbuf_ref[npages * psz:npages * psz + pair, :] = kl
    s = _dot_nt(q, kl) + _dot_nt(qr, krl_ref[...].astype(BF16))
    row = lax.broadcasted_iota(jnp.int32, s.shape, 0)
    col = lax.broadcasted_iota(jnp.int32, s.shape, 1)
    sbuf_ref[:, npages * psz:npages * psz + pair] = jnp.where((col < sl) & (col <= row % sl), s, NEG)

    sall = sbuf_ref[...]
    m = jnp.max(sall, axis=1, keepdims=True)
    p = jnp.exp(sall - m)
    lsum = jnp.sum(p, axis=1, keepdims=True)
    pbuf_ref[...] = p.astype(BF16)
    acc = _dot(pbuf_ref[...], kbuf_ref[...])
    accn = (acc / lsum).astype(BF16)
    for h in range(MLA_HEADS):
        o_ref[0, :, h * V_HEAD:(h + 1) * V_HEAD] = _dot(accn[h * sl:(h + 1) * sl, :], wuv_ref[h])


def mla_sample(qlat, qr, ckv3, kr3, wuv, cache_lat, cache_rope_t, page_table, li, cp=8):
    nh, m, lkv = qlat.shape
    nseq, sl, rd = kr3.shape
    npages = page_table.shape[1]
    nslots = MLA_PAGE_SLOTS
    cp = min(cp, npages // nslots)
    assert npages % (nslots * cp) == 0 and cp % 2 == 0
    psz = cache_lat.shape[2]
    nkeys = npages * psz + 2 * psz
    rows = nh * sl
    in_specs = [pl.BlockSpec((nh, sl, lkv), lambda b, pt: (0, b, 0)),
                pl.BlockSpec((nh, sl, rd), lambda b, pt: (0, b, 0)),
                pl.BlockSpec((1, sl, lkv), lambda b, pt: (b, 0, 0)),
                pl.BlockSpec((1, sl, rd), lambda b, pt: (b, 0, 0)),
                pl.BlockSpec(wuv.shape, lambda b, pt: (0, 0, 0)),
                pl.BlockSpec(memory_space=pl.ANY), pl.BlockSpec(memory_space=pl.ANY)]
    grid_spec = pltpu.PrefetchScalarGridSpec(
        num_scalar_prefetch=1, grid=(nseq,), in_specs=in_specs,
        out_specs=pl.BlockSpec((1, sl, nh * V_HEAD), lambda b, pt: (b, 0, 0)),
        scratch_shapes=[pltpu.VMEM((nslots, cp, psz, lkv), F32), pltpu.VMEM((nslots, cp, rd, psz), F32),
                        pltpu.SemaphoreType.DMA((2, nslots)),
                        pltpu.VMEM((nkeys, lkv), BF16), pltpu.VMEM((rows, nkeys), F32), pltpu.VMEM((rows, nkeys), BF16),
                        pltpu.VMEM((2 * psz, lkv), F32), pltpu.VMEM((2 * psz, rd), F32)])
    return pl.pallas_call(
        functools.partial(_mla_sample_kernel, cp=cp, sl=sl, npages=npages, li=li),
        out_shape=jax.ShapeDtypeStruct((nseq, sl, nh * V_HEAD), F32),
        grid_spec=grid_spec,
        compiler_params=_cparams(("arbitrary",)), name="mla_sample",
    )(page_table, qlat, qr, ckv3, kr3, wuv, cache_lat, cache_rope_t)


def _block_diag(w, per):
    nh, b, _ = w.shape
    w4 = w.reshape(nh // per, per, b, b)
    eye = jnp.eye(per, dtype=w.dtype)
    return jnp.einsum("gaij,ab->gaibj", w4, eye).reshape(nh // per, per * b, per * b)


def _rope_tables(pos):
    half = QK_ROPE // 2
    inv = jnp.exp(-(math.log(ROPE_THETA) / half) * jnp.arange(half, dtype=F32))
    ang = pos.astype(F32)[:, None] * inv[None, :]
    cos, sin = jnp.cos(ang), jnp.sin(ang)
    n = pos.shape[0]
    pad = LANES - QK_NOPE - QK_ROPE
    ctab = jnp.concatenate([jnp.ones((n, QK_NOPE), F32), cos, cos, jnp.ones((n, pad), F32)], axis=1)
    stab = jnp.concatenate([jnp.zeros((n, QK_NOPE), F32), -sin, sin, jnp.zeros((n, pad), F32)], axis=1)
    return ctab, stab


def _prep_weights(P):
    w = {}
    lw = P["lru_conv_w"].shape[-1]
    qw = SWA_HEADS * SWA_HEAD_DIM
    kvw = SWA_KV_HEADS * SWA_HEAD_DIM
    ew = P["even_w_in"][0].astype(BF16)
    b = [0, lw, 2 * lw, 2 * lw + qw, 2 * lw + qw + kvw, 2 * lw + qw + 2 * kvw]
    w["even_in"] = [ew[:, b[i]:b[i + 1]] for i in range(5)]
    per = 256 // (lw // LRU_HEADS)
    w["wa_bd"] = _block_diag(P["lru_wa"][0], per).astype(BF16)
    w["wx_bd"] = _block_diag(P["lru_wx"][0], per).astype(BF16)
    eo = P["even_w_out"][0].astype(BF16)
    w["even_out"] = (eo[:lw], eo[lw:])

    inner = P["ssd_norm"].shape[-1]
    cd = P["ssd_conv_w"].shape[-1]
    lq = P["mla_q_norm"].shape[-1]
    lkv = P["mla_kv_norm"].shape[-1]
    ow = P["odd_w_in"][0]
    b = [0, inner, inner + cd, inner + cd + SSD_HEADS, inner + cd + SSD_HEADS + lq,
         inner + cd + SSD_HEADS + lq + lkv, inner + cd + SSD_HEADS + lq + lkv + QK_ROPE]
    parts = [ow[:, b[i]:b[i + 1]] for i in range(6)]
    pad = LANES - QK_NOPE - QK_ROPE
    kr_pad = jnp.pad(parts[5], ((0, 0), (QK_NOPE, pad)))
    w["odd_in"] = [parts[0].astype(BF16), parts[1].astype(BF16), parts[2].astype(BF16), parts[2].T.astype(BF16),
                   parts[3].astype(BF16), parts[4].astype(BF16), kr_pad.astype(BF16)]
    uq = P["mla_w_uq"][0].reshape(lq, MLA_HEADS, QK_NOPE + QK_ROPE)
    w["wuq_pad"] = jnp.pad(uq, ((0, 0), (0, 0), (0, pad))).reshape(lq, MLA_HEADS * LANES).astype(BF16)
    uk = P["mla_w_uk"][0]
    uv = P["mla_w_uv"][0]
    w["wuk_pad"] = jnp.pad(jnp.transpose(uk, (1, 0, 2)), ((0, 0), (0, 0), (0, LANES - QK_NOPE))).astype(BF16)
    w["wuk_t"] = jnp.transpose(uk, (1, 2, 0)).astype(BF16)
    w["wuv_t_pad"] = jnp.pad(jnp.transpose(uv, (1, 2, 0)), ((0, 0), (0, VT_ROWS - V_HEAD), (0, 0))).astype(BF16)
    w["wuv"] = jnp.transpose(uv, (1, 0, 2)).astype(BF16)
    oo = P["odd_w_out"][0].astype(BF16)
    w["odd_out"] = (oo[:inner], oo[inner:])
    w["dvec"] = jnp.repeat(P["ssd_d"][0], SSD_HEAD_DIM)
    w["ffn"] = [(P["ffn_w_gate"][l].astype(BF16), P["ffn_w_up"][l].astype(BF16), P["ffn_w_down"][l].astype(BF16))
                for l in range(P["ffn_w_gate"].shape[0])]
    return w


def _trunk(x3, pos, P, W, st, sample):
    nseq, length, d = x3.shape
    m = nseq * length
    lw = P["lru_conv_w"].shape[-1]
    kvw = SWA_KV_HEADS * SWA_HEAD_DIM
    if sample:
        seq_tile, time_tile = min(nseq, 64), length
    else:
        seq_tile, time_tile = 1, min(length, 512)
    out = {}

    if sample:
        xr, gate, q, k, v = norm_proj(x3.reshape(m, d), P["mix_norm"][0], W["even_in"], [False] * 5)
    else:
        xr, gate, q, k, v, vt = norm_proj(x3.reshape(m, d), P["mix_norm"][0], W["even_in"] + [W["even_in"][4].T],
                                          [False] * 5 + [True])
    lru_ns, lru_tt = (min(nseq, 32), length) if sample else (1, min(length, 256))
    rec, out["lru_conv"], h_last = lru(
        xr.reshape(nseq, length, lw), gate.reshape(nseq, length, lw), st["lru_conv"], st["lru_h"].reshape(nseq, 1, lw),
        P["lru_conv_w"][0], P["lru_conv_b"][0], W["wa_bd"], P["lru_ba"][0], W["wx_bd"], P["lru_bx"][0],
        P["lru_lambda"][0], lru_ns, lru_tt)
    out["lru_h"] = h_last.reshape(nseq, lw)
    if sample:
        att, sk, sv = swa_sample(q.reshape(nseq, length, -1), k.reshape(nseq, length, kvw), v.reshape(nseq, length, kvw),
                                 st["swa_k"].reshape(nseq, WINDOW, kvw), st["swa_v"].reshape(nseq, WINDOW, kvw),
                                 P["swa_sink"][0], bs=min(nseq, 8))
        att = att.reshape(m, -1)
        out["swa_k"] = sk.reshape(nseq, WINDOW, SWA_KV_HEADS, SWA_HEAD_DIM)
        out["swa_v"] = sv.reshape(nseq, WINDOW, SWA_KV_HEADS, SWA_HEAD_DIM)
    else:
        att = swa_prompt(q, k, vt, P["swa_sink"][0])
        out["swa_k"] = k[-WINDOW:].reshape(1, WINDOW, SWA_KV_HEADS, SWA_HEAD_DIM)
        out["swa_v"] = v[-WINDOW:].reshape(1, WINDOW, SWA_KV_HEADS, SWA_HEAD_DIM)
    x2 = out_res(x3.reshape(m, d), rec.reshape(m, lw), W["even_out"][0], att, W["even_out"][1], a2_t=not sample)
    wg, wu, wd = W["ffn"][0]
    x3, fc0 = ffn(x2.reshape(nseq, length, d), P["ffn_norm"][0], wg, wu, P["ffn_conv_w"][0], P["ffn_conv_b"][0], wd,
                  st["ffn_conv"][0], P["final_norm"], seq_tile, time_tile, final=False)

    z, xbc, dt, dtt, cq, ckv_raw, krp = norm_proj(x3.reshape(m, d), P["mix_norm"][1], W["odd_in"],
                                                   [False, False, False, True, False, False, False])
    inner = z.shape[-1]
    cd = xbc.shape[-1]
    dtt3 = jnp.transpose(dtt.reshape(SSD_HEADS, nseq, length), (1, 0, 2))
    y, out["ssd_conv"], out["ssd"] = ssd(
        z.reshape(nseq, length, inner), xbc.reshape(nseq, length, cd), dt.reshape(nseq, length, SSD_HEADS), dtt3,
        st["ssd_conv"], st["ssd"], P["ssd_conv_w"][0], P["ssd_conv_b"][0], P["ssd_dt_bias"][0], P["ssd_a_log"][0],
        W["dvec"], P["ssd_norm"][0], tin=min(length, SSD_CHUNK))
    ctab, stab = _rope_tables(pos)
    if sample:
        lat, rp, qlat, qr = mla_prep(cq, ckv_raw, krp, ctab, stab, P["mla_q_norm"][0], P["mla_kv_norm"][0],
                                     W["wuq_pad"], W["wuk_t"], W["wuv"], sample=True)
        att = mla_sample(qlat, qr, lat.reshape(nseq, length, -1), rp.reshape(nseq, length, -1), W["wuv"],
                         st["mla_latent"], jnp.swapaxes(st["mla_rope"], 2, 3), st["page_table"], 0).reshape(m, -1)
    else:
        lat, rp, qc, kc, vt = mla_prep(cq, ckv_raw, krp, ctab, stab, P["mla_q_norm"][0], P["mla_kv_norm"][0],
                                       W["wuq_pad"], W["wuk_pad"], W["wuv_t_pad"], sample=False)
        att = mla_flash(qc, kc, vt)
    out["mla_latent"] = lat.reshape(nseq, length, -1)
    out["mla_rope"] = rp.reshape(nseq, length, -1)
    x2 = out_res(x3.reshape(m, d), y.reshape(m, inner), W["odd_out"][0], att, W["odd_out"][1], a2_t=not sample)
    wg, wu, wd = W["ffn"][1]
    y3, fc1 = ffn(x2.reshape(nseq, length, d), P["ffn_norm"][1], wg, wu, P["ffn_conv_w"][1], P["ffn_conv_b"][1], wd,
                  st["ffn_conv"][1], P["final_norm"], seq_tile, time_tile, final=True)
    out["ffn_conv"] = jnp.stack([fc0, fc1])
    return y3, out


def kernel(x_prompt, x_sample, state_lru_conv, state_lru_h, cache_swa_k, cache_swa_v, state_ssd_conv, state_ssd,
           cache_mla_latent, cache_mla_rope, page_table, state_ffn_conv, mix_norm, ffn_norm, final_norm, even_w_in,
           lru_conv_w, lru_conv_b, lru_wa, lru_ba, lru_wx, lru_bx, lru_lambda, swa_sink, even_w_out, odd_w_in,
           ssd_conv_w, ssd_conv_b, ssd_dt_bias, ssd_a_log, ssd_d, ssd_norm, mla_q_norm, mla_w_uq, mla_kv_norm,
           mla_w_uk, mla_w_uv, odd_w_out, ffn_w_gate, ffn_w_up, ffn_conv_w, ffn_conv_b, ffn_w_down):
    P = dict(mix_norm=mix_norm, ffn_norm=ffn_norm, final_norm=final_norm, even_w_in=even_w_in,
             lru_conv_w=lru_conv_w, lru_conv_b=lru_conv_b, lru_wa=lru_wa, lru_ba=lru_ba, lru_wx=lru_wx,
             lru_bx=lru_bx, lru_lambda=lru_lambda, swa_sink=swa_sink, even_w_out=even_w_out,
             odd_w_in=odd_w_in, ssd_conv_w=ssd_conv_w, ssd_conv_b=ssd_conv_b, ssd_dt_bias=ssd_dt_bias,
             ssd_a_log=ssd_a_log, ssd_d=ssd_d, ssd_norm=ssd_norm, mla_q_norm=mla_q_norm,
             mla_w_uq=mla_w_uq, mla_kv_norm=mla_kv_norm, mla_w_uk=mla_w_uk, mla_w_uv=mla_w_uv,
             odd_w_out=odd_w_out, ffn_w_gate=ffn_w_gate, ffn_w_up=ffn_w_up, ffn_conv_w=ffn_conv_w,
             ffn_conv_b=ffn_conv_b, ffn_w_down=ffn_w_down)
    W = _prep_weights(P)
    bp, lp, d = x_prompt.shape
    bs, ls, _ = x_sample.shape
    depth = ffn_w_gate.shape[0]
    lw = lru_conv_w.shape[-1]
    cd = ssd_conv_w.shape[-1]
    f = ffn_w_gate.shape[-1]
    kvw = SWA_KV_HEADS * SWA_HEAD_DIM
    past_len = page_table.shape[1] * PAGE_SIZE

    st_p = dict(lru_conv=jnp.zeros((bp, CONV_W - 1, lw), F32), lru_h=jnp.zeros((bp, lw), F32),
                ssd_conv=jnp.zeros((bp, CONV_W - 1, cd), F32),
                ssd=jnp.zeros((bp, SSD_HEADS, SSD_HEAD_DIM, SSD_STATE), F32),
                ffn_conv=jnp.zeros((depth, bp, FFN_CONV - 1, f), F32))
    st_s = dict(lru_conv=state_lru_conv[0], lru_h=state_lru_h[0], swa_k=cache_swa_k[0], swa_v=cache_swa_v[0],
                ssd_conv=state_ssd_conv[0], ssd=state_ssd[0], mla_latent=cache_mla_latent, mla_rope=cache_mla_rope,
                page_table=page_table, ffn_conv=state_ffn_conv)
    pos_p = jnp.tile(jnp.arange(lp), bp)
    pos_s = jnp.tile(past_len + jnp.arange(ls), bs)
    y_p, sp = _trunk(x_prompt, pos_p, P, W, st_p, False)
    y_s, ss = _trunk(x_sample, pos_s, P, W, st_s, True)
    e = lambda a: a[None]
    return (y_p, y_s,
            e(sp["lru_conv"]), e(ss["lru_conv"]), e(sp["lru_h"]), e(ss["lru_h"]),
            e(sp["swa_k"]), e(ss["swa_k"]), e(sp["swa_v"]), e(ss["swa_v"]),
            e(sp["ssd_conv"]), e(ss["ssd_conv"]), e(sp["ssd"]), e(ss["ssd"]),
            e(sp["mla_latent"]), e(ss["mla_latent"]), e(sp["mla_rope"]), e(ss["mla_rope"]),
            sp["ffn_conv"], ss["ffn_conv"])
```

```python
import functools
import math

import jax
import jax.numpy as jnp
from jax import lax
from jax.experimental import pallas as pl
from jax.experimental.pallas import tpu as pltpu

F32 = jnp.float32
BF16 = jnp.bfloat16

EPS = 1e-6
LRU_C = 8.0
LRU_HEADS = 16
CONV_W = 4
SWA_HEADS = 8
SWA_KV_HEADS = 2
SWA_HEAD_DIM = 64
WINDOW = 128
SSD_HEADS = 16
SSD_HEAD_DIM = 64
SSD_GROUPS = 2
SSD_STATE = 128
SSD_CHUNK = 128
MLA_HEADS = 8
QK_NOPE = 64
QK_ROPE = 32
V_HEAD = 64
ROPE_THETA = 10000.0
PAGE_SIZE = 128
FFN_CONV = 3
NEG = -1e30
LOG2E = math.log2(math.e)
VT_ROWS = 80
MLA_PAGE_SLOTS = 4

VMEM_LIMIT = 56 * 1024 * 1024
SUBLANES = 8
BF16_SUBLANES = 16
LANES = 128


def _cparams(sem):
    return pltpu.CompilerParams(dimension_semantics=sem, vmem_limit_bytes=VMEM_LIMIT)


def _dot(a, b):
    return jnp.dot(a, b, preferred_element_type=F32)


def _dot_nt(a, b):
    return lax.dot_general(a, b, (((1,), (1,)), ((), ())), preferred_element_type=F32)


def _dot_tn(a, b):
    return lax.dot_general(a, b, (((0,), (0,)), ((), ())), preferred_element_type=F32)


def _sigmoid(x):
    return 1.0 / (1.0 + jnp.exp(-x))


def _silu(x):
    return x * _sigmoid(x)


def _softplus(x):
    return jnp.maximum(x, 0.0) + jnp.log1p(jnp.exp(-jnp.abs(x)))


def _gelu_tanh(x):
    return 0.5 * x * (1.0 + jnp.tanh(math.sqrt(2.0 / math.pi) * (x + 0.044715 * (x * x * x))))


def _rms(x, g):
    return x * lax.rsqrt(jnp.mean(x * x, axis=-1, keepdims=True) + EPS) * g


def _split3(x):
    x1 = x.astype(BF16)
    r1 = x - x1.astype(F32)
    x2 = r1.astype(BF16)
    x3 = (r1 - x2.astype(F32)).astype(BF16)
    return x1, x2, x3


def _const_spec(shape):
    nd = len(shape)
    return pl.BlockSpec(shape, lambda *_: (0,) * nd, pipeline_mode=pl.Buffered(1))


def _norm_proj_kernel(x_ref, g_ref, *refs, nts):
    n = len(nts)
    w_refs, o_refs = refs[:n], refs[n:]
    xn = _rms(x_ref[...], g_ref[...]).astype(BF16)
    for w_ref, o_ref, nt in zip(w_refs, o_refs, nts):
        if nt:
            o_ref[...] = _dot_nt(w_ref[...], xn)
        else:
            o_ref[...] = _dot(xn, w_ref[...])


def norm_proj(x2, g, ws, nts, tm=512):
    m, k = x2.shape
    tm = min(tm, m)
    in_specs = [pl.BlockSpec((tm, k), lambda i: (i, 0)), _const_spec((1, k))]
    out_shape, out_specs = [], []
    for w, nt in zip(ws, nts):
        in_specs.append(_const_spec(w.shape))
        if nt:
            out_shape.append(jax.ShapeDtypeStruct((w.shape[0], m), F32))
            out_specs.append(pl.BlockSpec((w.shape[0], tm), lambda i: (0, i)))
        else:
            out_shape.append(jax.ShapeDtypeStruct((m, w.shape[1]), F32))
            out_specs.append(pl.BlockSpec((tm, w.shape[1]), lambda i: (i, 0)))
    return pl.pallas_call(
        functools.partial(_norm_proj_kernel, nts=tuple(nts)),
        out_shape=out_shape, grid=(m // tm,), in_specs=in_specs, out_specs=out_specs,
        compiler_params=_cparams(("parallel",)), name="norm_proj",
    )(x2, g.reshape(1, k), *ws)


def _lru_kernel(xr_ref, gate_ref, hist_ref, h0_ref, cw_ref, cb_ref, wa_ref, ba_ref, wx_ref, bx_ref, lam_ref,
                rec_ref, conv_ref, hlast_ref, ext_ref, a_ref, b_ref, h_ref, hprev_ref, *, ns, tt, c, gw):
    t = pl.program_id(1)
    hk = CONV_W - 1

    @pl.when(t == 0)
    def _():
        ext_ref[:, SUBLANES - hk:SUBLANES, :] = hist_ref[...]
        hprev_ref[...] = h0_ref[...]

    ext_ref[:, SUBLANES:SUBLANES + tt, :] = xr_ref[...]
    xc = cb_ref[...]
    for k in range(CONV_W):
        xc = xc + ext_ref[:, SUBLANES - hk + k:SUBLANES - hk + k + tt, :] * cw_ref[k:k + 1, :]
    conv_ref[...] = ext_ref[:, SUBLANES + tt - hk:SUBLANES + tt, :]
    ext_ref[:, 0:SUBLANES, :] = ext_ref[:, tt:tt + SUBLANES, :]

    x2 = xc.reshape(ns * tt, c)
    ra, rx = [], []
    for j in range(c // gw):
        xg = x2[:, j * gw:(j + 1) * gw].astype(BF16)
        ra.append(_dot(xg, wa_ref[j]))
        rx.append(_dot(xg, wx_ref[j]))
    r = _sigmoid(jnp.concatenate(ra, axis=1) + ba_ref[...])
    ig = _sigmoid(jnp.concatenate(rx, axis=1) + bx_ref[...])
    log_a = (-LRU_C) * r * _softplus(-lam_ref[...])
    a = jnp.exp(log_a)
    b = jnp.sqrt(-jnp.tanh(log_a) * (a * a + 1.0)) * (ig * x2)

    a3 = a.reshape(ns * tt // SUBLANES, SUBLANES, c)
    b3 = b.reshape(ns * tt // SUBLANES, SUBLANES, c)
    row = lax.broadcasted_iota(jnp.int32, a3.shape, 1)
    d = 1
    while d < SUBLANES:
        a_sh = jnp.where(row >= d, pltpu.roll(a3, d, axis=1), 1.0)
        b_sh = jnp.where(row >= d, pltpu.roll(b3, d, axis=1), 0.0)
        b3 = a3 * b_sh + b3
        a3 = a3 * a_sh
        d *= 2
    a_ref[...] = a3.reshape(ns, tt, c)
    b_ref[...] = b3.reshape(ns, tt, c)

    def slab(j, hp):
        s = pl.multiple_of(j * SUBLANES, SUBLANES)
        h8 = a_ref[:, pl.ds(s, SUBLANES), :] * hp + b_ref[:, pl.ds(s, SUBLANES), :]
        h_ref[:, pl.ds(s, SUBLANES), :] = h8
        return h8[:, SUBLANES - 1:SUBLANES, :]

    hp = lax.fori_loop(0, tt // SUBLANES, slab, hprev_ref[...])
    hprev_ref[...] = hp
    hlast_ref[...] = hp
    rec_ref[...] = h_ref[...] * _gelu_tanh(gate_ref[...])


def lru(xr3, gate3, hist, h0, cw, cb, wa_bd, ba, wx_bd, bx, lam, ns, tt):
    nseq, length, c = xr3.shape
    gw = wa_bd.shape[-1]
    hk = CONV_W - 1
    grid = (nseq // ns, length // tt)
    blk = pl.BlockSpec((ns, tt, c), lambda s, t: (s, t, 0))
    vec = _const_spec((1, c))
    return pl.pallas_call(
        functools.partial(_lru_kernel, ns=ns, tt=tt, c=c, gw=gw),
        out_shape=[jax.ShapeDtypeStruct((nseq, length, c), F32),
                   jax.ShapeDtypeStruct((nseq, hk, c), F32),
                   jax.ShapeDtypeStruct((nseq, 1, c), F32)],
        grid=grid,
        in_specs=[blk, blk,
                  pl.BlockSpec((ns, hk, c), lambda s, t: (s, 0, 0)),
                  pl.BlockSpec((ns, 1, c), lambda s, t: (s, 0, 0)),
                  _const_spec((CONV_W, c)), vec,
                  _const_spec(wa_bd.shape), vec, _const_spec(wx_bd.shape), vec, vec],
        out_specs=[blk,
                   pl.BlockSpec((ns, hk, c), lambda s, t: (s, 0, 0)),
                   pl.BlockSpec((ns, 1, c), lambda s, t: (s, 0, 0))],
        scratch_shapes=[pltpu.VMEM((ns, SUBLANES + tt, c), F32),
                        pltpu.VMEM((ns, tt, c), F32), pltpu.VMEM((ns, tt, c), F32), pltpu.VMEM((ns, tt, c), F32),
                        pltpu.VMEM((ns, 1, c), F32)],
        compiler_params=_cparams(("arbitrary", "arbitrary")), name="lru",
    )(xr3, gate3, hist, h0, cw, cb.reshape(1, c), wa_bd, ba.reshape(1, c), wx_bd, bx.reshape(1, c), lam.reshape(1, c))


def _swa_prompt_kernel(sink_ref, q_ref, kc_ref, kp_ref, vtc_ref, vtp_ref, o_ref):
    i = pl.program_id(0)
    w, hd = WINDOW, SWA_HEAD_DIM
    grp = SWA_HEADS // SWA_KV_HEADS
    nq = grp * w
    key = lax.broadcasted_iota(jnp.int32, (2 * w, nq), 0)
    qpos = lax.broadcasted_iota(jnp.int32, (2 * w, nq), 1) % w
    keep = ((key < w) & (key >= qpos) & (i > 0)) | ((key >= w) & ((key - w) <= qpos))
    colh = lax.broadcasted_iota(jnp.int32, (1, nq), 1) // w
    scale = 1.0 / math.sqrt(hd)
    for kh in range(SWA_KV_HEADS):
        ls = slice(kh * hd, (kh + 1) * hd)
        kk = jnp.concatenate([kp_ref[:, ls], kc_ref[:, ls]], axis=0).astype(BF16)
        q4 = jnp.concatenate([q_ref[:, (kh * grp + g) * hd:(kh * grp + g + 1) * hd] for g in range(grp)], axis=0)
        st = jnp.where(keep, _dot_nt(kk, (q4 * scale).astype(BF16)), NEG)
        sk = jnp.zeros((1, nq), F32)
        for g in range(grp):
            sk = jnp.where(colh == g, sink_ref[kh * grp + g], sk)
        m = jnp.maximum(jnp.max(st, axis=0, keepdims=True), sk)
        p = jnp.exp(st - m)
        den = jnp.sum(p, axis=0, keepdims=True) + jnp.exp(sk - m)
        vt = jnp.concatenate([vtp_ref[ls, :], vtc_ref[ls, :]], axis=1).astype(BF16)
        ot = _dot(vt, p.astype(BF16)) / den
        for g in range(grp):
            h = kh * grp + g
            o_ref[h * hd:(h + 1) * hd, :] = ot[:, g * w:(g + 1) * w]


def swa_prompt(q, k, vt, sink):
    length, qw = q.shape
    w = WINDOW
    kvw = SWA_KV_HEADS * SWA_HEAD_DIM
    cur = lambda i: (i, 0)
    prev = lambda i: (jnp.maximum(i - 1, 0), 0)
    cur_t = lambda i: (0, i)
    prev_t = lambda i: (0, jnp.maximum(i - 1, 0))
    return pl.pallas_call(
        _swa_prompt_kernel,
        out_shape=jax.ShapeDtypeStruct((qw, length), F32),
        grid=(length // w,),
        in_specs=[pl.BlockSpec(memory_space=pltpu.SMEM),
                  pl.BlockSpec((w, qw), cur),
                  pl.BlockSpec((w, kvw), cur), pl.BlockSpec((w, kvw), prev),
                  pl.BlockSpec((kvw, w), cur_t), pl.BlockSpec((kvw, w), prev_t)],
        out_specs=pl.BlockSpec((qw, w), cur_t),
        compiler_params=_cparams(("parallel",)), name="swa_prompt",
    )(sink, q, k, k, vt, vt)


def _swa_sample_kernel(sink_ref, q_ref, kn_ref, vn_ref, kb_ref, vb_ref, o_ref, ko_ref, vo_ref, kk_ref, vv_ref, *, bs, sl):
    w, hd = WINDOW, SWA_HEAD_DIM
    grp = SWA_HEADS // SWA_KV_HEADS
    nk = kk_ref.shape[0]
    scale = 1.0 / math.sqrt(hd)
    kk_ref[w + sl:nk, :] = jnp.zeros((nk - w - sl, kk_ref.shape[1]), F32)
    vv_ref[w + sl:nk, :] = jnp.zeros((nk - w - sl, vv_ref.shape[1]), F32)
    row = lax.broadcasted_iota(jnp.int32, (grp * sl, nk), 0)
    col = lax.broadcasted_iota(jnp.int32, (grp * sl, nk), 1)
    qi = row % sl
    mask = (col >= qi) & (col <= qi + w)
    rowc = lax.broadcasted_iota(jnp.int32, (grp * sl, 1), 0)

    def seq(b, carry):
        kb, kn = kb_ref[b], kn_ref[b]
        vb, vn = vb_ref[b], vn_ref[b]
        ko_ref[b, 0:w - sl, :] = kb_ref[b, sl:w, :]
        ko_ref[b, w - sl:w, :] = kn
        vo_ref[b, 0:w - sl, :] = vb_ref[b, sl:w, :]
        vo_ref[b, w - sl:w, :] = vn
        kk_ref[0:w, :] = kb
        kk_ref[w:w + sl, :] = kn
        vv_ref[0:w, :] = vb
        vv_ref[w:w + sl, :] = vn
        q = q_ref[b]
        for kh in range(SWA_KV_HEADS):
            ls = slice(kh * hd, (kh + 1) * hd)
            kkh = kk_ref[:, ls].astype(BF16)
            vvh = vv_ref[:, ls].astype(BF16)
            qs = jnp.concatenate([q[:, (kh * grp + g) * hd:(kh * grp + g + 1) * hd] for g in range(grp)], axis=0)
            s = jnp.where(mask, _dot_nt((qs * scale).astype(BF16), kkh), NEG)
            sk = jnp.zeros((grp * sl, 1), F32)
            for g in range(grp):
                sk = jnp.where(rowc // sl == g, sink_ref[kh * grp + g], sk)
            m = jnp.maximum(jnp.max(s, axis=1, keepdims=True), sk)
            p = jnp.exp(s - m)
            den = jnp.sum(p, axis=1, keepdims=True) + jnp.exp(sk - m)
            o = _dot(p.astype(BF16), vvh) / den
            for g in range(grp):
                h = kh * grp + g
                o_ref[b, :, h * hd:(h + 1) * hd] = o[g * sl:(g + 1) * sl, :]
        return carry

    lax.fori_loop(0, bs, seq, 0)


def swa_sample(q3, kn3, vn3, kbuf, vbuf, sink, bs=8):
    nseq, sl, qw = q3.shape
    w = WINDOW
    kvw = kn3.shape[-1]
    nk = 2 * w
    b3 = lambda shape: pl.BlockSpec(shape, lambda i: (i, 0, 0))
    return pl.pallas_call(
        functools.partial(_swa_sample_kernel, bs=bs, sl=sl),
        out_shape=[jax.ShapeDtypeStruct((nseq, sl, qw), F32),
                   jax.ShapeDtypeStruct((nseq, w, kvw), F32),
                   jax.ShapeDtypeStruct((nseq, w, kvw), F32)],
        grid=(nseq // bs,),
        in_specs=[pl.BlockSpec(memory_space=pltpu.SMEM),
                  b3((bs, sl, qw)), b3((bs, sl, kvw)), b3((bs, sl, kvw)), b3((bs, w, kvw)), b3((bs, w, kvw))],
        out_specs=[b3((bs, sl, qw)), b3((bs, w, kvw)), b3((bs, w, kvw))],
        scratch_shapes=[pltpu.VMEM((nk, kvw), F32), pltpu.VMEM((nk, kvw), F32)],
        compiler_params=_cparams(("arbitrary",)), name="swa_sample",
    )(sink, q3, kn3, vn3, kbuf, vbuf)


def _out_res_kernel(x_ref, a1_ref, w1_ref, a2_ref, w2_ref, o_ref, *, a2_t):
    a2 = a2_ref[...].astype(BF16)
    second = _dot_tn(a2, w2_ref[...]) if a2_t else _dot(a2, w2_ref[...])
    o_ref[...] = x_ref[...] + _dot(a1_ref[...].astype(BF16), w1_ref[...]) + second


def out_res(x2, a1, w1, a2, w2, a2_t=False, tm=512):
    m, d = x2.shape
    tm = min(tm, m)
    row = lambda width: pl.BlockSpec((tm, width), lambda i: (i, 0))
    a2_spec = pl.BlockSpec((a2.shape[0], tm), lambda i: (0, i)) if a2_t else row(a2.shape[1])
    return pl.pallas_call(
        functools.partial(_out_res_kernel, a2_t=a2_t),
        out_shape=jax.ShapeDtypeStruct((m, d), F32),
        grid=(m // tm,),
        in_specs=[row(d), row(a1.shape[1]), _const_spec(w1.shape), a2_spec, _const_spec(w2.shape)],
        out_specs=row(d),
        compiler_params=_cparams(("parallel",)), name="out_res",
    )(x2, a1, w1, a2, w2)


def _ffn_kernel(x_ref, g_ref, wg_ref, wu_ref, cw_ref, cb_ref, wd_ref, hist_ref, fg_ref,
                o_ref, hout_ref, ext_ref, act_ref, *, ns, tt, d, f, tf, final):
    t = pl.program_id(1)
    hk = FFN_CONV - 1

    @pl.when(t == 0)
    def _():
        ext_ref[:, SUBLANES - hk:SUBLANES, :] = hist_ref[...]

    x = x_ref[...].reshape(ns * tt, d)
    xn = _rms(x, g_ref[...]).astype(BF16)
    for c in range(f // tf):
        sl = slice(c * tf, (c + 1) * tf)
        g3 = _dot(xn, wg_ref[:, sl]).reshape(ns, tt, tf)
        ext_ref[:, SUBLANES:SUBLANES + tt, sl] = g3
        gc = cb_ref[:, sl] + g3 * cw_ref[hk:hk + 1, sl]
        for k in range(hk):
            gc = gc + ext_ref[:, SUBLANES - hk + k:SUBLANES - hk + k + tt, sl] * cw_ref[k:k + 1, sl]
        u = _dot(xn, wu_ref[:, sl])
        act_ref[:, sl] = (_silu(gc).reshape(ns * tt, tf) * u).astype(BF16)
    y = x + _dot(act_ref[...], wd_ref[...])
    if final:
        y = _rms(y, fg_ref[...])
    o_ref[...] = y.reshape(ns, tt, d)
    hout_ref[...] = ext_ref[:, SUBLANES + tt - hk:SUBLANES + tt, :]
    ext_ref[:, 0:SUBLANES, :] = ext_ref[:, tt:tt + SUBLANES, :]


def ffn(x3, g, wg, wu, cw, cb, wd, hist, fg, ns, tt, final, tf=256):
    nseq, length, d = x3.shape
    f = wg.shape[1]
    hk = FFN_CONV - 1
    blk = pl.BlockSpec((ns, tt, d), lambda s, t: (s, t, 0))
    hspec = pl.BlockSpec((ns, hk, f), lambda s, t: (s, 0, 0))
    return pl.pallas_call(
        functools.partial(_ffn_kernel, ns=ns, tt=tt, d=d, f=f, tf=tf, final=final),
        out_shape=[jax.ShapeDtypeStruct((nseq, length, d), F32), jax.ShapeDtypeStruct((nseq, hk, f), F32)],
        grid=(nseq // ns, length // tt),
        in_specs=[blk, _const_spec((1, d)), _const_spec(wg.shape), _const_spec(wu.shape),
                  _const_spec((FFN_CONV, f)), _const_spec((1, f)), _const_spec(wd.shape), hspec, _const_spec((1, d))],
        out_specs=[blk, hspec],
        scratch_shapes=[pltpu.VMEM((ns, SUBLANES + tt, f), F32), pltpu.VMEM((ns * tt, f), BF16)],
        compiler_params=_cparams(("arbitrary", "arbitrary")), name="ffn",
    )(x3, g.reshape(1, d), wg, wu, cw, cb.reshape(1, f), wd, hist, fg.reshape(1, d))


def _ssd_kernel(z_ref, xbc_ref, dt_ref, dtt_ref, hist_ref, s0_ref, cw_ref, cb_ref, dtb_ref, dtbt_ref,
                alog_ref, alogt_ref, dvec_ref, nw_ref,
                y_ref, conv_ref, sout_ref, ext_ref, st_ref, ybuf_ref, dtp_ref, dttp_ref, *, tin, tc, inner, nst):
    c = pl.program_id(1)
    carried = tin == tc
    hk = CONV_W - 1
    nh, hp = SSD_HEADS, SSD_HEAD_DIM
    gh = nh // SSD_GROUPS

    @pl.when(c == 0)
    def _():
        if tin < tc:
            ext_ref[...] = jnp.zeros(ext_ref.shape, F32)
            dtp_ref[...] = jnp.zeros(dtp_ref.shape, F32)
            dttp_ref[...] = jnp.zeros(dttp_ref.shape, F32)
        ext_ref[SUBLANES - hk:SUBLANES, :] = hist_ref[0]
        if carried:
            for h in range(nh):
                st_ref[:, h * hp:(h + 1) * hp] = s0_ref[0, h].T

    ext_ref[SUBLANES:SUBLANES + tin, :] = xbc_ref[0]
    xc = cb_ref[...]
    for k in range(CONV_W):
        xc = xc + ext_ref[SUBLANES - hk + k:SUBLANES - hk + k + tc, :] * cw_ref[k:k + 1, :]
    conv_ref[0] = ext_ref[SUBLANES + tin - hk:SUBLANES + tin, :]
    if tin == tc:
        ext_ref[0:SUBLANES, :] = ext_ref[tc:tc + SUBLANES, :]
    xa = _silu(xc)
    xs = xa[:, :inner]
    bm = [xa[:, inner + g * nst:inner + (g + 1) * nst].astype(BF16) for g in range(SSD_GROUPS)]
    cm = [xa[:, inner + (SSD_GROUPS + g) * nst:inner + (SSD_GROUPS + g + 1) * nst].astype(BF16) for g in range(SSD_GROUPS)]

    if tin < tc:
        dtp_ref[0:tin, :] = dt_ref[0]
        dttp_ref[:, 0:tin] = dtt_ref[0]
        dt_raw, dtt_raw = dtp_ref[...], dttp_ref[...]
        valid_r = lax.broadcasted_iota(jnp.int32, (tc, nh), 0) < tin
        valid_c = lax.broadcasted_iota(jnp.int32, (nh, tc), 1) < tin
        dt = jnp.where(valid_r, _softplus(dt_raw + dtb_ref[...]), 0.0)
        dtt = jnp.where(valid_c, _softplus(dtt_raw + dtbt_ref[...]), 0.0)
    else:
        dt = _softplus(dt_ref[0] + dtb_ref[...])
        dtt = _softplus(dtt_ref[0] + dtbt_ref[...])
    da = dt * (-jnp.exp(alog_ref[...]))
    dat = dtt * (-jnp.exp(alogt_ref[...]))

    li = lax.broadcasted_iota(jnp.int32, (tc, tc), 0)
    si = lax.broadcasted_iota(jnp.int32, (tc, tc), 1)
    causal = li >= si
    tri = jnp.where(causal, 1.0, 0.0).astype(BF16)
    trit = jnp.where(li <= si, 1.0, 0.0).astype(BF16)
    cs = sum(_dot(tri, p) for p in _split3(da))
    cst = sum(_dot(p, trit) for p in _split3(dat))
    cb_g =[_dot_nt(cm[g], bm[g]) for g in range(SSD_GROUPS)]

    gwid = inner // SSD_GROUPS
    if carried:
        erow = lax.broadcasted_iota(jnp.int32, (nh, inner), 0)
        ehead = lax.broadcasted_iota(jnp.int32, (nh, inner), 1) // hp
        expand = jnp.where(erow == ehead, 1.0, 0.0).astype(BF16)
        dt_e = sum(_dot(p, expand) for p in _split3(dt))
        cs_e = sum(_dot(p, expand) for p in _split3(cs))
        last_e = cs_e[tc - 1:tc, :]
        xd = xs * dt_e
        xdb = xd.astype(BF16)
        xdd = (xd * jnp.exp(last_e - cs_e)).astype(BF16)
        y_off = []
        for g in range(SSD_GROUPS):
            gs = slice(g * gwid, (g + 1) * gwid)
            st_g = st_ref[:, gs]
            y_off.append(_dot(cm[g], st_g.astype(BF16)))
            st_ref[:, gs] = jnp.exp(last_e[:, gs]) * st_g + _dot_tn(bm[g], xdd[:, gs])
        for h in range(nh):
            ls = slice(h * hp, (h + 1) * hp)
            lmat = jnp.exp(jnp.where(causal, cs[:, h:h + 1] - cst[h:h + 1, :], NEG))
            ybuf_ref[:, ls] = _dot((cb_g[h // gh] * lmat).astype(BF16), xdb[:, ls])
        y_all = ybuf_ref[...] + jnp.exp(cs_e) * jnp.concatenate(y_off, axis=1) + dvec_ref[...] * xs

        @pl.when(c == pl.num_programs(1) - 1)
        def _():
            for h in range(nh):
                sout_ref[0, h] = st_ref[:, h * hp:(h + 1) * hp].T
    else:
        for h in range(nh):
            g = h // gh
            ls = slice(h * hp, (h + 1) * hp)
            col = cs[:, h:h + 1]
            last = cs[tc - 1:tc, h:h + 1]
            lmat = jnp.exp(jnp.where(causal, col - cst[h:h + 1, :], NEG))
            xs_h = xs[:, ls]
            xd = xs_h * dt[:, h:h + 1]
            s_h = s0_ref[0, h]
            yh = _dot((cb_g[g] * lmat).astype(BF16), xd.astype(BF16))
            yh = yh + jnp.exp(col) * _dot_nt(cm[g], s_h.astype(BF16))
            sout_ref[0, h] = jnp.exp(last) * s_h + _dot_tn((xd * jnp.exp(last - col)).astype(BF16), bm[g])
            ybuf_ref[:, ls] = yh + dvec_ref[:, ls] * xs_h
        y_all = ybuf_ref[...]

    y = y_all[0:tin, :] * _silu(z_ref[0])
    for g in range(SSD_GROUPS):
        gs = slice(g * gwid, (g + 1) * gwid)
        y_ref[0, :, gs] = _rms(y[:, gs], nw_ref[:, gs])


def ssd(z3, xbc3, dt3, dtt3, hist, s0, cw, cb, dtb, alog, dvec, nw, tin):
    nseq, length, inner = z3.shape
    cd = xbc3.shape[-1]
    nh, hp, nst = s0.shape[1:]
    tc = SSD_CHUNK if tin == SSD_CHUNK else -(-tin // BF16_SUBLANES) * BF16_SUBLANES
    assert tin == SSD_CHUNK or tin == length
    hk = CONV_W - 1
    t3 = lambda width: pl.BlockSpec((1, tin, width), lambda s, c: (s, c, 0))
    per_seq = lambda shape: pl.BlockSpec((1,) + shape, lambda s, c: (s,) + (0,) * len(shape))
    return pl.pallas_call(
        functools.partial(_ssd_kernel, tin=tin, tc=tc, inner=inner, nst=nst),
        out_shape=[jax.ShapeDtypeStruct((nseq, length, inner), F32),
                   jax.ShapeDtypeStruct((nseq, hk, cd), F32),
                   jax.ShapeDtypeStruct(s0.shape, F32)],
        grid=(nseq, length // tin),
        in_specs=[t3(inner), t3(cd), t3(nh),
                  pl.BlockSpec((1, nh, tin), lambda s, c: (s, 0, c)),
                  per_seq((hk, cd)), per_seq((nh, hp, nst)),
                  _const_spec((CONV_W, cd)), _const_spec((1, cd)), _const_spec((1, nh)), _const_spec((nh, 1)),
                  _const_spec((1, nh)), _const_spec((nh, 1)), _const_spec((1, inner)), _const_spec((1, inner))],
        out_specs=[t3(inner), per_seq((hk, cd)), per_seq((nh, hp, nst))],
        scratch_shapes=[pltpu.VMEM((SUBLANES + tc, cd), F32), pltpu.VMEM((nst, inner), F32),
                        pltpu.VMEM((tc, inner), F32), pltpu.VMEM((tc, nh), F32), pltpu.VMEM((nh, tc), F32)],
        compiler_params=_cparams(("arbitrary", "arbitrary")), name="ssd",
    )(z3, xbc3, dt3, dtt3, hist, s0, cw, cb.reshape(1, cd), dtb.reshape(1, nh), dtb.reshape(nh, 1),
      alog.reshape(1, nh), alog.reshape(nh, 1), dvec.reshape(1, inner), nw.reshape(1, inner))


def _rope128(t, ctab, stab):
    half = QK_ROPE // 2
    lane = lax.broadcasted_iota(jnp.int32, t.shape, 1)
    swapped = jnp.where(lane < QK_NOPE + half, pltpu.roll(t, LANES - half, axis=1), pltpu.roll(t, half, axis=1))
    return t * ctab + swapped * stab


def _mla_prep_kernel(cq_ref, ckv_ref, kr_ref, ct_ref, st_ref, qg_ref, kg_ref, wuq_ref, wk_ref, wv_ref,
                     lat_ref, rope_ref, *outs, sample):
    scale = 1.0 / math.sqrt(QK_NOPE + QK_ROPE)
    ctab, stab = ct_ref[...], st_ref[...]
    qf = _dot(_rms(cq_ref[...], qg_ref[...]).astype(BF16), wuq_ref[...])
    ckv = _rms(ckv_ref[...], kg_ref[...])
    lat_ref[...] = ckv
    krr = _rope128(kr_ref[...], ctab, stab)
    rope_ref[...] = krr[:, QK_NOPE:QK_NOPE + QK_ROPE]
    if sample:
        qlat_ref, qr_ref = outs
        for h in range(MLA_HEADS):
            qh = _rope128(qf[:, h * LANES:(h + 1) * LANES], ctab, stab) * scale
            qlat_ref[h] = _dot(qh[:, :QK_NOPE].astype(BF16), wk_ref[h])
            qr_ref[h] = qh[:, QK_NOPE:QK_NOPE + QK_ROPE]
    else:
        qc_ref, kc_ref, vt_ref = outs
        ckvb = ckv.astype(BF16)
        vrows = vt_ref.shape[1]
        ones_row = jnp.where(lax.broadcasted_iota(jnp.int32, (vrows, ckvb.shape[0]), 0) == V_HEAD, 1.0, 0.0)
        for h in range(MLA_HEADS):
            qh = _rope128(qf[:, h * LANES:(h + 1) * LANES], ctab, stab) * (scale * LOG2E)
            qc_ref[h] = qh.astype(BF16)
            kc_ref[h] = (_dot(ckvb, wk_ref[h]) + krr).astype(BF16)
            vt_ref[h] = (_dot_nt(wv_ref[h], ckvb) + ones_row).astype(BF16)


def mla_prep(cq, ckv_raw, krp, ctab, stab, qg, kg, wuq_pad, wk, wv, sample, tm=512):
    m = cq.shape[0]
    tm = min(tm, m)
    nh = MLA_HEADS
    row = lambda width: pl.BlockSpec((tm, width), lambda i: (i, 0))
    hrow = lambda width: pl.BlockSpec((nh, tm, width), lambda i: (0, i, 0))
    lq, lkv = cq.shape[1], ckv_raw.shape[1]
    out_shape = [jax.ShapeDtypeStruct((m, lkv), F32), jax.ShapeDtypeStruct((m, QK_ROPE), F32)]
    out_specs = [row(lkv), row(QK_ROPE)]
    if sample:
        out_shape += [jax.ShapeDtypeStruct((nh, m, lkv), F32), jax.ShapeDtypeStruct((nh, m, QK_ROPE), F32)]
        out_specs += [hrow(lkv), hrow(QK_ROPE)]
    else:
        out_shape += [jax.ShapeDtypeStruct((nh, m, LANES), BF16)] * 2 + [jax.ShapeDtypeStruct((nh, VT_ROWS, m), BF16)]
        out_specs += [hrow(LANES)] * 2 + [pl.BlockSpec((nh, VT_ROWS, tm), lambda i: (0, 0, i))]
    return pl.pallas_call(
        functools.partial(_mla_prep_kernel, sample=sample),
        out_shape=out_shape, grid=(m // tm,),
        in_specs=[row(lq), row(lkv), row(LANES), row(LANES), row(LANES), _const_spec((1, lq)), _const_spec((1, lkv)),
                  _const_spec(wuq_pad.shape), _const_spec(wk.shape), _const_spec(wv.shape)],
        out_specs=out_specs,
        compiler_params=_cparams(("parallel",)), name="mla_prep",
    )(cq, ckv_raw, krp, ctab, stab, qg.reshape(1, lq), kg.reshape(1, lkv), wuq_pad, wk, wv)


def _mla_flash_kernel(qi_ref, kj_ref, q_ref, k_ref, vt_ref, o_ref, m_ref, acc_ref, *, t):
    n = pl.program_id(0)
    i, j = qi_ref[n], kj_ref[n]

    @pl.when(j == 0)
    def _():
        m_ref[...] = jnp.full(m_ref.shape, NEG, F32)
        acc_ref[...] = jnp.zeros(acc_ref.shape, F32)

    def tile(masked):
        if masked:
            keep = lax.broadcasted_iota(jnp.int32, (t, t), 0) <= lax.broadcasted_iota(jnp.int32, (t, t), 1)
        nxt = _dot_nt(k_ref[0], q_ref[0])
        pending = None
        for h in range(MLA_HEADS + 1):
            if h < MLA_HEADS:
                st = nxt
                if h + 1 < MLA_HEADS:
                    nxt = _dot_nt(k_ref[h + 1], q_ref[h + 1])
                if masked:
                    st = jnp.where(keep, st, NEG)
                m_prev = m_ref[h]
                m_new = jnp.maximum(m_prev, jnp.max(st, axis=0, keepdims=True))
                alpha = jnp.exp2(m_prev - m_new)
                p = jnp.exp2(st - m_new).astype(BF16)
                m_ref[h] = m_new
            if pending is not None:
                hp, alpha_p, p_p = pending
                acc_ref[hp] = alpha_p * acc_ref[hp] + _dot(vt_ref[hp], p_p)
            pending = (h, alpha, p) if h < MLA_HEADS else None

    @pl.when(j < i)
    def _():
        tile(False)

    @pl.when(j == i)
    def _():
        tile(True)
        for h in range(MLA_HEADS):
            a = acc_ref[h]
            o_ref[h * V_HEAD:(h + 1) * V_HEAD, :] = a[0:V_HEAD, :] / a[V_HEAD:V_HEAD + 1, :]


def mla_flash(qc, kc, vt, t=512):
    nh, length, _ = qc.shape
    t = min(t, length)
    nb = length // t
    qi = [i for i in range(nb) for _ in range(i + 1)]
    kj = [j for i in range(nb) for j in range(i + 1)]
    grid_spec = pltpu.PrefetchScalarGridSpec(
        num_scalar_prefetch=2, grid=(len(qi),),
        in_specs=[pl.BlockSpec((nh, t, LANES), lambda n, qi, kj: (0, qi[n], 0)),
                  pl.BlockSpec((nh, t, LANES), lambda n, qi, kj: (0, kj[n], 0)),
                  pl.BlockSpec((nh, VT_ROWS, t), lambda n, qi, kj: (0, 0, kj[n]))],
        out_specs=pl.BlockSpec((nh * V_HEAD, t), lambda n, qi, kj: (0, qi[n])),
        scratch_shapes=[pltpu.VMEM((nh, 1, t), F32), pltpu.VMEM((nh, VT_ROWS, t), F32)])
    return pl.pallas_call(
        functools.partial(_mla_flash_kernel, t=t),
        out_shape=jax.ShapeDtypeStruct((nh * V_HEAD, length), F32),
        grid_spec=grid_spec,
        compiler_params=_cparams(("arbitrary",)), name="mla_flash",
    )(jnp.asarray(qi, jnp.int32), jnp.asarray(kj, jnp.int32), qc, kc, vt)


def _mla_sample_kernel(pt_ref, qlat_ref, qr_ref, ckv_ref, kr_ref, wuv_ref, lat_hbm, rope_hbm,
                       o_ref, latbuf, ropebuf, sem, kbuf_ref, sbuf_ref, kl_ref, krl_ref,
                       *, cp, sl, npages, li):
    b, nb = pl.program_id(0), pl.num_programs(0)
    rows = MLA_HEADS * sl
    psz = latbuf.shape[2]
    pair = 2 * psz
    nch = npages // cp

    def chunk_copies(seq, ch, slot):
        out = []
        for c in range(cp):
            pid = pt_ref[seq, ch * cp + c]
            out.append(pltpu.make_async_copy(lat_hbm.at[li, pid], latbuf.at[slot, c], sem.at[0, slot]))
            out.append(pltpu.make_async_copy(rope_hbm.at[li, pid], ropebuf.at[slot, c], sem.at[1, slot]))
        return out

    def start(seq, ch, slot):
        for cpy in chunk_copies(seq, ch, slot):
            cpy.start()

    def wait(seq, ch, slot):
        for cpy in chunk_copies(seq, ch, slot):
            cpy.wait()

    nslots = latbuf.shape[0]
    ahead = nslots - 1

    @pl.when(b == 0)
    def _():
        for ch in range(ahead):
            start(0, ch, ch % nslots)

    q = qlat_ref[...].reshape(rows, qlat_ref.shape[-1]).astype(BF16)
    qr = qr_ref[...].reshape(rows, qr_ref.shape[-1]).astype(BF16)

    mrun = None
    for ch in range(nch):
        slot = ch % nslots
        wait(b, ch, slot)
        nxt = ch + ahead
        if nxt < nch:
            start(b, nxt, nxt % nslots)
        else:
            @pl.when(b + 1 < nb)
            def _():
                start(b + 1, nxt - nch, nxt % nslots)
        for c2 in range(cp // 2):
            base = (ch * cp + 2 * c2) * psz
            kk = jnp.concatenate([latbuf[slot, 2 * c2], latbuf[slot, 2 * c2 + 1]], axis=0).astype(BF16)
            kbuf_ref[base:base + pair, :] = kk
            rt = jnp.concatenate([ropebuf[slot, 2 * c2], ropebuf[slot, 2 * c2 + 1]], axis=1).astype(BF16)
            s = _dot_nt(q, kk) + _dot(qr, rt)
            sbuf_ref[:, base:base + pair] = s
            mrun = s if mrun is None else jnp.maximum(mrun, s)

    kl_ref[...] = jnp.zeros(kl_ref.shape, F32)
    krl_ref[...] = jnp.zeros(krl_ref.shape, F32)
    kl_ref[0:sl, :] = ckv_ref[0]
    krl_ref[0:sl, :] = kr_ref[0]
    kl = kl_ref[...].astype(BF16)
    kbuf_ref[npages * psz:npages * psz + pair, :] = kl
    s = _dot_nt(q, kl) + _dot_nt(qr, krl_ref[...].astype(BF16))
    row = lax.broadcasted_iota(jnp.int32, s.shape, 0)
    col = lax.broadcasted_iota(jnp.int32, s.shape, 1)
    s = jnp.where((col < sl) & (col <= row % sl), s, NEG)
    sbuf_ref[:, npages * psz:npages * psz + pair] = s

    m = jnp.broadcast_to(jnp.max(jnp.maximum(mrun, s), axis=1, keepdims=True), s.shape)
    nblk = sbuf_ref.shape[1] // pair
    hw = kbuf_ref.shape[1] // 2
    lacc = jnp.zeros(s.shape, F32)
    accs = [jnp.zeros((rows, hw), F32), jnp.zeros((rows, hw), F32)]
    for blk in range(nblk):
        cols = slice(blk * pair, (blk + 1) * pair)
        p = jnp.exp(sbuf_ref[:, cols] - m)
        lacc = lacc + p
        pb = p.astype(BF16)
        accs[0] = accs[0] + _dot(pb, kbuf_ref[cols, 0:hw])
        accs[1] = accs[1] + _dot(pb, kbuf_ref[cols, hw:2 * hw])
    lsum = jnp.sum(lacc, axis=1, keepdims=True)
    accn = (jnp.concatenate(accs, axis=1) / lsum).astype(BF16)
    for h in range(MLA_HEADS):
        o_ref[0, :, h * V_HEAD:(h + 1) * V_HEAD] = _dot(accn[h * sl:(h + 1) * sl, :], wuv_ref[h])


def mla_sample(qlat, qr, ckv3, kr3, wuv, cache_lat, cache_rope_t, page_table, li, cp=16):
    nh, m, lkv = qlat.shape
    nseq, sl, rd = kr3.shape
    npages = page_table.shape[1]
    nslots = MLA_PAGE_SLOTS
    cp = min(cp, npages // nslots)
    assert npages % (nslots * cp) == 0 and cp % 2 == 0
    psz = cache_lat.shape[2]
    nkeys = npages * psz + 2 * psz
    rows = nh * sl
    in_specs = [pl.BlockSpec((nh, sl, lkv), lambda b, pt: (0, b, 0)),
                pl.BlockSpec((nh, sl, rd), lambda b, pt: (0, b, 0)),
                pl.BlockSpec((1, sl, lkv), lambda b, pt: (b, 0, 0)),
                pl.BlockSpec((1, sl, rd), lambda b, pt: (b, 0, 0)),
                pl.BlockSpec(wuv.shape, lambda b, pt: (0, 0, 0)),
                pl.BlockSpec(memory_space=pl.ANY), pl.BlockSpec(memory_space=pl.ANY)]
    grid_spec = pltpu.PrefetchScalarGridSpec(
        num_scalar_prefetch=1, grid=(nseq,), in_specs=in_specs,
        out_specs=pl.BlockSpec((1, sl, nh * V_HEAD), lambda b, pt: (b, 0, 0)),
        scratch_shapes=[pltpu.VMEM((nslots, cp, psz, lkv), F32), pltpu.VMEM((nslots, cp, rd, psz), F32),
                        pltpu.SemaphoreType.DMA((2, nslots)),
                        pltpu.VMEM((nkeys, lkv), BF16), pltpu.VMEM((rows, nkeys), F32),
                        pltpu.VMEM((2 * psz, lkv), F32), pltpu.VMEM((2 * psz, rd), F32)])
    return pl.pallas_call(
        functools.partial(_mla_sample_kernel, cp=cp, sl=sl, npages=npages, li=li),
        out_shape=jax.ShapeDtypeStruct((nseq, sl, nh * V_HEAD), F32),
        grid_spec=grid_spec,
        compiler_params=_cparams(("arbitrary",)), name="mla_sample",
    )(page_table, qlat, qr, ckv3, kr3, wuv, cache_lat, cache_rope_t)


def _block_diag(w, per):
    nh, b, _ = w.shape
    w4 = w.reshape(nh // per, per, b, b)
    eye = jnp.eye(per, dtype=w.dtype)
    return jnp.einsum("gaij,ab->gaibj", w4, eye).reshape(nh // per, per * b, per * b)


def _rope_tables(pos):
    half = QK_ROPE // 2
    inv = jnp.exp(-(math.log(ROPE_THETA) / half) * jnp.arange(half, dtype=F32))
    ang = pos.astype(F32)[:, None] * inv[None, :]
    cos, sin = jnp.cos(ang), jnp.sin(ang)
    n = pos.shape[0]
    pad = LANES - QK_NOPE - QK_ROPE
    ctab = jnp.concatenate([jnp.ones((n, QK_NOPE), F32), cos, cos, jnp.ones((n, pad), F32)], axis=1)
    stab = jnp.concatenate([jnp.zeros((n, QK_NOPE), F32), -sin, sin, jnp.zeros((n, pad), F32)], axis=1)
    return ctab, stab


def _prep_weights(P):
    w = {}
    lw = P["lru_conv_w"].shape[-1]
    qw = SWA_HEADS * SWA_HEAD_DIM
    kvw = SWA_KV_HEADS * SWA_HEAD_DIM
    ew = P["even_w_in"][0].astype(BF16)
    b = [0, lw, 2 * lw, 2 * lw + qw, 2 * lw + qw + kvw, 2 * lw + qw + 2 * kvw]
    w["even_in"] = [ew[:, b[i]:b[i + 1]] for i in range(5)]
    per = 256 // (lw // LRU_HEADS)
    w["wa_bd"] = _block_diag(P["lru_wa"][0], per).astype(BF16)
    w["wx_bd"] = _block_diag(P["lru_wx"][0], per).astype(BF16)
    eo = P["even_w_out"][0].astype(BF16)
    w["even_out"] = (eo[:lw], eo[lw:])

    inner = P["ssd_norm"].shape[-1]
    cd = P["ssd_conv_w"].shape[-1]
    lq = P["mla_q_norm"].shape[-1]
    lkv = P["mla_kv_norm"].shape[-1]
    ow = P["odd_w_in"][0]
    b = [0, inner, inner + cd, inner + cd + SSD_HEADS, inner + cd + SSD_HEADS + lq,
         inner + cd + SSD_HEADS + lq + lkv, inner + cd + SSD_HEADS + lq + lkv + QK_ROPE]
    parts = [ow[:, b[i]:b[i + 1]] for i in range(6)]
    pad = LANES - QK_NOPE - QK_ROPE
    kr_pad = jnp.pad(parts[5], ((0, 0), (QK_NOPE, pad)))
    w["odd_in"] = [parts[0].astype(BF16), parts[1].astype(BF16), parts[2].astype(BF16), parts[2].T.astype(BF16),
                   parts[3].astype(BF16), parts[4].astype(BF16), kr_pad.astype(BF16)]
    uq = P["mla_w_uq"][0].reshape(lq, MLA_HEADS, QK_NOPE + QK_ROPE)
    w["wuq_pad"] = jnp.pad(uq, ((0, 0), (0, 0), (0, pad))).reshape(lq, MLA_HEADS * LANES).astype(BF16)
    uk = P["mla_w_uk"][0]
    uv = P["mla_w_uv"][0]
    w["wuk_pad"] = jnp.pad(jnp.transpose(uk, (1, 0, 2)), ((0, 0), (0, 0), (0, LANES - QK_NOPE))).astype(BF16)
    w["wuk_t"] = jnp.transpose(uk, (1, 2, 0)).astype(BF16)
    w["wuv_t_pad"] = jnp.pad(jnp.transpose(uv, (1, 2, 0)), ((0, 0), (0, VT_ROWS - V_HEAD), (0, 0))).astype(BF16)
    w["wuv"] = jnp.transpose(uv, (1, 0, 2)).astype(BF16)
    oo = P["odd_w_out"][0].astype(BF16)
    w["odd_out"] = (oo[:inner], oo[inner:])
    w["dvec"] = jnp.repeat(P["ssd_d"][0], SSD_HEAD_DIM)
    w["ffn"] = [(P["ffn_w_gate"][l].astype(BF16), P["ffn_w_up"][l].astype(BF16), P["ffn_w_down"][l].astype(BF16))
                for l in range(P["ffn_w_gate"].shape[0])]
    return w


def _trunk(x3, pos, P, W, st, sample):
    nseq, length, d = x3.shape
    m = nseq * length
    lw = P["lru_conv_w"].shape[-1]
    kvw = SWA_KV_HEADS * SWA_HEAD_DIM
    if sample:
        seq_tile, time_tile = min(nseq, 64), length
    else:
        seq_tile, time_tile = 1, min(length, 512)
    out = {}

    if sample:
        xr, gate, q, k, v = norm_proj(x3.reshape(m, d), P["mix_norm"][0], W["even_in"], [False] * 5)
    else:
        xr, gate, q, k, v, vt = norm_proj(x3.reshape(m, d), P["mix_norm"][0], W["even_in"] + [W["even_in"][4].T],
                                          [False] * 5 + [True])
    lru_ns, lru_tt = (min(nseq, 32), length) if sample else (1, min(length, 256))
    rec, out["lru_conv"], h_last = lru(
        xr.reshape(nseq, length, lw), gate.reshape(nseq, length, lw), st["lru_conv"], st["lru_h"].reshape(nseq, 1, lw),
        P["lru_conv_w"][0], P["lru_conv_b"][0], W["wa_bd"], P["lru_ba"][0], W["wx_bd"], P["lru_bx"][0],
        P["lru_lambda"][0], lru_ns, lru_tt)
    out["lru_h"] = h_last.reshape(nseq, lw)
    if sample:
        att, sk, sv = swa_sample(q.reshape(nseq, length, -1), k.reshape(nseq, length, kvw), v.reshape(nseq, length, kvw),
                                 st["swa_k"].reshape(nseq, WINDOW, kvw), st["swa_v"].reshape(nseq, WINDOW, kvw),
                                 P["swa_sink"][0], bs=min(nseq, 8))
        att = att.reshape(m, -1)
        out["swa_k"] = sk.reshape(nseq, WINDOW, SWA_KV_HEADS, SWA_HEAD_DIM)
        out["swa_v"] = sv.reshape(nseq, WINDOW, SWA_KV_HEADS, SWA_HEAD_DIM)
    else:
        att = swa_prompt(q, k, vt, P["swa_sink"][0])
        out["swa_k"] = k[-WINDOW:].reshape(1, WINDOW, SWA_KV_HEADS, SWA_HEAD_DIM)
        out["swa_v"] = v[-WINDOW:].reshape(1, WINDOW, SWA_KV_HEADS, SWA_HEAD_DIM)
    x2 = out_res(x3.reshape(m, d), rec.reshape(m, lw), W["even_out"][0], att, W["even_out"][1], a2_t=not sample)
    wg, wu, wd = W["ffn"][0]
    x3, fc0 = ffn(x2.reshape(nseq, length, d), P["ffn_norm"][0], wg, wu, P["ffn_conv_w"][0], P["ffn_conv_b"][0], wd,
                  st["ffn_conv"][0], P["final_norm"], seq_tile, time_tile, final=False)

    z, xbc, dt, dtt, cq, ckv_raw, krp = norm_proj(x3.reshape(m, d), P["mix_norm"][1], W["odd_in"],
                                                   [False, False, False, True, False, False, False])
    inner = z.shape[-1]
    cd = xbc.shape[-1]
    dtt3 = jnp.transpose(dtt.reshape(SSD_HEADS, nseq, length), (1, 0, 2))
    y, out["ssd_conv"], out["ssd"] = ssd(
        z.reshape(nseq, length, inner), xbc.reshape(nseq, length, cd), dt.reshape(nseq, length, SSD_HEADS), dtt3,
        st["ssd_conv"], st["ssd"], P["ssd_conv_w"][0], P["ssd_conv_b"][0], P["ssd_dt_bias"][0], P["ssd_a_log"][0],
        W["dvec"], P["ssd_norm"][0], tin=min(length, SSD_CHUNK))
    ctab, stab = _rope_tables(pos)
    if sample:
        lat, rp, qlat, qr = mla_prep(cq, ckv_raw, krp, ctab, stab, P["mla_q_norm"][0], P["mla_kv_norm"][0],
                                     W["wuq_pad"], W["wuk_t"], W["wuv"], sample=True)
        att = mla_sample(qlat, qr, lat.reshape(nseq, length, -1), rp.reshape(nseq, length, -1), W["wuv"],
                         st["mla_latent"], jnp.swapaxes(st["mla_rope"], 2, 3), st["page_table"], 0).reshape(m, -1)
    else:
        lat, rp, qc, kc, vt = mla_prep(cq, ckv_raw, krp, ctab, stab, P["mla_q_norm"][0], P["mla_kv_norm"][0],
                                       W["wuq_pad"], W["wuk_pad"], W["wuv_t_pad"], sample=False)
        att = mla_flash(qc, kc, vt)
    out["mla_latent"] = lat.reshape(nseq, length, -1)
    out["mla_rope"] = rp.reshape(nseq, length, -1)
    x2 = out_res(x3.reshape(m, d), y.reshape(m, inner), W["odd_out"][0], att, W["odd_out"][1], a2_t=not sample)
    wg, wu, wd = W["ffn"][1]
    y3, fc1 = ffn(x2.reshape(nseq, length, d), P["ffn_norm"][1], wg, wu, P["ffn_conv_w"][1], P["ffn_conv_b"][1], wd,
                  st["ffn_conv"][1], P["final_norm"], seq_tile, time_tile, final=True)
    out["ffn_conv"] = jnp.stack([fc0, fc1])
    return y3, out


def kernel(x_prompt, x_sample, state_lru_conv, state_lru_h, cache_swa_k, cache_swa_v, state_ssd_conv, state_ssd,
           cache_mla_latent, cache_mla_rope, page_table, state_ffn_conv, mix_norm, ffn_norm, final_norm, even_w_in,
           lru_conv_w, lru_conv_b, lru_wa, lru_ba, lru_wx, lru_bx, lru_lambda, swa_sink, even_w_out, odd_w_in,
           ssd_conv_w, ssd_conv_b, ssd_dt_bias, ssd_a_log, ssd_d, ssd_norm, mla_q_norm, mla_w_uq, mla_kv_norm,
           mla_w_uk, mla_w_uv, odd_w_out, ffn_w_gate, ffn_w_up, ffn_conv_w, ffn_conv_b, ffn_w_down):
    P = dict(mix_norm=mix_norm, ffn_norm=ffn_norm, final_norm=final_norm, even_w_in=even_w_in,
             lru_conv_w=lru_conv_w, lru_conv_b=lru_conv_b, lru_wa=lru_wa, lru_ba=lru_ba, lru_wx=lru_wx,
             lru_bx=lru_bx, lru_lambda=lru_lambda, swa_sink=swa_sink, even_w_out=even_w_out,
             odd_w_in=odd_w_in, ssd_conv_w=ssd_conv_w, ssd_conv_b=ssd_conv_b, ssd_dt_bias=ssd_dt_bias,
             ssd_a_log=ssd_a_log, ssd_d=ssd_d, ssd_norm=ssd_norm, mla_q_norm=mla_q_norm,
             mla_w_uq=mla_w_uq, mla_kv_norm=mla_kv_norm, mla_w_uk=mla_w_uk, mla_w_uv=mla_w_uv,
             odd_w_out=odd_w_out, ffn_w_gate=ffn_w_gate, ffn_w_up=ffn_w_up, ffn_conv_w=ffn_conv_w,
             ffn_conv_b=ffn_conv_b, ffn_w_down=ffn_w_down)
    W = _prep_weights(P)
    bp, lp, d = x_prompt.shape
    bs, ls, _ = x_sample.shape
    depth = ffn_w_gate.shape[0]
    lw = lru_conv_w.shape[-1]
    cd = ssd_conv_w.shape[-1]
    f = ffn_w_gate.shape[-1]
    kvw = SWA_KV_HEADS * SWA_HEAD_DIM
    past_len = page_table.shape[1] * PAGE_SIZE

    st_p = dict(lru_conv=jnp.zeros((bp, CONV_W - 1, lw), F32), lru_h=jnp.zeros((bp, lw), F32),
                ssd_conv=jnp.zeros((bp, CONV_W - 1, cd), F32),
                ssd=jnp.zeros((bp, SSD_HEADS, SSD_HEAD_DIM, SSD_STATE), F32),
                ffn_conv=jnp.zeros((depth, bp, FFN_CONV - 1, f), F32))
    st_s = dict(lru_conv=state_lru_conv[0], lru_h=state_lru_h[0], swa_k=cache_swa_k[0], swa_v=cache_swa_v[0],
                ssd_conv=state_ssd_conv[0], ssd=state_ssd[0], mla_latent=cache_mla_latent, mla_rope=cache_mla_rope,
                page_table=page_table, ffn_conv=state_ffn_conv)
    pos_p = jnp.tile(jnp.arange(lp), bp)
    pos_s = jnp.tile(past_len + jnp.arange(ls), bs)
    y_p, sp = _trunk(x_prompt, pos_p, P, W, st_p, False)
    y_s, ss = _trunk(x_sample, pos_s, P, W, st_s, True)
    e = lambda a: a[None]
    return (y_p, y_s,
            e(sp["lru_conv"]), e(ss["lru_conv"]), e(sp["lru_h"]), e(ss["lru_h"]),
            e(sp["swa_k"]), e(ss["swa_k"]), e(sp["swa_v"]), e(ss["swa_v"]),
            e(sp["ssd_conv"]), e(ss["ssd_conv"]), e(sp["ssd"]), e(ss["ssd"]),
            e(sp["mla_latent"]), e(ss["mla_latent"]), e(sp["mla_rope"]), e(ss["mla_rope"]),
            sp["ffn_conv"], ss["ffn_conv"])
```

```python
import functools
import math

import jax
import jax.numpy as jnp
from jax import lax
from jax.experimental import pallas as pl
from jax.experimental.pallas import tpu as pltpu

F32 = jnp.float32
BF16 = jnp.bfloat16

EPS = 1e-6
LRU_C = 8.0
LRU_HEADS = 16
CONV_W = 4
SWA_HEADS = 8
SWA_KV_HEADS = 2
SWA_HEAD_DIM = 64
WINDOW = 128
SSD_HEADS = 16
SSD_HEAD_DIM = 64
SSD_GROUPS = 2
SSD_STATE = 128
SSD_CHUNK = 128
MLA_HEADS = 8
QK_NOPE = 64
QK_ROPE = 32
V_HEAD = 64
ROPE_THETA = 10000.0
PAGE_SIZE = 128
FFN_CONV = 3
NEG = -1e30
LOG2E = math.log2(math.e)
VT_ROWS = 80
MLA_PAGE_SLOTS = 4

VMEM_LIMIT = 56 * 1024 * 1024
SUBLANES = 8
BF16_SUBLANES = 16
LANES = 128


def _cparams(sem):
    return pltpu.CompilerParams(dimension_semantics=sem, vmem_limit_bytes=VMEM_LIMIT)


def _dot(a, b):
    return jnp.dot(a, b, preferred_element_type=F32)


def _dot_nt(a, b):
    return lax.dot_general(a, b, (((1,), (1,)), ((), ())), preferred_element_type=F32)


def _dot_tn(a, b):
    return lax.dot_general(a, b, (((0,), (0,)), ((), ())), preferred_element_type=F32)


def _sigmoid(x):
    return 1.0 / (1.0 + jnp.exp(-x))


def _silu(x):
    return x * _sigmoid(x)


def _softplus(x):
    return jnp.maximum(x, 0.0) + jnp.log1p(jnp.exp(-jnp.abs(x)))


def _gelu_tanh(x):
    return 0.5 * x * (1.0 + jnp.tanh(math.sqrt(2.0 / math.pi) * (x + 0.044715 * (x * x * x))))


def _rms(x, g):
    return x * lax.rsqrt(jnp.mean(x * x, axis=-1, keepdims=True) + EPS) * g


def _split3(x):
    x1 = x.astype(BF16)
    r1 = x - x1.astype(F32)
    x2 = r1.astype(BF16)
    x3 = (r1 - x2.astype(F32)).astype(BF16)
    return x1, x2, x3


def _const_spec(shape):
    nd = len(shape)
    return pl.BlockSpec(shape, lambda *_: (0,) * nd, pipeline_mode=pl.Buffered(1))


def _norm_proj_kernel(x_ref, g_ref, *refs, nts):
    n = len(nts)
    w_refs, o_refs = refs[:n], refs[n:]
    xn = _rms(x_ref[...], g_ref[...]).astype(BF16)
    for w_ref, o_ref, nt in zip(w_refs, o_refs, nts):
        if nt:
            o_ref[...] = _dot_nt(w_ref[...], xn)
        else:
            o_ref[...] = _dot(xn, w_ref[...])


def norm_proj(x2, g, ws, nts, tm=512):
    m, k = x2.shape
    tm = min(tm, m)
    in_specs = [pl.BlockSpec((tm, k), lambda i: (i, 0)), _const_spec((1, k))]
    out_shape, out_specs = [], []
    for w, nt in zip(ws, nts):
        in_specs.append(_const_spec(w.shape))
        if nt:
            out_shape.append(jax.ShapeDtypeStruct((w.shape[0], m), F32))
            out_specs.append(pl.BlockSpec((w.shape[0], tm), lambda i: (0, i)))
        else:
            out_shape.append(jax.ShapeDtypeStruct((m, w.shape[1]), F32))
            out_specs.append(pl.BlockSpec((tm, w.shape[1]), lambda i: (i, 0)))
    return pl.pallas_call(
        functools.partial(_norm_proj_kernel, nts=tuple(nts)),
        out_shape=out_shape, grid=(m // tm,), in_specs=in_specs, out_specs=out_specs,
        compiler_params=_cparams(("parallel",)), name="norm_proj",
    )(x2, g.reshape(1, k), *ws)


def _lru_kernel(xr_ref, gate_ref, hist_ref, h0_ref, cw_ref, cb_ref, wa_ref, ba_ref, wx_ref, bx_ref, lam_ref,
                rec_ref, conv_ref, hlast_ref, ext_ref, a_ref, b_ref, h_ref, hprev_ref, *, ns, tt, c, gw):
    t = pl.program_id(1)
    hk = CONV_W - 1

    @pl.when(t == 0)
    def _():
        ext_ref[:, SUBLANES - hk:SUBLANES, :] = hist_ref[...]
        hprev_ref[...] = h0_ref[...]

    ext_ref[:, SUBLANES:SUBLANES + tt, :] = xr_ref[...]
    xc = cb_ref[...]
    for k in range(CONV_W):
        xc = xc + ext_ref[:, SUBLANES - hk + k:SUBLANES - hk + k + tt, :] * cw_ref[k:k + 1, :]
    conv_ref[...] = ext_ref[:, SUBLANES + tt - hk:SUBLANES + tt, :]
    ext_ref[:, 0:SUBLANES, :] = ext_ref[:, tt:tt + SUBLANES, :]

    x2 = xc.reshape(ns * tt, c)
    ra, rx = [], []
    for j in range(c // gw):
        xg = x2[:, j * gw:(j + 1) * gw].astype(BF16)
        ra.append(_dot(xg, wa_ref[j]))
        rx.append(_dot(xg, wx_ref[j]))
    r = _sigmoid(jnp.concatenate(ra, axis=1) + ba_ref[...])
    ig = _sigmoid(jnp.concatenate(rx, axis=1) + bx_ref[...])
    log_a = (-LRU_C) * r * _softplus(-lam_ref[...])
    a = jnp.exp(log_a)
    b = jnp.sqrt(-jnp.tanh(log_a) * (a * a + 1.0)) * (ig * x2)

    a3 = a.reshape(ns * tt // SUBLANES, SUBLANES, c)
    b3 = b.reshape(ns * tt // SUBLANES, SUBLANES, c)
    row = lax.broadcasted_iota(jnp.int32, a3.shape, 1)
    d = 1
    while d < SUBLANES:
        a_sh = jnp.where(row >= d, pltpu.roll(a3, d, axis=1), 1.0)
        b_sh = jnp.where(row >= d, pltpu.roll(b3, d, axis=1), 0.0)
        b3 = a3 * b_sh + b3
        a3 = a3 * a_sh
        d *= 2
    a_ref[...] = a3.reshape(ns, tt, c)
    b_ref[...] = b3.reshape(ns, tt, c)

    def slab(j, hp):
        s = pl.multiple_of(j * SUBLANES, SUBLANES)
        h8 = a_ref[:, pl.ds(s, SUBLANES), :] * hp + b_ref[:, pl.ds(s, SUBLANES), :]
        h_ref[:, pl.ds(s, SUBLANES), :] = h8
        return h8[:, SUBLANES - 1:SUBLANES, :]

    hp = lax.fori_loop(0, tt // SUBLANES, slab, hprev_ref[...])
    hprev_ref[...] = hp
    hlast_ref[...] = hp
    rec_ref[...] = h_ref[...] * _gelu_tanh(gate_ref[...])


def lru(xr3, gate3, hist, h0, cw, cb, wa_bd, ba, wx_bd, bx, lam, ns, tt):
    nseq, length, c = xr3.shape
    gw = wa_bd.shape[-1]
    hk = CONV_W - 1
    grid = (nseq // ns, length // tt)
    blk = pl.BlockSpec((ns, tt, c), lambda s, t: (s, t, 0))
    vec = _const_spec((1, c))
    return pl.pallas_call(
        functools.partial(_lru_kernel, ns=ns, tt=tt, c=c, gw=gw),
        out_shape=[jax.ShapeDtypeStruct((nseq, length, c), F32),
                   jax.ShapeDtypeStruct((nseq, hk, c), F32),
                   jax.ShapeDtypeStruct((nseq, 1, c), F32)],
        grid=grid,
        in_specs=[blk, blk,
                  pl.BlockSpec((ns, hk, c), lambda s, t: (s, 0, 0)),
                  pl.BlockSpec((ns, 1, c), lambda s, t: (s, 0, 0)),
                  _const_spec((CONV_W, c)), vec,
                  _const_spec(wa_bd.shape), vec, _const_spec(wx_bd.shape), vec, vec],
        out_specs=[blk,
                   pl.BlockSpec((ns, hk, c), lambda s, t: (s, 0, 0)),
                   pl.BlockSpec((ns, 1, c), lambda s, t: (s, 0, 0))],
        scratch_shapes=[pltpu.VMEM((ns, SUBLANES + tt, c), F32),
                        pltpu.VMEM((ns, tt, c), F32), pltpu.VMEM((ns, tt, c), F32), pltpu.VMEM((ns, tt, c), F32),
                        pltpu.VMEM((ns, 1, c), F32)],
        compiler_params=_cparams(("arbitrary", "arbitrary")), name="lru",
    )(xr3, gate3, hist, h0, cw, cb.reshape(1, c), wa_bd, ba.reshape(1, c), wx_bd, bx.reshape(1, c), lam.reshape(1, c))


def _swa_prompt_kernel(sink_ref, q_ref, kc_ref, kp_ref, vtc_ref, vtp_ref, o_ref):
    i = pl.program_id(0)
    w, hd = WINDOW, SWA_HEAD_DIM
    grp = SWA_HEADS // SWA_KV_HEADS
    nq = grp * w
    key = lax.broadcasted_iota(jnp.int32, (2 * w, nq), 0)
    qpos = lax.broadcasted_iota(jnp.int32, (2 * w, nq), 1) % w
    keep = ((key < w) & (key >= qpos) & (i > 0)) | ((key >= w) & ((key - w) <= qpos))
    colh = lax.broadcasted_iota(jnp.int32, (1, nq), 1) // w
    scale = 1.0 / math.sqrt(hd)
    for kh in range(SWA_KV_HEADS):
        ls = slice(kh * hd, (kh + 1) * hd)
        kk = jnp.concatenate([kp_ref[:, ls], kc_ref[:, ls]], axis=0).astype(BF16)
        q4 = jnp.concatenate([q_ref[:, (kh * grp + g) * hd:(kh * grp + g + 1) * hd] for g in range(grp)], axis=0)
        st = jnp.where(keep, _dot_nt(kk, (q4 * scale).astype(BF16)), NEG)
        sk = jnp.zeros((1, nq), F32)
        for g in range(grp):
            sk = jnp.where(colh == g, sink_ref[kh * grp + g], sk)
        m = jnp.maximum(jnp.max(st, axis=0, keepdims=True), sk)
        p = jnp.exp(st - m)
        den = jnp.sum(p, axis=0, keepdims=True) + jnp.exp(sk - m)
        vt = jnp.concatenate([vtp_ref[ls, :], vtc_ref[ls, :]], axis=1).astype(BF16)
        ot = _dot(vt, p.astype(BF16)) / den
        for g in range(grp):
            h = kh * grp + g
            o_ref[h * hd:(h + 1) * hd, :] = ot[:, g * w:(g + 1) * w]


def swa_prompt(q, k, vt, sink):
    length, qw = q.shape
    w = WINDOW
    kvw = SWA_KV_HEADS * SWA_HEAD_DIM
    cur = lambda i: (i, 0)
    prev = lambda i: (jnp.maximum(i - 1, 0), 0)
    cur_t = lambda i: (0, i)
    prev_t = lambda i: (0, jnp.maximum(i - 1, 0))
    return pl.pallas_call(
        _swa_prompt_kernel,
        out_shape=jax.ShapeDtypeStruct((qw, length), F32),
        grid=(length // w,),
        in_specs=[pl.BlockSpec(memory_space=pltpu.SMEM),
                  pl.BlockSpec((w, qw), cur),
                  pl.BlockSpec((w, kvw), cur), pl.BlockSpec((w, kvw), prev),
                  pl.BlockSpec((kvw, w), cur_t), pl.BlockSpec((kvw, w), prev_t)],
        out_specs=pl.BlockSpec((qw, w), cur_t),
        compiler_params=_cparams(("parallel",)), name="swa_prompt",
    )(sink, q, k, k, vt, vt)


def _swa_sample_kernel(sink_ref, q_ref, kn_ref, vn_ref, kb_ref, vb_ref, o_ref, ko_ref, vo_ref, kk_ref, vv_ref, *, bs, sl):
    w, hd = WINDOW, SWA_HEAD_DIM
    grp = SWA_HEADS // SWA_KV_HEADS
    nk = kk_ref.shape[0]
    scale = 1.0 / math.sqrt(hd)
    kk_ref[w + sl:nk, :] = jnp.zeros((nk - w - sl, kk_ref.shape[1]), F32)
    vv_ref[w + sl:nk, :] = jnp.zeros((nk - w - sl, vv_ref.shape[1]), F32)
    row = lax.broadcasted_iota(jnp.int32, (grp * sl, nk), 0)
    col = lax.broadcasted_iota(jnp.int32, (grp * sl, nk), 1)
    qi = row % sl
    mask = (col >= qi) & (col <= qi + w)
    rowc = lax.broadcasted_iota(jnp.int32, (grp * sl, 1), 0)

    def seq(b, carry):
        kb, kn = kb_ref[b], kn_ref[b]
        vb, vn = vb_ref[b], vn_ref[b]
        ko_ref[b, 0:w - sl, :] = kb_ref[b, sl:w, :]
        ko_ref[b, w - sl:w, :] = kn
        vo_ref[b, 0:w - sl, :] = vb_ref[b, sl:w, :]
        vo_ref[b, w - sl:w, :] = vn
        kk_ref[0:w, :] = kb
        kk_ref[w:w + sl, :] = kn
        vv_ref[0:w, :] = vb
        vv_ref[w:w + sl, :] = vn
        q = q_ref[b]
        for kh in range(SWA_KV_HEADS):
            ls = slice(kh * hd, (kh + 1) * hd)
            kkh = kk_ref[:, ls].astype(BF16)
            vvh = vv_ref[:, ls].astype(BF16)
            qs = jnp.concatenate([q[:, (kh * grp + g) * hd:(kh * grp + g + 1) * hd] for g in range(grp)], axis=0)
            s = jnp.where(mask, _dot_nt((qs * scale).astype(BF16), kkh), NEG)
            sk = jnp.zeros((grp * sl, 1), F32)
            for g in range(grp):
                sk = jnp.where(rowc // sl == g, sink_ref[kh * grp + g], sk)
            m = jnp.maximum(jnp.max(s, axis=1, keepdims=True), sk)
            p = jnp.exp(s - m)
            den = jnp.sum(p, axis=1, keepdims=True) + jnp.exp(sk - m)
            o = _dot(p.astype(BF16), vvh) / den
            for g in range(grp):
                h = kh * grp + g
                o_ref[b, :, h * hd:(h + 1) * hd] = o[g * sl:(g + 1) * sl, :]
        return carry

    lax.fori_loop(0, bs, seq, 0)


def swa_sample(q3, kn3, vn3, kbuf, vbuf, sink, bs=8):
    nseq, sl, qw = q3.shape
    w = WINDOW
    kvw = kn3.shape[-1]
    nk = 2 * w
    b3 = lambda shape: pl.BlockSpec(shape, lambda i: (i, 0, 0))
    return pl.pallas_call(
        functools.partial(_swa_sample_kernel, bs=bs, sl=sl),
        out_shape=[jax.ShapeDtypeStruct((nseq, sl, qw), F32),
                   jax.ShapeDtypeStruct((nseq, w, kvw), F32),
                   jax.ShapeDtypeStruct((nseq, w, kvw), F32)],
        grid=(nseq // bs,),
        in_specs=[pl.BlockSpec(memory_space=pltpu.SMEM),
                  b3((bs, sl, qw)), b3((bs, sl, kvw)), b3((bs, sl, kvw)), b3((bs, w, kvw)), b3((bs, w, kvw))],
        out_specs=[b3((bs, sl, qw)), b3((bs, w, kvw)), b3((bs, w, kvw))],
        scratch_shapes=[pltpu.VMEM((nk, kvw), F32), pltpu.VMEM((nk, kvw), F32)],
        compiler_params=_cparams(("arbitrary",)), name="swa_sample",
    )(sink, q3, kn3, vn3, kbuf, vbuf)


def _out_res_kernel(x_ref, a1_ref, w1_ref, a2_ref, w2_ref, o_ref, *, a2_t):
    a2 = a2_ref[...].astype(BF16)
    second = _dot_tn(a2, w2_ref[...]) if a2_t else _dot(a2, w2_ref[...])
    o_ref[...] = x_ref[...] + _dot(a1_ref[...].astype(BF16), w1_ref[...]) + second


def out_res(x2, a1, w1, a2, w2, a2_t=False, tm=512):
    m, d = x2.shape
    tm = min(tm, m)
    row = lambda width: pl.BlockSpec((tm, width), lambda i: (i, 0))
    a2_spec = pl.BlockSpec((a2.shape[0], tm), lambda i: (0, i)) if a2_t else row(a2.shape[1])
    return pl.pallas_call(
        functools.partial(_out_res_kernel, a2_t=a2_t),
        out_shape=jax.ShapeDtypeStruct((m, d), F32),
        grid=(m // tm,),
        in_specs=[row(d), row(a1.shape[1]), _const_spec(w1.shape), a2_spec, _const_spec(w2.shape)],
        out_specs=row(d),
        compiler_params=_cparams(("parallel",)), name="out_res",
    )(x2, a1, w1, a2, w2)


def _ffn_kernel(x_ref, g_ref, wg_ref, wu_ref, cw_ref, cb_ref, wd_ref, hist_ref, fg_ref,
                o_ref, hout_ref, ext_ref, act_ref, *, ns, tt, d, f, tf, final):
    t = pl.program_id(1)
    hk = FFN_CONV - 1

    @pl.when(t == 0)
    def _():
        ext_ref[:, SUBLANES - hk:SUBLANES, :] = hist_ref[...]

    x = x_ref[...].reshape(ns * tt, d)
    xn = _rms(x, g_ref[...]).astype(BF16)
    for c in range(f // tf):
        sl = slice(c * tf, (c + 1) * tf)
        g3 = _dot(xn, wg_ref[:, sl]).reshape(ns, tt, tf)
        ext_ref[:, SUBLANES:SUBLANES + tt, sl] = g3
        gc = cb_ref[:, sl] + g3 * cw_ref[hk:hk + 1, sl]
        for k in range(hk):
            gc = gc + ext_ref[:, SUBLANES - hk + k:SUBLANES - hk + k + tt, sl] * cw_ref[k:k + 1, sl]
        u = _dot(xn, wu_ref[:, sl])
        act_ref[:, sl] = (_silu(gc).reshape(ns * tt, tf) * u).astype(BF16)
    y = x + _dot(act_ref[...], wd_ref[...])
    if final:
        y = _rms(y, fg_ref[...])
    o_ref[...] = y.reshape(ns, tt, d)
    hout_ref[...] = ext_ref[:, SUBLANES + tt - hk:SUBLANES + tt, :]
    ext_ref[:, 0:SUBLANES, :] = ext_ref[:, tt:tt + SUBLANES, :]


def ffn(x3, g, wg, wu, cw, cb, wd, hist, fg, ns, tt, final, tf=256):
    nseq, length, d = x3.shape
    f = wg.shape[1]
    hk = FFN_CONV - 1
    blk = pl.BlockSpec((ns, tt, d), lambda s, t: (s, t, 0))
    hspec = pl.BlockSpec((ns, hk, f), lambda s, t: (s, 0, 0))
    return pl.pallas_call(
        functools.partial(_ffn_kernel, ns=ns, tt=tt, d=d, f=f, tf=tf, final=final),
        out_shape=[jax.ShapeDtypeStruct((nseq, length, d), F32), jax.ShapeDtypeStruct((nseq, hk, f), F32)],
        grid=(nseq // ns, length // tt),
        in_specs=[blk, _const_spec((1, d)), _const_spec(wg.shape), _const_spec(wu.shape),
                  _const_spec((FFN_CONV, f)), _const_spec((1, f)), _const_spec(wd.shape), hspec, _const_spec((1, d))],
        out_specs=[blk, hspec],
        scratch_shapes=[pltpu.VMEM((ns, SUBLANES + tt, f), F32), pltpu.VMEM((ns * tt, f), BF16)],
        compiler_params=_cparams(("arbitrary", "arbitrary")), name="ffn",
    )(x3, g.reshape(1, d), wg, wu, cw, cb.reshape(1, f), wd, hist, fg.reshape(1, d))


def _ssd_kernel(z_ref, xbc_ref, dt_ref, dtt_ref, hist_ref, s0_ref, cw_ref, cb_ref, dtb_ref, dtbt_ref,
                alog_ref, alogt_ref, dvec_ref, nw_ref,
                y_ref, conv_ref, sout_ref, ext_ref, st_ref, ybuf_ref, dtp_ref, dttp_ref, *, tin, tc, inner, nst):
    c = pl.program_id(1)
    carried = tin == tc
    hk = CONV_W - 1
    nh, hp = SSD_HEADS, SSD_HEAD_DIM
    gh = nh // SSD_GROUPS

    @pl.when(c == 0)
    def _():
        if tin < tc:
            ext_ref[...] = jnp.zeros(ext_ref.shape, F32)
            dtp_ref[...] = jnp.zeros(dtp_ref.shape, F32)
            dttp_ref[...] = jnp.zeros(dttp_ref.shape, F32)
        ext_ref[SUBLANES - hk:SUBLANES, :] = hist_ref[0]
        if carried:
            for h in range(nh):
                st_ref[:, h * hp:(h + 1) * hp] = s0_ref[0, h].T

    ext_ref[SUBLANES:SUBLANES + tin, :] = xbc_ref[0]
    xc = cb_ref[...]
    for k in range(CONV_W):
        xc = xc + ext_ref[SUBLANES - hk + k:SUBLANES - hk + k + tc, :] * cw_ref[k:k + 1, :]
    conv_ref[0] = ext_ref[SUBLANES + tin - hk:SUBLANES + tin, :]
    if tin == tc:
        ext_ref[0:SUBLANES, :] = ext_ref[tc:tc + SUBLANES, :]
    xa = _silu(xc)
    xs = xa[:, :inner]
    bm = [xa[:, inner + g * nst:inner + (g + 1) * nst].astype(BF16) for g in range(SSD_GROUPS)]
    cm = [xa[:, inner + (SSD_GROUPS + g) * nst:inner + (SSD_GROUPS + g + 1) * nst].astype(BF16) for g in range(SSD_GROUPS)]

    if tin < tc:
        dtp_ref[0:tin, :] = dt_ref[0]
        dttp_ref[:, 0:tin] = dtt_ref[0]
        dt_raw, dtt_raw = dtp_ref[...], dttp_ref[...]
        valid_r = lax.broadcasted_iota(jnp.int32, (tc, nh), 0) < tin
        valid_c = lax.broadcasted_iota(jnp.int32, (nh, tc), 1) < tin
        dt = jnp.where(valid_r, _softplus(dt_raw + dtb_ref[...]), 0.0)
        dtt = jnp.where(valid_c, _softplus(dtt_raw + dtbt_ref[...]), 0.0)
    else:
        dt = _softplus(dt_ref[0] + dtb_ref[...])
        dtt = _softplus(dtt_ref[0] + dtbt_ref[...])
    da = dt * (-jnp.exp(alog_ref[...]))
    dat = dtt * (-jnp.exp(alogt_ref[...]))

    li = lax.broadcasted_iota(jnp.int32, (tc, tc), 0)
    si = lax.broadcasted_iota(jnp.int32, (tc, tc), 1)
    causal = li >= si
    tri = jnp.where(causal, 1.0, 0.0).astype(BF16)
    trit = jnp.where(li <= si, 1.0, 0.0).astype(BF16)
    cs = sum(_dot(tri, p) for p in _split3(da))
    cst = sum(_dot(p, trit) for p in _split3(dat))
    cb_g =[_dot_nt(cm[g], bm[g]) for g in range(SSD_GROUPS)]

    gwid = inner // SSD_GROUPS
    if carried:
        erow = lax.broadcasted_iota(jnp.int32, (nh, inner), 0)
        ehead = lax.broadcasted_iota(jnp.int32, (nh, inner), 1) // hp
        expand = jnp.where(erow == ehead, 1.0, 0.0).astype(BF16)
        dt_e = sum(_dot(p, expand) for p in _split3(dt))
        cs_e = sum(_dot(p, expand) for p in _split3(cs))
        last_e = cs_e[tc - 1:tc, :]
        xd = xs * dt_e
        xdb = xd.astype(BF16)
        xdd = (xd * jnp.exp(last_e - cs_e)).astype(BF16)
        y_off = []
        for g in range(SSD_GROUPS):
            gs = slice(g * gwid, (g + 1) * gwid)
            st_g = st_ref[:, gs]
            y_off.append(_dot(cm[g], st_g.astype(BF16)))
            st_ref[:, gs] = jnp.exp(last_e[:, gs]) * st_g + _dot_tn(bm[g], xdd[:, gs])
        for h in range(nh):
            ls = slice(h * hp, (h + 1) * hp)
            lmat = jnp.exp(jnp.where(causal, cs[:, h:h + 1] - cst[h:h + 1, :], NEG))
            ybuf_ref[:, ls] = _dot((cb_g[h // gh] * lmat).astype(BF16), xdb[:, ls])
        y_all = ybuf_ref[...] + jnp.exp(cs_e) * jnp.concatenate(y_off, axis=1) + dvec_ref[...] * xs

        @pl.when(c == pl.num_programs(1) - 1)
        def _():
            for h in range(nh):
                sout_ref[0, h] = st_ref[:, h * hp:(h + 1) * hp].T
    else:
        for h in range(nh):
            g = h // gh
            ls = slice(h * hp, (h + 1) * hp)
            col = cs[:, h:h + 1]
            last = cs[tc - 1:tc, h:h + 1]
            lmat = jnp.exp(jnp.where(causal, col - cst[h:h + 1, :], NEG))
            xs_h = xs[:, ls]
            xd = xs_h * dt[:, h:h + 1]
            s_h = s0_ref[0, h]
            yh = _dot((cb_g[g] * lmat).astype(BF16), xd.astype(BF16))
            yh = yh + jnp.exp(col) * _dot_nt(cm[g], s_h.astype(BF16))
            sout_ref[0, h] = jnp.exp(last) * s_h + _dot_tn((xd * jnp.exp(last - col)).astype(BF16), bm[g])
            ybuf_ref[:, ls] = yh + dvec_ref[:, ls] * xs_h
        y_all = ybuf_ref[...]

    y = y_all[0:tin, :] * _silu(z_ref[0])
    for g in range(SSD_GROUPS):
        gs = slice(g * gwid, (g + 1) * gwid)
        y_ref[0, :, gs] = _rms(y[:, gs], nw_ref[:, gs])


def ssd(z3, xbc3, dt3, dtt3, hist, s0, cw, cb, dtb, alog, dvec, nw, tin):
    nseq, length, inner = z3.shape
    cd = xbc3.shape[-1]
    nh, hp, nst = s0.shape[1:]
    tc = SSD_CHUNK if tin == SSD_CHUNK else -(-tin // BF16_SUBLANES) * BF16_SUBLANES
    assert tin == SSD_CHUNK or tin == length
    hk = CONV_W - 1
    t3 = lambda width: pl.BlockSpec((1, tin, width), lambda s, c: (s, c, 0))
    per_seq = lambda shape: pl.BlockSpec((1,) + shape, lambda s, c: (s,) + (0,) * len(shape))
    return pl.pallas_call(
        functools.partial(_ssd_kernel, tin=tin, tc=tc, inner=inner, nst=nst),
        out_shape=[jax.ShapeDtypeStruct((nseq, length, inner), F32),
                   jax.ShapeDtypeStruct((nseq, hk, cd), F32),
                   jax.ShapeDtypeStruct(s0.shape, F32)],
        grid=(nseq, length // tin),
        in_specs=[t3(inner), t3(cd), t3(nh),
                  pl.BlockSpec((1, nh, tin), lambda s, c: (s, 0, c)),
                  per_seq((hk, cd)), per_seq((nh, hp, nst)),
                  _const_spec((CONV_W, cd)), _const_spec((1, cd)), _const_spec((1, nh)), _const_spec((nh, 1)),
                  _const_spec((1, nh)), _const_spec((nh, 1)), _const_spec((1, inner)), _const_spec((1, inner))],
        out_specs=[t3(inner), per_seq((hk, cd)), per_seq((nh, hp, nst))],
        scratch_shapes=[pltpu.VMEM((SUBLANES + tc, cd), F32), pltpu.VMEM((nst, inner), F32),
                        pltpu.VMEM((tc, inner), F32), pltpu.VMEM((tc, nh), F32), pltpu.VMEM((nh, tc), F32)],
        compiler_params=_cparams(("arbitrary", "arbitrary")), name="ssd",
    )(z3, xbc3, dt3, dtt3, hist, s0, cw, cb.reshape(1, cd), dtb.reshape(1, nh), dtb.reshape(nh, 1),
      alog.reshape(1, nh), alog.reshape(nh, 1), dvec.reshape(1, inner), nw.reshape(1, inner))


def _rope128(t, ctab, stab):
    half = QK_ROPE // 2
    lane = lax.broadcasted_iota(jnp.int32, t.shape, 1)
    swapped = jnp.where(lane < QK_NOPE + half, pltpu.roll(t, LANES - half, axis=1), pltpu.roll(t, half, axis=1))
    return t * ctab + swapped * stab


def _mla_prep_kernel(cq_ref, ckv_ref, kr_ref, ct_ref, st_ref, qg_ref, kg_ref, wuq_ref, wk_ref, wv_ref,
                     lat_ref, rope_ref, *outs, sample):
    scale = 1.0 / math.sqrt(QK_NOPE + QK_ROPE)
    ctab, stab = ct_ref[...], st_ref[...]
    qf = _dot(_rms(cq_ref[...], qg_ref[...]).astype(BF16), wuq_ref[...])
    ckv = _rms(ckv_ref[...], kg_ref[...])
    lat_ref[...] = ckv
    krr = _rope128(kr_ref[...], ctab, stab)
    rope_ref[...] = krr[:, QK_NOPE:QK_NOPE + QK_ROPE]
    if sample:
        qlat_ref, qr_ref = outs
        for h in range(MLA_HEADS):
            qh = _rope128(qf[:, h * LANES:(h + 1) * LANES], ctab, stab) * scale
            qlat_ref[h] = _dot(qh[:, :QK_NOPE].astype(BF16), wk_ref[h])
            qr_ref[h] = qh[:, QK_NOPE:QK_NOPE + QK_ROPE]
    else:
        qc_ref, kc_ref, vt_ref = outs
        ckvb = ckv.astype(BF16)
        vrows = vt_ref.shape[1]
        ones_row = jnp.where(lax.broadcasted_iota(jnp.int32, (vrows, ckvb.shape[0]), 0) == V_HEAD, 1.0, 0.0)
        for h in range(MLA_HEADS):
            qh = _rope128(qf[:, h * LANES:(h + 1) * LANES], ctab, stab) * (scale * LOG2E)
            qc_ref[h] = qh.astype(BF16)
            kc_ref[h] = (_dot(ckvb, wk_ref[h]) + krr).astype(BF16)
            vt_ref[h] = (_dot_nt(wv_ref[h], ckvb) + ones_row).astype(BF16)


def mla_prep(cq, ckv_raw, krp, ctab, stab, qg, kg, wuq_pad, wk, wv, sample, tm=512):
    m = cq.shape[0]
    tm = min(tm, m)
    nh = MLA_HEADS
    row = lambda width: pl.BlockSpec((tm, width), lambda i: (i, 0))
    hrow = lambda width: pl.BlockSpec((nh, tm, width), lambda i: (0, i, 0))
    lq, lkv = cq.shape[1], ckv_raw.shape[1]
    out_shape = [jax.ShapeDtypeStruct((m, lkv), F32), jax.ShapeDtypeStruct((m, QK_ROPE), F32)]
    out_specs = [row(lkv), row(QK_ROPE)]
    if sample:
        out_shape += [jax.ShapeDtypeStruct((nh, m, lkv), F32), jax.ShapeDtypeStruct((nh, m, QK_ROPE), F32)]
        out_specs += [hrow(lkv), hrow(QK_ROPE)]
    else:
        out_shape += [jax.ShapeDtypeStruct((nh, m, LANES), BF16)] * 2 + [jax.ShapeDtypeStruct((nh, VT_ROWS, m), BF16)]
        out_specs += [hrow(LANES)] * 2 + [pl.BlockSpec((nh, VT_ROWS, tm), lambda i: (0, 0, i))]
    return pl.pallas_call(
        functools.partial(_mla_prep_kernel, sample=sample),
        out_shape=out_shape, grid=(m // tm,),
        in_specs=[row(lq), row(lkv), row(LANES), row(LANES), row(LANES), _const_spec((1, lq)), _const_spec((1, lkv)),
                  _const_spec(wuq_pad.shape), _const_spec(wk.shape), _const_spec(wv.shape)],
        out_specs=out_specs,
        compiler_params=_cparams(("parallel",)), name="mla_prep",
    )(cq, ckv_raw, krp, ctab, stab, qg.reshape(1, lq), kg.reshape(1, lkv), wuq_pad, wk, wv)


def _mla_flash_kernel(qi_ref, kj_ref, q_ref, k_ref, vt_ref, o_ref, m_ref, acc_ref, *, t, nq):
    n = pl.program_id(0)
    i, j = qi_ref[n], kj_ref[n]

    @pl.when(j == 0)
    def _():
        m_ref[...] = jnp.full(m_ref.shape, NEG, F32)
        acc_ref[...] = jnp.zeros(acc_ref.shape, F32)

    def tile(u, masked):
        qs = slice(u * t, (u + 1) * t)
        if masked:
            keep = lax.broadcasted_iota(jnp.int32, (t, t), 0) <= lax.broadcasted_iota(jnp.int32, (t, t), 1)
        nxt = _dot_nt(k_ref[0], q_ref[0, qs, :])
        pending = None
        for h in range(MLA_HEADS + 1):
            if h < MLA_HEADS:
                st = nxt
                if h + 1 < MLA_HEADS:
                    nxt = _dot_nt(k_ref[h + 1], q_ref[h + 1, qs, :])
                if masked:
                    st = jnp.where(keep, st, NEG)
                m_prev = m_ref[h, :, qs]
                m_new = jnp.maximum(m_prev, jnp.max(st, axis=0, keepdims=True))
                alpha = jnp.exp2(m_prev - m_new)
                p = jnp.exp2(st - m_new).astype(BF16)
                m_ref[h, :, qs] = m_new
            if pending is not None:
                hp, alpha_p, p_p = pending
                acc_ref[hp, :, qs] = alpha_p * acc_ref[hp, :, qs] + _dot(vt_ref[hp], p_p)
            pending = (h, alpha, p) if h < MLA_HEADS else None

    for u in range(nq):
        qt = nq * i + u

        @pl.when(j < qt)
        def _():
            tile(u, False)

        @pl.when(j == qt)
        def _():
            tile(u, True)
            for h in range(MLA_HEADS):
                a = acc_ref[h, :, u * t:(u + 1) * t]
                o_ref[h * V_HEAD:(h + 1) * V_HEAD, u * t:(u + 1) * t] = a[0:V_HEAD, :] / a[V_HEAD:V_HEAD + 1, :]


def mla_flash(qc, kc, vt, t=512, nq=2):
    nh, length, _ = qc.shape
    t = min(t, length)
    nq = min(nq, length // t)
    nb = length // (t * nq)
    qi = [i for i in range(nb) for _ in range(nq * i + nq)]
    kj = [j for i in range(nb) for j in range(nq * i + nq)]
    grid_spec = pltpu.PrefetchScalarGridSpec(
        num_scalar_prefetch=2, grid=(len(qi),),
        in_specs=[pl.BlockSpec((nh, nq * t, LANES), lambda n, qi, kj: (0, qi[n], 0)),
                  pl.BlockSpec((nh, t, LANES), lambda n, qi, kj: (0, kj[n], 0)),
                  pl.BlockSpec((nh, VT_ROWS, t), lambda n, qi, kj: (0, 0, kj[n]))],
        out_specs=pl.BlockSpec((nh * V_HEAD, nq * t), lambda n, qi, kj: (0, qi[n])),
        scratch_shapes=[pltpu.VMEM((nh, 1, nq * t), F32), pltpu.VMEM((nh, VT_ROWS, nq * t), F32)])
    return pl.pallas_call(
        functools.partial(_mla_flash_kernel, t=t, nq=nq),
        out_shape=jax.ShapeDtypeStruct((nh * V_HEAD, length), F32),
        grid_spec=grid_spec,
        compiler_params=_cparams(("arbitrary",)), name="mla_flash",
    )(jnp.asarray(qi, jnp.int32), jnp.asarray(kj, jnp.int32), qc, kc, vt)


def _mla_sample_kernel(pt_ref, qlat_ref, qr_ref, ckv_ref, kr_ref, wuv_ref, lat_hbm, rope_hbm,
                       o_ref, latbuf, ropebuf, sem, kbuf_ref, sbuf_ref, kl_ref, krl_ref,
                       *, cp, sl, npages, li):
    b, nb = pl.program_id(0), pl.num_programs(0)
    rows = MLA_HEADS * sl
    psz = latbuf.shape[2]
    pair = 2 * psz
    nch = npages // cp

    def chunk_copies(seq, ch, slot):
        out = []
        for c in range(cp):
            pid = pt_ref[seq, ch * cp + c]
            out.append(pltpu.make_async_copy(lat_hbm.at[li, pid], latbuf.at[slot, c], sem.at[0, slot]))
            out.append(pltpu.make_async_copy(rope_hbm.at[li, pid], ropebuf.at[slot, c], sem.at[1, slot]))
        return out

    def start(seq, ch, slot):
        for cpy in chunk_copies(seq, ch, slot):
            cpy.start()

    def wait(seq, ch, slot):
        for cpy in chunk_copies(seq, ch, slot):
            cpy.wait()

    nslots = latbuf.shape[0]
    ahead = nslots - 1

    @pl.when(b == 0)
    def _():
        for ch in range(ahead):
            start(0, ch, ch % nslots)

    q = qlat_ref[...].reshape(rows, qlat_ref.shape[-1]).astype(BF16)
    qr = qr_ref[...].reshape(rows, qr_ref.shape[-1]).astype(BF16)

    mrun = None
    for ch in range(nch):
        slot = ch % nslots
        wait(b, ch, slot)
        nxt = ch + ahead
        if nxt < nch:
            start(b, nxt, nxt % nslots)
        else:
            @pl.when(b + 1 < nb)
            def _():
                start(b + 1, nxt - nch, nxt % nslots)
        for c2 in range(cp // 2):
            base = (ch * cp + 2 * c2) * psz
            kk = jnp.concatenate([latbuf[slot, 2 * c2], latbuf[slot, 2 * c2 + 1]], axis=0).astype(BF16)
            kbuf_ref[base:base + pair, :] = kk
            rt = jnp.concatenate([ropebuf[slot, 2 * c2], ropebuf[slot, 2 * c2 + 1]], axis=1).astype(BF16)
            s = _dot_nt(q, kk) + _dot(qr, rt)
            sbuf_ref[:, base:base + pair] = s
            mrun = s if mrun is None else jnp.maximum(mrun, s)

    kl_ref[...] = jnp.zeros(kl_ref.shape, F32)
    krl_ref[...] = jnp.zeros(krl_ref.shape, F32)
    kl_ref[0:sl, :] = ckv_ref[0]
    krl_ref[0:sl, :] = kr_ref[0]
    kl = kl_ref[...].astype(BF16)
    kbuf_ref[npages * psz:npages * psz + pair, :] = kl
    s = _dot_nt(q, kl) + _dot_nt(qr, krl_ref[...].astype(BF16))
    row = lax.broadcasted_iota(jnp.int32, s.shape, 0)
    col = lax.broadcasted_iota(jnp.int32, s.shape, 1)
    s = jnp.where((col < sl) & (col <= row % sl), s, NEG)
    sbuf_ref[:, npages * psz:npages * psz + pair] = s

    m = jnp.broadcast_to(jnp.max(jnp.maximum(mrun, s), axis=1, keepdims=True), s.shape)
    nblk = sbuf_ref.shape[1] // pair
    hw = kbuf_ref.shape[1] // 2
    lacc = jnp.zeros(s.shape, F32)
    accs = [jnp.zeros((rows, hw), F32), jnp.zeros((rows, hw), F32)]
    for blk in range(nblk):
        cols = slice(blk * pair, (blk + 1) * pair)
        p = jnp.exp(sbuf_ref[:, cols] - m)
        lacc = lacc + p
        pb = p.astype(BF16)
        accs[0] = accs[0] + _dot(pb, kbuf_ref[cols, 0:hw])
        accs[1] = accs[1] + _dot(pb, kbuf_ref[cols, hw:2 * hw])
    lsum = jnp.sum(lacc, axis=1, keepdims=True)
    accn = (jnp.concatenate(accs, axis=1) / lsum).astype(BF16)
    for h in range(MLA_HEADS):
        o_ref[0, :, h * V_HEAD:(h + 1) * V_HEAD] = _dot(accn[h * sl:(h + 1) * sl, :], wuv_ref[h])


def mla_sample(qlat, qr, ckv3, kr3, wuv, cache_lat, cache_rope_t, page_table, li, cp=16):
    nh, m, lkv = qlat.shape
    nseq, sl, rd = kr3.shape
    npages = page_table.shape[1]
    nslots = MLA_PAGE_SLOTS
    cp = min(cp, npages // nslots)
    assert npages % (nslots * cp) == 0 and cp % 2 == 0
    psz = cache_lat.shape[2]
    nkeys = npages * psz + 2 * psz
    rows = nh * sl
    in_specs = [pl.BlockSpec((nh, sl, lkv), lambda b, pt: (0, b, 0)),
                pl.BlockSpec((nh, sl, rd), lambda b, pt: (0, b, 0)),
                pl.BlockSpec((1, sl, lkv), lambda b, pt: (b, 0, 0)),
                pl.BlockSpec((1, sl, rd), lambda b, pt: (b, 0, 0)),
                pl.BlockSpec(wuv.shape, lambda b, pt: (0, 0, 0)),
                pl.BlockSpec(memory_space=pl.ANY), pl.BlockSpec(memory_space=pl.ANY)]
    grid_spec = pltpu.PrefetchScalarGridSpec(
        num_scalar_prefetch=1, grid=(nseq,), in_specs=in_specs,
        out_specs=pl.BlockSpec((1, sl, nh * V_HEAD), lambda b, pt: (b, 0, 0)),
        scratch_shapes=[pltpu.VMEM((nslots, cp, psz, lkv), F32), pltpu.VMEM((nslots, cp, rd, psz), F32),
                        pltpu.SemaphoreType.DMA((2, nslots)),
                        pltpu.VMEM((nkeys, lkv), BF16), pltpu.VMEM((rows, nkeys), F32),
                        pltpu.VMEM((2 * psz, lkv), F32), pltpu.VMEM((2 * psz, rd), F32)])
    return pl.pallas_call(
        functools.partial(_mla_sample_kernel, cp=cp, sl=sl, npages=npages, li=li),
        out_shape=jax.ShapeDtypeStruct((nseq, sl, nh * V_HEAD), F32),
        grid_spec=grid_spec,
        compiler_params=_cparams(("arbitrary",)), name="mla_sample",
    )(page_table, qlat, qr, ckv3, kr3, wuv, cache_lat, cache_rope_t)


def _block_diag(w, per):
    nh, b, _ = w.shape
    w4 = w.reshape(nh // per, per, b, b)
    eye = jnp.eye(per, dtype=w.dtype)
    return jnp.einsum("gaij,ab->gaibj", w4, eye).reshape(nh // per, per * b, per * b)


def _rope_tables(pos):
    half = QK_ROPE // 2
    inv = jnp.exp(-(math.log(ROPE_THETA) / half) * jnp.arange(half, dtype=F32))
    ang = pos.astype(F32)[:, None] * inv[None, :]
    cos, sin = jnp.cos(ang), jnp.sin(ang)
    n = pos.shape[0]
    pad = LANES - QK_NOPE - QK_ROPE
    ctab = jnp.concatenate([jnp.ones((n, QK_NOPE), F32), cos, cos, jnp.ones((n, pad), F32)], axis=1)
    stab = jnp.concatenate([jnp.zeros((n, QK_NOPE), F32), -sin, sin, jnp.zeros((n, pad), F32)], axis=1)
    return ctab, stab


def _prep_weights(P):
    w = {}
    lw = P["lru_conv_w"].shape[-1]
    qw = SWA_HEADS * SWA_HEAD_DIM
    kvw = SWA_KV_HEADS * SWA_HEAD_DIM
    ew = P["even_w_in"][0].astype(BF16)
    b = [0, lw, 2 * lw, 2 * lw + qw, 2 * lw + qw + kvw, 2 * lw + qw + 2 * kvw]
    w["even_in"] = [ew[:, b[i]:b[i + 1]] for i in range(5)]
    per = 256 // (lw // LRU_HEADS)
    w["wa_bd"] = _block_diag(P["lru_wa"][0], per).astype(BF16)
    w["wx_bd"] = _block_diag(P["lru_wx"][0], per).astype(BF16)
    eo = P["even_w_out"][0].astype(BF16)
    w["even_out"] = (eo[:lw], eo[lw:])

    inner = P["ssd_norm"].shape[-1]
    cd = P["ssd_conv_w"].shape[-1]
    lq = P["mla_q_norm"].shape[-1]
    lkv = P["mla_kv_norm"].shape[-1]
    ow = P["odd_w_in"][0]
    b = [0, inner, inner + cd, inner + cd + SSD_HEADS, inner + cd + SSD_HEADS + lq,
         inner + cd + SSD_HEADS + lq + lkv, inner + cd + SSD_HEADS + lq + lkv + QK_ROPE]
    parts = [ow[:, b[i]:b[i + 1]] for i in range(6)]
    pad = LANES - QK_NOPE - QK_ROPE
    kr_pad = jnp.pad(parts[5], ((0, 0), (QK_NOPE, pad)))
    w["odd_in"] = [parts[0].astype(BF16), parts[1].astype(BF16), parts[2].astype(BF16), parts[2].T.astype(BF16),
                   parts[3].astype(BF16), parts[4].astype(BF16), kr_pad.astype(BF16)]
    uq = P["mla_w_uq"][0].reshape(lq, MLA_HEADS, QK_NOPE + QK_ROPE)
    w["wuq_pad"] = jnp.pad(uq, ((0, 0), (0, 0), (0, pad))).reshape(lq, MLA_HEADS * LANES).astype(BF16)
    uk = P["mla_w_uk"][0]
    uv = P["mla_w_uv"][0]
    w["wuk_pad"] = jnp.pad(jnp.transpose(uk, (1, 0, 2)), ((0, 0), (0, 0), (0, LANES - QK_NOPE))).astype(BF16)
    w["wuk_t"] = jnp.transpose(uk, (1, 2, 0)).astype(BF16)
    w["wuv_t_pad"] = jnp.pad(jnp.transpose(uv, (1, 2, 0)), ((0, 0), (0, VT_ROWS - V_HEAD), (0, 0))).astype(BF16)
    w["wuv"] = jnp.transpose(uv, (1, 0, 2)).astype(BF16)
    oo = P["odd_w_out"][0].astype(BF16)
    w["odd_out"] = (oo[:inner], oo[inner:])
    w["dvec"] = jnp.repeat(P["ssd_d"][0], SSD_HEAD_DIM)
    w["ffn"] = [(P["ffn_w_gate"][l].astype(BF16), P["ffn_w_up"][l].astype(BF16), P["ffn_w_down"][l].astype(BF16))
                for l in range(P["ffn_w_gate"].shape[0])]
    return w


def _trunk(x3, pos, P, W, st, sample):
    nseq, length, d = x3.shape
    m = nseq * length
    lw = P["lru_conv_w"].shape[-1]
    kvw = SWA_KV_HEADS * SWA_HEAD_DIM
    if sample:
        seq_tile, time_tile = min(nseq, 64), length
    else:
        seq_tile, time_tile = 1, min(length, 512)
    out = {}

    if sample:
        xr, gate, q, k, v = norm_proj(x3.reshape(m, d), P["mix_norm"][0], W["even_in"], [False] * 5)
    else:
        xr, gate, q, k, v, vt = norm_proj(x3.reshape(m, d), P["mix_norm"][0], W["even_in"] + [W["even_in"][4].T],
                                          [False] * 5 + [True])
    lru_ns, lru_tt = (min(nseq, 32), length) if sample else (1, min(length, 256))
    rec, out["lru_conv"], h_last = lru(
        xr.reshape(nseq, length, lw), gate.reshape(nseq, length, lw), st["lru_conv"], st["lru_h"].reshape(nseq, 1, lw),
        P["lru_conv_w"][0], P["lru_conv_b"][0], W["wa_bd"], P["lru_ba"][0], W["wx_bd"], P["lru_bx"][0],
        P["lru_lambda"][0], lru_ns, lru_tt)
    out["lru_h"] = h_last.reshape(nseq, lw)
    if sample:
        att, sk, sv = swa_sample(q.reshape(nseq, length, -1), k.reshape(nseq, length, kvw), v.reshape(nseq, length, kvw),
                                 st["swa_k"].reshape(nseq, WINDOW, kvw), st["swa_v"].reshape(nseq, WINDOW, kvw),
                                 P["swa_sink"][0], bs=min(nseq, 8))
        att = att.reshape(m, -1)
        out["swa_k"] = sk.reshape(nseq, WINDOW, SWA_KV_HEADS, SWA_HEAD_DIM)
        out["swa_v"] = sv.reshape(nseq, WINDOW, SWA_KV_HEADS, SWA_HEAD_DIM)
    else:
        att = swa_prompt(q, k, vt, P["swa_sink"][0])
        out["swa_k"] = k[-WINDOW:].reshape(1, WINDOW, SWA_KV_HEADS, SWA_HEAD_DIM)
        out["swa_v"] = v[-WINDOW:].reshape(1, WINDOW, SWA_KV_HEADS, SWA_HEAD_DIM)
    x2 = out_res(x3.reshape(m, d), rec.reshape(m, lw), W["even_out"][0], att, W["even_out"][1], a2_t=not sample)
    wg, wu, wd = W["ffn"][0]
    x3, fc0 = ffn(x2.reshape(nseq, length, d), P["ffn_norm"][0], wg, wu, P["ffn_conv_w"][0], P["ffn_conv_b"][0], wd,
                  st["ffn_conv"][0], P["final_norm"], seq_tile, time_tile, final=False)

    z, xbc, dt, dtt, cq, ckv_raw, krp = norm_proj(x3.reshape(m, d), P["mix_norm"][1], W["odd_in"],
                                                   [False, False, False, True, False, False, False])
    inner = z.shape[-1]
    cd = xbc.shape[-1]
    dtt3 = jnp.transpose(dtt.reshape(SSD_HEADS, nseq, length), (1, 0, 2))
    y, out["ssd_conv"], out["ssd"] = ssd(
        z.reshape(nseq, length, inner), xbc.reshape(nseq, length, cd), dt.reshape(nseq, length, SSD_HEADS), dtt3,
        st["ssd_conv"], st["ssd"], P["ssd_conv_w"][0], P["ssd_conv_b"][0], P["ssd_dt_bias"][0], P["ssd_a_log"][0],
        W["dvec"], P["ssd_norm"][0], tin=min(length, SSD_CHUNK))
    ctab, stab = _rope_tables(pos)
    if sample:
        lat, rp, qlat, qr = mla_prep(cq, ckv_raw, krp, ctab, stab, P["mla_q_norm"][0], P["mla_kv_norm"][0],
                                     W["wuq_pad"], W["wuk_t"], W["wuv"], sample=True)
        att = mla_sample(qlat, qr, lat.reshape(nseq, length, -1), rp.reshape(nseq, length, -1), W["wuv"],
                         st["mla_latent"], jnp.swapaxes(st["mla_rope"], 2, 3), st["page_table"], 0).reshape(m, -1)
    else:
        lat, rp, qc, kc, vt = mla_prep(cq, ckv_raw, krp, ctab, stab, P["mla_q_norm"][0], P["mla_kv_norm"][0],
                                       W["wuq_pad"], W["wuk_pad"], W["wuv_t_pad"], sample=False)
        att = mla_flash(qc, kc, vt)
    out["mla_latent"] = lat.reshape(nseq, length, -1)
    out["mla_rope"] = rp.reshape(nseq, length, -1)
    x2 = out_res(x3.reshape(m, d), y.reshape(m, inner), W["odd_out"][0], att, W["odd_out"][1], a2_t=not sample)
    wg, wu, wd = W["ffn"][1]
    y3, fc1 = ffn(x2.reshape(nseq, length, d), P["ffn_norm"][1], wg, wu, P["ffn_conv_w"][1], P["ffn_conv_b"][1], wd,
                  st["ffn_conv"][1], P["final_norm"], seq_tile, time_tile, final=True)
    out["ffn_conv"] = jnp.stack([fc0, fc1])
    return y3, out


def kernel(x_prompt, x_sample, state_lru_conv, state_lru_h, cache_swa_k, cache_swa_v, state_ssd_conv, state_ssd,
           cache_mla_latent, cache_mla_rope, page_table, state_ffn_conv, mix_norm, ffn_norm, final_norm, even_w_in,
           lru_conv_w, lru_conv_b, lru_wa, lru_ba, lru_wx, lru_bx, lru_lambda, swa_sink, even_w_out, odd_w_in,
           ssd_conv_w, ssd_conv_b, ssd_dt_bias, ssd_a_log, ssd_d, ssd_norm, mla_q_norm, mla_w_uq, mla_kv_norm,
           mla_w_uk, mla_w_uv, odd_w_out, ffn_w_gate, ffn_w_up, ffn_conv_w, ffn_conv_b, ffn_w_down):
    P = dict(mix_norm=mix_norm, ffn_norm=ffn_norm, final_norm=final_norm, even_w_in=even_w_in,
             lru_conv_w=lru_conv_w, lru_conv_b=lru_conv_b, lru_wa=lru_wa, lru_ba=lru_ba, lru_wx=lru_wx,
             lru_bx=lru_bx, lru_lambda=lru_lambda, swa_sink=swa_sink, even_w_out=even_w_out,
             odd_w_in=odd_w_in, ssd_conv_w=ssd_conv_w, ssd_conv_b=ssd_conv_b, ssd_dt_bias=ssd_dt_bias,
             ssd_a_log=ssd_a_log, ssd_d=ssd_d, ssd_norm=ssd_norm, mla_q_norm=mla_q_norm,
             mla_w_uq=mla_w_uq, mla_kv_norm=mla_kv_norm, mla_w_uk=mla_w_uk, mla_w_uv=mla_w_uv,
             odd_w_out=odd_w_out, ffn_w_gate=ffn_w_gate, ffn_w_up=ffn_w_up, ffn_conv_w=ffn_conv_w,
             ffn_conv_b=ffn_conv_b, ffn_w_down=ffn_w_down)
    W = _prep_weights(P)
    bp, lp, d = x_prompt.shape
    bs, ls, _ = x_sample.shape
    depth = ffn_w_gate.shape[0]
    lw = lru_conv_w.shape[-1]
    cd = ssd_conv_w.shape[-1]
    f = ffn_w_gate.shape[-1]
    kvw = SWA_KV_HEADS * SWA_HEAD_DIM
    past_len = page_table.shape[1] * PAGE_SIZE

    st_p = dict(lru_conv=jnp.zeros((bp, CONV_W - 1, lw), F32), lru_h=jnp.zeros((bp, lw), F32),
                ssd_conv=jnp.zeros((bp, CONV_W - 1, cd), F32),
                ssd=jnp.zeros((bp, SSD_HEADS, SSD_HEAD_DIM, SSD_STATE), F32),
                ffn_conv=jnp.zeros((depth, bp, FFN_CONV - 1, f), F32))
    st_s = dict(lru_conv=state_lru_conv[0], lru_h=state_lru_h[0], swa_k=cache_swa_k[0], swa_v=cache_swa_v[0],
                ssd_conv=state_ssd_conv[0], ssd=state_ssd[0], mla_latent=cache_mla_latent, mla_rope=cache_mla_rope,
                page_table=page_table, ffn_conv=state_ffn_conv)
    pos_p = jnp.tile(jnp.arange(lp), bp)
    pos_s = jnp.tile(past_len + jnp.arange(ls), bs)
    y_p, sp = _trunk(x_prompt, pos_p, P, W, st_p, False)
    y_s, ss = _trunk(x_sample, pos_s, P, W, st_s, True)
    e = lambda a: a[None]
    return (y_p, y_s,
            e(sp["lru_conv"]), e(ss["lru_conv"]), e(sp["lru_h"]), e(ss["lru_h"]),
            e(sp["swa_k"]), e(ss["swa_k"]), e(sp["swa_v"]), e(ss["swa_v"]),
            e(sp["ssd_conv"]), e(ss["ssd_conv"]), e(sp["ssd"]), e(ss["ssd"]),
            e(sp["mla_latent"]), e(ss["mla_latent"]), e(sp["mla_rope"]), e(ss["mla_rope"]),
            sp["ffn_conv"], ss["ffn_conv"])
```

```python
import functools
import math

import jax
import jax.numpy as jnp
from jax import lax
from jax.experimental import pallas as pl
from jax.experimental.pallas import tpu as pltpu

F32 = jnp.float32
BF16 = jnp.bfloat16

EPS = 1e-6
LRU_C = 8.0
LRU_HEADS = 16
CONV_W = 4
SWA_HEADS = 8
SWA_KV_HEADS = 2
SWA_HEAD_DIM = 64
WINDOW = 128
SSD_HEADS = 16
SSD_HEAD_DIM = 64
SSD_GROUPS = 2
SSD_STATE = 128
SSD_CHUNK = 128
MLA_HEADS = 8
QK_NOPE = 64
QK_ROPE = 32
V_HEAD = 64
ROPE_THETA = 10000.0
PAGE_SIZE = 128
FFN_CONV = 3
NEG = -1e30
LOG2E = math.log2(math.e)
VT_ROWS = 80
MLA_PAGE_SLOTS = 4

VMEM_LIMIT = 56 * 1024 * 1024
SUBLANES = 8
BF16_SUBLANES = 16
LANES = 128


def _cparams(sem):
    return pltpu.CompilerParams(dimension_semantics=sem, vmem_limit_bytes=VMEM_LIMIT)


def _dot(a, b):
    return jnp.dot(a, b, preferred_element_type=F32)


def _dot_nt(a, b):
    return lax.dot_general(a, b, (((1,), (1,)), ((), ())), preferred_element_type=F32)


def _dot_tn(a, b):
    return lax.dot_general(a, b, (((0,), (0,)), ((), ())), preferred_element_type=F32)


def _sigmoid(x):
    return 1.0 / (1.0 + jnp.exp(-x))


def _silu(x):
    return x * _sigmoid(x)


def _softplus(x):
    return jnp.maximum(x, 0.0) + jnp.log1p(jnp.exp(-jnp.abs(x)))


def _gelu_tanh(x):
    return 0.5 * x * (1.0 + jnp.tanh(math.sqrt(2.0 / math.pi) * (x + 0.044715 * (x * x * x))))


def _rms(x, g):
    return x * lax.rsqrt(jnp.mean(x * x, axis=-1, keepdims=True) + EPS) * g


def _split3(x):
    x1 = x.astype(BF16)
    r1 = x - x1.astype(F32)
    x2 = r1.astype(BF16)
    x3 = (r1 - x2.astype(F32)).astype(BF16)
    return x1, x2, x3


def _const_spec(shape):
    nd = len(shape)
    return pl.BlockSpec(shape, lambda *_: (0,) * nd, pipeline_mode=pl.Buffered(1))


def _norm_proj_kernel(x_ref, g_ref, *refs, nts):
    n = len(nts)
    w_refs, o_refs = refs[:n], refs[n:]
    xn = _rms(x_ref[...], g_ref[...]).astype(BF16)
    for w_ref, o_ref, nt in zip(w_refs, o_refs, nts):
        if nt:
            o_ref[...] = _dot_nt(w_ref[...], xn)
        else:
            o_ref[...] = _dot(xn, w_ref[...])


def norm_proj(x2, g, ws, nts, tm=512):
    m, k = x2.shape
    tm = min(tm, m)
    in_specs = [pl.BlockSpec((tm, k), lambda i: (i, 0)), _const_spec((1, k))]
    out_shape, out_specs = [], []
    for w, nt in zip(ws, nts):
        in_specs.append(_const_spec(w.shape))
        if nt:
            out_shape.append(jax.ShapeDtypeStruct((w.shape[0], m), F32))
            out_specs.append(pl.BlockSpec((w.shape[0], tm), lambda i: (0, i)))
        else:
            out_shape.append(jax.ShapeDtypeStruct((m, w.shape[1]), F32))
            out_specs.append(pl.BlockSpec((tm, w.shape[1]), lambda i: (i, 0)))
    return pl.pallas_call(
        functools.partial(_norm_proj_kernel, nts=tuple(nts)),
        out_shape=out_shape, grid=(m // tm,), in_specs=in_specs, out_specs=out_specs,
        compiler_params=_cparams(("parallel",)), name="norm_proj",
    )(x2, g.reshape(1, k), *ws)


def _lru_kernel(xr_ref, gate_ref, hist_ref, h0_ref, cw_ref, cb_ref, wa_ref, ba_ref, wx_ref, bx_ref, lam_ref,
                rec_ref, conv_ref, hlast_ref, ext_ref, a_ref, b_ref, h_ref, hprev_ref, *, ns, tt, c, gw):
    t = pl.program_id(1)
    hk = CONV_W - 1

    @pl.when(t == 0)
    def _():
        ext_ref[:, SUBLANES - hk:SUBLANES, :] = hist_ref[...]
        hprev_ref[...] = h0_ref[...]

    ext_ref[:, SUBLANES:SUBLANES + tt, :] = xr_ref[...]
    xc = cb_ref[...]
    for k in range(CONV_W):
        xc = xc + ext_ref[:, SUBLANES - hk + k:SUBLANES - hk + k + tt, :] * cw_ref[k:k + 1, :]
    conv_ref[...] = ext_ref[:, SUBLANES + tt - hk:SUBLANES + tt, :]
    ext_ref[:, 0:SUBLANES, :] = ext_ref[:, tt:tt + SUBLANES, :]

    x2 = xc.reshape(ns * tt, c)
    ra, rx = [], []
    for j in range(c // gw):
        xg = x2[:, j * gw:(j + 1) * gw].astype(BF16)
        ra.append(_dot(xg, wa_ref[j]))
        rx.append(_dot(xg, wx_ref[j]))
    r = _sigmoid(jnp.concatenate(ra, axis=1) + ba_ref[...])
    ig = _sigmoid(jnp.concatenate(rx, axis=1) + bx_ref[...])
    log_a = (-LRU_C) * r * _softplus(-lam_ref[...])
    a = jnp.exp(log_a)
    b = jnp.sqrt(-jnp.tanh(log_a) * (a * a + 1.0)) * (ig * x2)

    a3 = a.reshape(ns * tt // SUBLANES, SUBLANES, c)
    b3 = b.reshape(ns * tt // SUBLANES, SUBLANES, c)
    row = lax.broadcasted_iota(jnp.int32, a3.shape, 1)
    d = 1
    while d < SUBLANES:
        a_sh = jnp.where(row >= d, pltpu.roll(a3, d, axis=1), 1.0)
        b_sh = jnp.where(row >= d, pltpu.roll(b3, d, axis=1), 0.0)
        b3 = a3 * b_sh + b3
        a3 = a3 * a_sh
        d *= 2
    a_ref[...] = a3.reshape(ns, tt, c)
    b_ref[...] = b3.reshape(ns, tt, c)

    def slab(j, hp):
        s = pl.multiple_of(j * SUBLANES, SUBLANES)
        h8 = a_ref[:, pl.ds(s, SUBLANES), :] * hp + b_ref[:, pl.ds(s, SUBLANES), :]
        h_ref[:, pl.ds(s, SUBLANES), :] = h8
        return h8[:, SUBLANES - 1:SUBLANES, :]

    hp = lax.fori_loop(0, tt // SUBLANES, slab, hprev_ref[...])
    hprev_ref[...] = hp
    hlast_ref[...] = hp
    rec_ref[...] = (h_ref[...] * _gelu_tanh(gate_ref[...])).astype(rec_ref.dtype)


def lru(xr3, gate3, hist, h0, cw, cb, wa_bd, ba, wx_bd, bx, lam, ns, tt, out_dtype):
    nseq, length, c = xr3.shape
    gw = wa_bd.shape[-1]
    hk = CONV_W - 1
    grid = (nseq // ns, length // tt)
    blk = pl.BlockSpec((ns, tt, c), lambda s, t: (s, t, 0))
    vec = _const_spec((1, c))
    return pl.pallas_call(
        functools.partial(_lru_kernel, ns=ns, tt=tt, c=c, gw=gw),
        out_shape=[jax.ShapeDtypeStruct((nseq, length, c), out_dtype),
                   jax.ShapeDtypeStruct((nseq, hk, c), F32),
                   jax.ShapeDtypeStruct((nseq, 1, c), F32)],
        grid=grid,
        in_specs=[blk, blk,
                  pl.BlockSpec((ns, hk, c), lambda s, t: (s, 0, 0)),
                  pl.BlockSpec((ns, 1, c), lambda s, t: (s, 0, 0)),
                  _const_spec((CONV_W, c)), vec,
                  _const_spec(wa_bd.shape), vec, _const_spec(wx_bd.shape), vec, vec],
        out_specs=[blk,
                   pl.BlockSpec((ns, hk, c), lambda s, t: (s, 0, 0)),
                   pl.BlockSpec((ns, 1, c), lambda s, t: (s, 0, 0))],
        scratch_shapes=[pltpu.VMEM((ns, SUBLANES + tt, c), F32),
                        pltpu.VMEM((ns, tt, c), F32), pltpu.VMEM((ns, tt, c), F32), pltpu.VMEM((ns, tt, c), F32),
                        pltpu.VMEM((ns, 1, c), F32)],
        compiler_params=_cparams(("arbitrary", "arbitrary")), name="lru",
    )(xr3, gate3, hist, h0, cw, cb.reshape(1, c), wa_bd, ba.reshape(1, c), wx_bd, bx.reshape(1, c), lam.reshape(1, c))


def _swa_prompt_kernel(sink_ref, q_ref, kc_ref, kp_ref, vtc_ref, vtp_ref, o_ref):
    i = pl.program_id(0)
    w, hd = WINDOW, SWA_HEAD_DIM
    grp = SWA_HEADS // SWA_KV_HEADS
    nq = grp * w
    key = lax.broadcasted_iota(jnp.int32, (2 * w, nq), 0)
    qpos = lax.broadcasted_iota(jnp.int32, (2 * w, nq), 1) % w
    keep = ((key < w) & (key >= qpos) & (i > 0)) | ((key >= w) & ((key - w) <= qpos))
    colh = lax.broadcasted_iota(jnp.int32, (1, nq), 1) // w
    scale = 1.0 / math.sqrt(hd)
    for kh in range(SWA_KV_HEADS):
        ls = slice(kh * hd, (kh + 1) * hd)
        kk = jnp.concatenate([kp_ref[:, ls], kc_ref[:, ls]], axis=0).astype(BF16)
        q4 = jnp.concatenate([q_ref[:, (kh * grp + g) * hd:(kh * grp + g + 1) * hd] for g in range(grp)], axis=0)
        st = jnp.where(keep, _dot_nt(kk, (q4 * scale).astype(BF16)), NEG)
        sk = jnp.zeros((1, nq), F32)
        for g in range(grp):
            sk = jnp.where(colh == g, sink_ref[kh * grp + g], sk)
        m = jnp.maximum(jnp.max(st, axis=0, keepdims=True), sk)
        p = jnp.exp(st - m)
        den = jnp.sum(p, axis=0, keepdims=True) + jnp.exp(sk - m)
        vt = jnp.concatenate([vtp_ref[ls, :], vtc_ref[ls, :]], axis=1).astype(BF16)
        ot = _dot(vt, p.astype(BF16)) / den
        for g in range(grp):
            h = kh * grp + g
            o_ref[h * hd:(h + 1) * hd, :] = ot[:, g * w:(g + 1) * w].astype(o_ref.dtype)


def swa_prompt(q, k, vt, sink):
    length, qw = q.shape
    w = WINDOW
    kvw = SWA_KV_HEADS * SWA_HEAD_DIM
    cur = lambda i: (i, 0)
    prev = lambda i: (jnp.maximum(i - 1, 0), 0)
    cur_t = lambda i: (0, i)
    prev_t = lambda i: (0, jnp.maximum(i - 1, 0))
    return pl.pallas_call(
        _swa_prompt_kernel,
        out_shape=jax.ShapeDtypeStruct((qw, length), BF16),
        grid=(length // w,),
        in_specs=[pl.BlockSpec(memory_space=pltpu.SMEM),
                  pl.BlockSpec((w, qw), cur),
                  pl.BlockSpec((w, kvw), cur), pl.BlockSpec((w, kvw), prev),
                  pl.BlockSpec((kvw, w), cur_t), pl.BlockSpec((kvw, w), prev_t)],
        out_specs=pl.BlockSpec((qw, w), cur_t),
        compiler_params=_cparams(("parallel",)), name="swa_prompt",
    )(sink, q, k, k, vt, vt)


def _swa_sample_kernel(sink_ref, q_ref, kn_ref, vn_ref, kb_ref, vb_ref, o_ref, ko_ref, vo_ref, kk_ref, vv_ref, *, bs, sl):
    w, hd = WINDOW, SWA_HEAD_DIM
    grp = SWA_HEADS // SWA_KV_HEADS
    nk = kk_ref.shape[0]
    scale = 1.0 / math.sqrt(hd)
    kk_ref[w + sl:nk, :] = jnp.zeros((nk - w - sl, kk_ref.shape[1]), F32)
    vv_ref[w + sl:nk, :] = jnp.zeros((nk - w - sl, vv_ref.shape[1]), F32)
    row = lax.broadcasted_iota(jnp.int32, (grp * sl, nk), 0)
    col = lax.broadcasted_iota(jnp.int32, (grp * sl, nk), 1)
    qi = row % sl
    mask = (col >= qi) & (col <= qi + w)
    rowc = lax.broadcasted_iota(jnp.int32, (grp * sl, 1), 0)

    def seq(b, carry):
        kb, kn = kb_ref[b], kn_ref[b]
        vb, vn = vb_ref[b], vn_ref[b]
        ko_ref[b, 0:w - sl, :] = kb_ref[b, sl:w, :]
        ko_ref[b, w - sl:w, :] = kn
        vo_ref[b, 0:w - sl, :] = vb_ref[b, sl:w, :]
        vo_ref[b, w - sl:w, :] = vn
        kk_ref[0:w, :] = kb
        kk_ref[w:w + sl, :] = kn
        vv_ref[0:w, :] = vb
        vv_ref[w:w + sl, :] = vn
        q = q_ref[b]
        for kh in range(SWA_KV_HEADS):
            ls = slice(kh * hd, (kh + 1) * hd)
            kkh = kk_ref[:, ls].astype(BF16)
            vvh = vv_ref[:, ls].astype(BF16)
            qs = jnp.concatenate([q[:, (kh * grp + g) * hd:(kh * grp + g + 1) * hd] for g in range(grp)], axis=0)
            s = jnp.where(mask, _dot_nt((qs * scale).astype(BF16), kkh), NEG)
            sk = jnp.zeros((grp * sl, 1), F32)
            for g in range(grp):
                sk = jnp.where(rowc // sl == g, sink_ref[kh * grp + g], sk)
            m = jnp.maximum(jnp.max(s, axis=1, keepdims=True), sk)
            p = jnp.exp(s - m)
            den = jnp.sum(p, axis=1, keepdims=True) + jnp.exp(sk - m)
            o = _dot(p.astype(BF16), vvh) / den
            for g in range(grp):
                h = kh * grp + g
                o_ref[b, :, h * hd:(h + 1) * hd] = o[g * sl:(g + 1) * sl, :]
        return carry

    lax.fori_loop(0, bs, seq, 0)


def swa_sample(q3, kn3, vn3, kbuf, vbuf, sink, bs=8):
    nseq, sl, qw = q3.shape
    w = WINDOW
    kvw = kn3.shape[-1]
    nk = 2 * w
    b3 = lambda shape: pl.BlockSpec(shape, lambda i: (i, 0, 0))
    return pl.pallas_call(
        functools.partial(_swa_sample_kernel, bs=bs, sl=sl),
        out_shape=[jax.ShapeDtypeStruct((nseq, sl, qw), F32),
                   jax.ShapeDtypeStruct((nseq, w, kvw), F32),
                   jax.ShapeDtypeStruct((nseq, w, kvw), F32)],
        grid=(nseq // bs,),
        in_specs=[pl.BlockSpec(memory_space=pltpu.SMEM),
                  b3((bs, sl, qw)), b3((bs, sl, kvw)), b3((bs, sl, kvw)), b3((bs, w, kvw)), b3((bs, w, kvw))],
        out_specs=[b3((bs, sl, qw)), b3((bs, w, kvw)), b3((bs, w, kvw))],
        scratch_shapes=[pltpu.VMEM((nk, kvw), F32), pltpu.VMEM((nk, kvw), F32)],
        compiler_params=_cparams(("arbitrary",)), name="swa_sample",
    )(sink, q3, kn3, vn3, kbuf, vbuf)


def _ffn_kernel(x_ref, a1_ref, w1_ref, a2_ref, w2_ref, g_ref, wg_ref, wu_ref, cw_ref, cb_ref, wd_ref, hist_ref, fg_ref,
                o_ref, hout_ref, ext_ref, act_ref, *, ns, tt, d, f, tf, final, a2_t):
    t = pl.program_id(1)
    hk = FFN_CONV - 1

    @pl.when(t == 0)
    def _():
        ext_ref[:, SUBLANES - hk:SUBLANES, :] = hist_ref[...]

    a1 = a1_ref[...].reshape(ns * tt, a1_ref.shape[-1]).astype(BF16)
    if a2_t:
        second = _dot_tn(a2_ref[...].astype(BF16), w2_ref[...])
    else:
        second = _dot(a2_ref[...].reshape(ns * tt, a2_ref.shape[-1]).astype(BF16), w2_ref[...])
    x = x_ref[...].reshape(ns * tt, d) + _dot(a1, w1_ref[...]) + second
    xn = _rms(x, g_ref[...]).astype(BF16)
    for c in range(f // tf):
        sl = slice(c * tf, (c + 1) * tf)
        g3 = _dot(xn, wg_ref[:, sl]).reshape(ns, tt, tf)
        ext_ref[:, SUBLANES:SUBLANES + tt, sl] = g3
        gc = cb_ref[:, sl] + g3 * cw_ref[hk:hk + 1, sl]
        for k in range(hk):
            gc = gc + ext_ref[:, SUBLANES - hk + k:SUBLANES - hk + k + tt, sl] * cw_ref[k:k + 1, sl]
        u = _dot(xn, wu_ref[:, sl])
        act_ref[:, sl] = (_silu(gc).reshape(ns * tt, tf) * u).astype(BF16)
    y = x + _dot(act_ref[...], wd_ref[...])
    if final:
        y = _rms(y, fg_ref[...])
    o_ref[...] = y.reshape(ns, tt, d)
    hout_ref[...] = ext_ref[:, SUBLANES + tt - hk:SUBLANES + tt, :]
    ext_ref[:, 0:SUBLANES, :] = ext_ref[:, tt:tt + SUBLANES, :]


def ffn(x3, a1, w1, a2, w2, a2_t, g, wg, wu, cw, cb, wd, hist, fg, ns, tt, final, tf=256):
    nseq, length, d = x3.shape
    f = wg.shape[1]
    hk = FFN_CONV - 1
    blk = lambda width: pl.BlockSpec((ns, tt, width), lambda s, t: (s, t, 0))
    hspec = pl.BlockSpec((ns, hk, f), lambda s, t: (s, 0, 0))
    if a2_t:
        assert nseq == 1
        a2_spec = pl.BlockSpec((a2.shape[0], tt), lambda s, t: (0, t))
    else:
        a2_spec = blk(a2.shape[-1])
    return pl.pallas_call(
        functools.partial(_ffn_kernel, ns=ns, tt=tt, d=d, f=f, tf=tf, final=final, a2_t=a2_t),
        out_shape=[jax.ShapeDtypeStruct((nseq, length, d), F32), jax.ShapeDtypeStruct((nseq, hk, f), F32)],
        grid=(nseq // ns, length // tt),
        in_specs=[blk(d), blk(a1.shape[-1]), _const_spec(w1.shape), a2_spec, _const_spec(w2.shape),
                  _const_spec((1, d)), _const_spec(wg.shape), _const_spec(wu.shape),
                  _const_spec((FFN_CONV, f)), _const_spec((1, f)), _const_spec(wd.shape), hspec, _const_spec((1, d))],
        out_specs=[blk(d), hspec],
        scratch_shapes=[pltpu.VMEM((ns, SUBLANES + tt, f), F32), pltpu.VMEM((ns * tt, f), BF16)],
        compiler_params=_cparams(("arbitrary", "arbitrary")), name="ffn",
    )(x3, a1, w1, a2, w2, g.reshape(1, d), wg, wu, cw, cb.reshape(1, f), wd, hist, fg.reshape(1, d))


def _ssd_kernel(z_ref, xbc_ref, dt_ref, dtt_ref, hist_ref, s0_ref, cw_ref, cb_ref, dtb_ref, dtbt_ref,
                alog_ref, alogt_ref, dvec_ref, nw_ref,
                y_ref, conv_ref, sout_ref, ext_ref, st_ref, ybuf_ref, dtp_ref, dttp_ref, *, tin, tc, inner, nst):
    c = pl.program_id(1)
    carried = tin == tc
    hk = CONV_W - 1
    nh, hp = SSD_HEADS, SSD_HEAD_DIM
    gh = nh // SSD_GROUPS

    @pl.when(c == 0)
    def _():
        if tin < tc:
            ext_ref[...] = jnp.zeros(ext_ref.shape, F32)
            dtp_ref[...] = jnp.zeros(dtp_ref.shape, F32)
            dttp_ref[...] = jnp.zeros(dttp_ref.shape, F32)
        ext_ref[SUBLANES - hk:SUBLANES, :] = hist_ref[0]
        if carried:
            for h in range(nh):
                st_ref[:, h * hp:(h + 1) * hp] = s0_ref[0, h].T

    ext_ref[SUBLANES:SUBLANES + tin, :] = xbc_ref[0]
    xc = cb_ref[...]
    for k in range(CONV_W):
        xc = xc + ext_ref[SUBLANES - hk + k:SUBLANES - hk + k + tc, :] * cw_ref[k:k + 1, :]
    conv_ref[0] = ext_ref[SUBLANES + tin - hk:SUBLANES + tin, :]
    if tin == tc:
        ext_ref[0:SUBLANES, :] = ext_ref[tc:tc + SUBLANES, :]
    xa = _silu(xc)
    xs = xa[:, :inner]
    bm = [xa[:, inner + g * nst:inner + (g + 1) * nst].astype(BF16) for g in range(SSD_GROUPS)]
    cm = [xa[:, inner + (SSD_GROUPS + g) * nst:inner + (SSD_GROUPS + g + 1) * nst].astype(BF16) for g in range(SSD_GROUPS)]

    if tin < tc:
        dtp_ref[0:tin, :] = dt_ref[0]
        dttp_ref[:, 0:tin] = dtt_ref[0]
        dt_raw, dtt_raw = dtp_ref[...], dttp_ref[...]
        valid_r = lax.broadcasted_iota(jnp.int32, (tc, nh), 0) < tin
        valid_c = lax.broadcasted_iota(jnp.int32, (nh, tc), 1) < tin
        dt = jnp.where(valid_r, _softplus(dt_raw + dtb_ref[...]), 0.0)
        dtt = jnp.where(valid_c, _softplus(dtt_raw + dtbt_ref[...]), 0.0)
    else:
        dt = _softplus(dt_ref[0] + dtb_ref[...])
        dtt = _softplus(dtt_ref[0] + dtbt_ref[...])
    da = dt * (-jnp.exp(alog_ref[...]))
    dat = dtt * (-jnp.exp(alogt_ref[...]))

    li = lax.broadcasted_iota(jnp.int32, (tc, tc), 0)
    si = lax.broadcasted_iota(jnp.int32, (tc, tc), 1)
    causal = li >= si
    tri = jnp.where(causal, 1.0, 0.0).astype(BF16)
    trit = jnp.where(li <= si, 1.0, 0.0).astype(BF16)
    cs = sum(_dot(tri, p) for p in _split3(da))
    cst = sum(_dot(p, trit) for p in _split3(dat))
    cb_g =[_dot_nt(cm[g], bm[g]) for g in range(SSD_GROUPS)]

    gwid = inner // SSD_GROUPS
    if carried:
        erow = lax.broadcasted_iota(jnp.int32, (nh, inner), 0)
        ehead = lax.broadcasted_iota(jnp.int32, (nh, inner), 1) // hp
        expand = jnp.where(erow == ehead, 1.0, 0.0).astype(BF16)
        dt_e = sum(_dot(p, expand) for p in _split3(dt))
        cs_e = sum(_dot(p, expand) for p in _split3(cs))
        last_e = cs_e[tc - 1:tc, :]
        xd = xs * dt_e
        xdb = xd.astype(BF16)
        xdd = (xd * jnp.exp(last_e - cs_e)).astype(BF16)
        y_off = []
        for g in range(SSD_GROUPS):
            gs = slice(g * gwid, (g + 1) * gwid)
            st_g = st_ref[:, gs]
            y_off.append(_dot(cm[g], st_g.astype(BF16)))
            st_ref[:, gs] = jnp.exp(last_e[:, gs]) * st_g + _dot_tn(bm[g], xdd[:, gs])
        for h in range(nh):
            ls = slice(h * hp, (h + 1) * hp)
            lmat = jnp.exp(jnp.where(causal, cs[:, h:h + 1] - cst[h:h + 1, :], NEG))
            ybuf_ref[:, ls] = _dot((cb_g[h // gh] * lmat).astype(BF16), xdb[:, ls])
        y_all = ybuf_ref[...] + jnp.exp(cs_e) * jnp.concatenate(y_off, axis=1) + dvec_ref[...] * xs

        @pl.when(c == pl.num_programs(1) - 1)
        def _():
            for h in range(nh):
                sout_ref[0, h] = st_ref[:, h * hp:(h + 1) * hp].T
    else:
        for h in range(nh):
            g = h // gh
            ls = slice(h * hp, (h + 1) * hp)
            col = cs[:, h:h + 1]
            last = cs[tc - 1:tc, h:h + 1]
            lmat = jnp.exp(jnp.where(causal, col - cst[h:h + 1, :], NEG))
            xs_h = xs[:, ls]
            xd = xs_h * dt[:, h:h + 1]
            s_h = s0_ref[0, h]
            yh = _dot((cb_g[g] * lmat).astype(BF16), xd.astype(BF16))
            yh = yh + jnp.exp(col) * _dot_nt(cm[g], s_h.astype(BF16))
            sout_ref[0, h] = jnp.exp(last) * s_h + _dot_tn((xd * jnp.exp(last - col)).astype(BF16), bm[g])
            ybuf_ref[:, ls] = yh + dvec_ref[:, ls] * xs_h
        y_all = ybuf_ref[...]

    y = y_all[0:tin, :] * _silu(z_ref[0])
    for g in range(SSD_GROUPS):
        gs = slice(g * gwid, (g + 1) * gwid)
        y_ref[0, :, gs] = _rms(y[:, gs], nw_ref[:, gs]).astype(y_ref.dtype)


def ssd(z3, xbc3, dt3, dtt3, hist, s0, cw, cb, dtb, alog, dvec, nw, tin, out_dtype):
    nseq, length, inner = z3.shape
    cd = xbc3.shape[-1]
    nh, hp, nst = s0.shape[1:]
    tc = SSD_CHUNK if tin == SSD_CHUNK else -(-tin // BF16_SUBLANES) * BF16_SUBLANES
    assert tin == SSD_CHUNK or tin == length
    hk = CONV_W - 1
    t3 = lambda width: pl.BlockSpec((1, tin, width), lambda s, c: (s, c, 0))
    per_seq = lambda shape: pl.BlockSpec((1,) + shape, lambda s, c: (s,) + (0,) * len(shape))
    return pl.pallas_call(
        functools.partial(_ssd_kernel, tin=tin, tc=tc, inner=inner, nst=nst),
        out_shape=[jax.ShapeDtypeStruct((nseq, length, inner), out_dtype),
                   jax.ShapeDtypeStruct((nseq, hk, cd), F32),
                   jax.ShapeDtypeStruct(s0.shape, F32)],
        grid=(nseq, length // tin),
        in_specs=[t3(inner), t3(cd), t3(nh),
                  pl.BlockSpec((1, nh, tin), lambda s, c: (s, 0, c)),
                  per_seq((hk, cd)), per_seq((nh, hp, nst)),
                  _const_spec((CONV_W, cd)), _const_spec((1, cd)), _const_spec((1, nh)), _const_spec((nh, 1)),
                  _const_spec((1, nh)), _const_spec((nh, 1)), _const_spec((1, inner)), _const_spec((1, inner))],
        out_specs=[t3(inner), per_seq((hk, cd)), per_seq((nh, hp, nst))],
        scratch_shapes=[pltpu.VMEM((SUBLANES + tc, cd), F32), pltpu.VMEM((nst, inner), F32),
                        pltpu.VMEM((tc, inner), F32), pltpu.VMEM((tc, nh), F32), pltpu.VMEM((nh, tc), F32)],
        compiler_params=_cparams(("arbitrary", "arbitrary")), name="ssd",
    )(z3, xbc3, dt3, dtt3, hist, s0, cw, cb.reshape(1, cd), dtb.reshape(1, nh), dtb.reshape(nh, 1),
      alog.reshape(1, nh), alog.reshape(nh, 1), dvec.reshape(1, inner), nw.reshape(1, inner))


def _rope128(t, ctab, stab):
    half = QK_ROPE // 2
    lane = lax.broadcasted_iota(jnp.int32, t.shape, 1)
    swapped = jnp.where(lane < QK_NOPE + half, pltpu.roll(t, LANES - half, axis=1), pltpu.roll(t, half, axis=1))
    return t * ctab + swapped * stab


def _mla_prep_kernel(cq_ref, ckv_ref, kr_ref, ct_ref, st_ref, qg_ref, kg_ref, wuq_ref, wk_ref, wv_ref,
                     lat_ref, rope_ref, *outs, sample):
    scale = 1.0 / math.sqrt(QK_NOPE + QK_ROPE)
    ctab, stab = ct_ref[...], st_ref[...]
    qf = _dot(_rms(cq_ref[...], qg_ref[...]).astype(BF16), wuq_ref[...])
    ckv = _rms(ckv_ref[...], kg_ref[...])
    lat_ref[...] = ckv
    krr = _rope128(kr_ref[...], ctab, stab)
    rope_ref[...] = krr[:, QK_NOPE:QK_NOPE + QK_ROPE]
    if sample:
        qlat_ref, qr_ref = outs
        for h in range(MLA_HEADS):
            qh = _rope128(qf[:, h * LANES:(h + 1) * LANES], ctab, stab) * scale
            qlat_ref[h] = _dot(qh[:, :QK_NOPE].astype(BF16), wk_ref[h])
            qr_ref[h] = qh[:, QK_NOPE:QK_NOPE + QK_ROPE]
    else:
        qc_ref, kc_ref, vt_ref = outs
        ckvb = ckv.astype(BF16)
        vrows = vt_ref.shape[1]
        ones_row = jnp.where(lax.broadcasted_iota(jnp.int32, (vrows, ckvb.shape[0]), 0) == V_HEAD, 1.0, 0.0)
        for h in range(MLA_HEADS):
            qh = _rope128(qf[:, h * LANES:(h + 1) * LANES], ctab, stab) * (scale * LOG2E)
            qc_ref[h] = qh.astype(BF16)
            kc_ref[h] = (_dot(ckvb, wk_ref[h]) + krr).astype(BF16)
            vt_ref[h] = (_dot_nt(wv_ref[h], ckvb) + ones_row).astype(BF16)


def mla_prep(cq, ckv_raw, krp, ctab, stab, qg, kg, wuq_pad, wk, wv, sample, tm=512):
    m = cq.shape[0]
    tm = min(tm, m)
    nh = MLA_HEADS
    row = lambda width: pl.BlockSpec((tm, width), lambda i: (i, 0))
    hrow = lambda width: pl.BlockSpec((nh, tm, width), lambda i: (0, i, 0))
    lq, lkv = cq.shape[1], ckv_raw.shape[1]
    out_shape = [jax.ShapeDtypeStruct((m, lkv), F32), jax.ShapeDtypeStruct((m, QK_ROPE), F32)]
    out_specs = [row(lkv), row(QK_ROPE)]
    if sample:
        out_shape += [jax.ShapeDtypeStruct((nh, m, lkv), F32), jax.ShapeDtypeStruct((nh, m, QK_ROPE), F32)]
        out_specs += [hrow(lkv), hrow(QK_ROPE)]
    else:
        out_shape += [jax.ShapeDtypeStruct((nh, m, LANES), BF16)] * 2 + [jax.ShapeDtypeStruct((nh, VT_ROWS, m), BF16)]
        out_specs += [hrow(LANES)] * 2 + [pl.BlockSpec((nh, VT_ROWS, tm), lambda i: (0, 0, i))]
    return pl.pallas_call(
        functools.partial(_mla_prep_kernel, sample=sample),
        out_shape=out_shape, grid=(m // tm,),
        in_specs=[row(lq), row(lkv), row(LANES), row(LANES), row(LANES), _const_spec((1, lq)), _const_spec((1, lkv)),
                  _const_spec(wuq_pad.shape), _const_spec(wk.shape), _const_spec(wv.shape)],
        out_specs=out_specs,
        compiler_params=_cparams(("parallel",)), name="mla_prep",
    )(cq, ckv_raw, krp, ctab, stab, qg.reshape(1, lq), kg.reshape(1, lkv), wuq_pad, wk, wv)


def _mla_flash_kernel(qi_ref, kj_ref, q_ref, k_ref, vt_ref, o_ref, m_ref, acc_ref, *, t, nq):
    n = pl.program_id(0)
    i, j = qi_ref[n], kj_ref[n]

    @pl.when(j == 0)
    def _():
        m_ref[...] = jnp.full(m_ref.shape, NEG, F32)
        acc_ref[...] = jnp.zeros(acc_ref.shape, F32)

    def tile(u, masked):
        qs = slice(u * t, (u + 1) * t)
        if masked:
            keep = lax.broadcasted_iota(jnp.int32, (t, t), 0) <= lax.broadcasted_iota(jnp.int32, (t, t), 1)
        nxt = _dot_nt(k_ref[0], q_ref[0, qs, :])
        pending = None
        for h in range(MLA_HEADS + 1):
            if h < MLA_HEADS:
                st = nxt
                if h + 1 < MLA_HEADS:
                    nxt = _dot_nt(k_ref[h + 1], q_ref[h + 1, qs, :])
                if masked:
                    st = jnp.where(keep, st, NEG)
                m_prev = m_ref[h, :, qs]
                m_new = jnp.maximum(m_prev, jnp.max(st, axis=0, keepdims=True))
                alpha = jnp.exp2(m_prev - m_new)
                p = jnp.exp2(st - m_new).astype(BF16)
                m_ref[h, :, qs] = m_new
            if pending is not None:
                hp, alpha_p, p_p = pending
                acc_ref[hp, :, qs] = alpha_p * acc_ref[hp, :, qs] + _dot(vt_ref[hp], p_p)
            pending = (h, alpha, p) if h < MLA_HEADS else None

    for u in range(nq):
        qt = nq * i + u

        @pl.when(j < qt)
        def _():
            tile(u, False)

        @pl.when(j == qt)
        def _():
            tile(u, True)
            for h in range(MLA_HEADS):
                a = acc_ref[h, :, u * t:(u + 1) * t]
                o_ref[h * V_HEAD:(h + 1) * V_HEAD, u * t:(u + 1) * t] = (
                    a[0:V_HEAD, :] / a[V_HEAD:V_HEAD + 1, :]).astype(o_ref.dtype)


def mla_flash(qc, kc, vt, t=512, nq=2):
    nh, length, _ = qc.shape
    t = min(t, length)
    nq = min(nq, length // t)
    nb = length // (t * nq)
    qi = [i for i in range(nb) for _ in range(nq * i + nq)]
    kj = [j for i in range(nb) for j in range(nq * i + nq)]
    grid_spec = pltpu.PrefetchScalarGridSpec(
        num_scalar_prefetch=2, grid=(len(qi),),
        in_specs=[pl.BlockSpec((nh, nq * t, LANES), lambda n, qi, kj: (0, qi[n], 0)),
                  pl.BlockSpec((nh, t, LANES), lambda n, qi, kj: (0, kj[n], 0)),
                  pl.BlockSpec((nh, VT_ROWS, t), lambda n, qi, kj: (0, 0, kj[n]))],
        out_specs=pl.BlockSpec((nh * V_HEAD, nq * t), lambda n, qi, kj: (0, qi[n])),
        scratch_shapes=[pltpu.VMEM((nh, 1, nq * t), F32), pltpu.VMEM((nh, VT_ROWS, nq * t), F32)])
    return pl.pallas_call(
        functools.partial(_mla_flash_kernel, t=t, nq=nq),
        out_shape=jax.ShapeDtypeStruct((nh * V_HEAD, length), BF16),
        grid_spec=grid_spec,
        compiler_params=_cparams(("arbitrary",)), name="mla_flash",
    )(jnp.asarray(qi, jnp.int32), jnp.asarray(kj, jnp.int32), qc, kc, vt)


def _mla_sample_kernel(pt_ref, qlat_ref, qr_ref, ckv_ref, kr_ref, wuv_ref, lat_hbm, rope_hbm,
                       o_ref, latbuf, ropebuf, sem, kbuf_ref, sbuf_ref, kl_ref, krl_ref,
                       *, cp, sl, npages, li):
    b, nb = pl.program_id(0), pl.num_programs(0)
    rows = MLA_HEADS * sl
    psz = latbuf.shape[2]
    pair = 2 * psz
    nch = npages // cp

    def chunk_copies(seq, ch, slot):
        out = []
        for c in range(cp):
            pid = pt_ref[seq, ch * cp + c]
            out.append(pltpu.make_async_copy(lat_hbm.at[li, pid], latbuf.at[slot, c], sem.at[0, slot]))
            out.append(pltpu.make_async_copy(rope_hbm.at[li, pid], ropebuf.at[slot, c], sem.at[1, slot]))
        return out

    def start(seq, ch, slot):
        for cpy in chunk_copies(seq, ch, slot):
            cpy.start()

    def wait(seq, ch, slot):
        for cpy in chunk_copies(seq, ch, slot):
            cpy.wait()

    nslots = latbuf.shape[0]
    ahead = nslots - 1

    @pl.when(b == 0)
    def _():
        for ch in range(ahead):
            start(0, ch, ch % nslots)

    q = qlat_ref[...].reshape(rows, qlat_ref.shape[-1]).astype(BF16)
    qr = qr_ref[...].reshape(rows, qr_ref.shape[-1]).astype(BF16)

    mrun = None
    for ch in range(nch):
        slot = ch % nslots
        wait(b, ch, slot)
        nxt = ch + ahead
        if nxt < nch:
            start(b, nxt, nxt % nslots)
        else:
            @pl.when(b + 1 < nb)
            def _():
                start(b + 1, nxt - nch, nxt % nslots)
        for c2 in range(cp // 2):
            base = (ch * cp + 2 * c2) * psz
            kk = jnp.concatenate([latbuf[slot, 2 * c2], latbuf[slot, 2 * c2 + 1]], axis=0).astype(BF16)
            kbuf_ref[base:base + pair, :] = kk
            rt = jnp.concatenate([ropebuf[slot, 2 * c2], ropebuf[slot, 2 * c2 + 1]], axis=1).astype(BF16)
            s = _dot_nt(q, kk) + _dot(qr, rt)
            sbuf_ref[:, base:base + pair] = s
            mrun = s if mrun is None else jnp.maximum(mrun, s)

    kl_ref[...] = jnp.zeros(kl_ref.shape, F32)
    krl_ref[...] = jnp.zeros(krl_ref.shape, F32)
    kl_ref[0:sl, :] = ckv_ref[0]
    krl_ref[0:sl, :] = kr_ref[0]
    kl = kl_ref[...].astype(BF16)
    kbuf_ref[npages * psz:npages * psz + pair, :] = kl
    s = _dot_nt(q, kl) + _dot_nt(qr, krl_ref[...].astype(BF16))
    row = lax.broadcasted_iota(jnp.int32, s.shape, 0)
    col = lax.broadcasted_iota(jnp.int32, s.shape, 1)
    s = jnp.where((col < sl) & (col <= row % sl), s, NEG)
    sbuf_ref[:, npages * psz:npages * psz + pair] = s

    m = jnp.broadcast_to(jnp.max(jnp.maximum(mrun, s), axis=1, keepdims=True), s.shape)
    nblk = sbuf_ref.shape[1] // pair
    hw = kbuf_ref.shape[1] // 2
    lacc = jnp.zeros(s.shape, F32)
    accs = [jnp.zeros((rows, hw), F32), jnp.zeros((rows, hw), F32)]
    for blk in range(nblk):
        cols = slice(blk * pair, (blk + 1) * pair)
        p = jnp.exp(sbuf_ref[:, cols] - m)
        lacc = lacc + p
        pb = p.astype(BF16)
        accs[0] = accs[0] + _dot(pb, kbuf_ref[cols, 0:hw])
        accs[1] = accs[1] + _dot(pb, kbuf_ref[cols, hw:2 * hw])
    lsum = jnp.sum(lacc, axis=1, keepdims=True)
    accn = (jnp.concatenate(accs, axis=1) / lsum).astype(BF16)
    for h in range(MLA_HEADS):
        o_ref[0, :, h * V_HEAD:(h + 1) * V_HEAD] = _dot(accn[h * sl:(h + 1) * sl, :], wuv_ref[h])


def mla_sample(qlat, qr, ckv3, kr3, wuv, cache_lat, cache_rope_t, page_table, li, cp=16):
    nh, m, lkv = qlat.shape
    nseq, sl, rd = kr3.shape
    npages = page_table.shape[1]
    nslots = MLA_PAGE_SLOTS
    cp = min(cp, npages // nslots)
    assert npages % (nslots * cp) == 0 and cp % 2 == 0
    psz = cache_lat.shape[2]
    nkeys = npages * psz + 2 * psz
    rows = nh * sl
    in_specs = [pl.BlockSpec((nh, sl, lkv), lambda b, pt: (0, b, 0)),
                pl.BlockSpec((nh, sl, rd), lambda b, pt: (0, b, 0)),
                pl.BlockSpec((1, sl, lkv), lambda b, pt: (b, 0, 0)),
                pl.BlockSpec((1, sl, rd), lambda b, pt: (b, 0, 0)),
                pl.BlockSpec(wuv.shape, lambda b, pt: (0, 0, 0)),
                pl.BlockSpec(memory_space=pl.ANY), pl.BlockSpec(memory_space=pl.ANY)]
    grid_spec = pltpu.PrefetchScalarGridSpec(
        num_scalar_prefetch=1, grid=(nseq,), in_specs=in_specs,
        out_specs=pl.BlockSpec((1, sl, nh * V_HEAD), lambda b, pt: (b, 0, 0)),
        scratch_shapes=[pltpu.VMEM((nslots, cp, psz, lkv), F32), pltpu.VMEM((nslots, cp, rd, psz), F32),
                        pltpu.SemaphoreType.DMA((2, nslots)),
                        pltpu.VMEM((nkeys, lkv), BF16), pltpu.VMEM((rows, nkeys), F32),
                        pltpu.VMEM((2 * psz, lkv), F32), pltpu.VMEM((2 * psz, rd), F32)])
    return pl.pallas_call(
        functools.partial(_mla_sample_kernel, cp=cp, sl=sl, npages=npages, li=li),
        out_shape=jax.ShapeDtypeStruct((nseq, sl, nh * V_HEAD), F32),
        grid_spec=grid_spec,
        compiler_params=_cparams(("arbitrary",)), name="mla_sample",
    )(page_table, qlat, qr, ckv3, kr3, wuv, cache_lat, cache_rope_t)


def _block_diag(w, per):
    nh, b, _ = w.shape
    w4 = w.reshape(nh // per, per, b, b)
    eye = jnp.eye(per, dtype=w.dtype)
    return jnp.einsum("gaij,ab->gaibj", w4, eye).reshape(nh // per, per * b, per * b)


def _rope_tables(pos):
    half = QK_ROPE // 2
    inv = jnp.exp(-(math.log(ROPE_THETA) / half) * jnp.arange(half, dtype=F32))
    ang = pos.astype(F32)[:, None] * inv[None, :]
    cos, sin = jnp.cos(ang), jnp.sin(ang)
    n = pos.shape[0]
    pad = LANES - QK_NOPE - QK_ROPE
    ctab = jnp.concatenate([jnp.ones((n, QK_NOPE), F32), cos, cos, jnp.ones((n, pad), F32)], axis=1)
    stab = jnp.concatenate([jnp.zeros((n, QK_NOPE), F32), -sin, sin, jnp.zeros((n, pad), F32)], axis=1)
    return ctab, stab


def _prep_weights(P):
    w = {}
    lw = P["lru_conv_w"].shape[-1]
    qw = SWA_HEADS * SWA_HEAD_DIM
    kvw = SWA_KV_HEADS * SWA_HEAD_DIM
    ew = P["even_w_in"][0].astype(BF16)
    b = [0, lw, 2 * lw, 2 * lw + qw, 2 * lw + qw + kvw, 2 * lw + qw + 2 * kvw]
    w["even_in"] = [ew[:, b[i]:b[i + 1]] for i in range(5)]
    per = 256 // (lw // LRU_HEADS)
    w["wa_bd"] = _block_diag(P["lru_wa"][0], per).astype(BF16)
    w["wx_bd"] = _block_diag(P["lru_wx"][0], per).astype(BF16)
    eo = P["even_w_out"][0].astype(BF16)
    w["even_out"] = (eo[:lw], eo[lw:])

    inner = P["ssd_norm"].shape[-1]
    cd = P["ssd_conv_w"].shape[-1]
    lq = P["mla_q_norm"].shape[-1]
    lkv = P["mla_kv_norm"].shape[-1]
    ow = P["odd_w_in"][0]
    b = [0, inner, inner + cd, inner + cd + SSD_HEADS, inner + cd + SSD_HEADS + lq,
         inner + cd + SSD_HEADS + lq + lkv, inner + cd + SSD_HEADS + lq + lkv + QK_ROPE]
    parts = [ow[:, b[i]:b[i + 1]] for i in range(6)]
    pad = LANES - QK_NOPE - QK_ROPE
    kr_pad = jnp.pad(parts[5], ((0, 0), (QK_NOPE, pad)))
    w["odd_in"] = [parts[0].astype(BF16), parts[1].astype(BF16), parts[2].astype(BF16), parts[2].T.astype(BF16),
                   parts[3].astype(BF16), parts[4].astype(BF16), kr_pad.astype(BF16)]
    uq = P["mla_w_uq"][0].reshape(lq, MLA_HEADS, QK_NOPE + QK_ROPE)
    w["wuq_pad"] = jnp.pad(uq, ((0, 0), (0, 0), (0, pad))).reshape(lq, MLA_HEADS * LANES).astype(BF16)
    uk = P["mla_w_uk"][0]
    uv = P["mla_w_uv"][0]
    w["wuk_pad"] = jnp.pad(jnp.transpose(uk, (1, 0, 2)), ((0, 0), (0, 0), (0, LANES - QK_NOPE))).astype(BF16)
    w["wuk_t"] = jnp.transpose(uk, (1, 2, 0)).astype(BF16)
    w["wuv_t_pad"] = jnp.pad(jnp.transpose(uv, (1, 2, 0)), ((0, 0), (0, VT_ROWS - V_HEAD), (0, 0))).astype(BF16)
    w["wuv"] = jnp.transpose(uv, (1, 0, 2)).astype(BF16)
    oo = P["odd_w_out"][0].astype(BF16)
    w["odd_out"] = (oo[:inner], oo[inner:])
    w["dvec"] = jnp.repeat(P["ssd_d"][0], SSD_HEAD_DIM)
    w["ffn"] = [(P["ffn_w_gate"][l].astype(BF16), P["ffn_w_up"][l].astype(BF16), P["ffn_w_down"][l].astype(BF16))
                for l in range(P["ffn_w_gate"].shape[0])]
    return w


def _trunk(x3, pos, P, W, st, sample):
    nseq, length, d = x3.shape
    m = nseq * length
    lw = P["lru_conv_w"].shape[-1]
    kvw = SWA_KV_HEADS * SWA_HEAD_DIM
    if sample:
        seq_tile, time_tile = min(nseq, 64), length
    else:
        seq_tile, time_tile = 1, min(length, 512)
    mix_dtype = F32 if sample else BF16
    out = {}

    if sample:
        xr, gate, q, k, v = norm_proj(x3.reshape(m, d), P["mix_norm"][0], W["even_in"], [False] * 5)
    else:
        xr, gate, q, k, v, vt = norm_proj(x3.reshape(m, d), P["mix_norm"][0], W["even_in"] + [W["even_in"][4].T],
                                          [False] * 5 + [True])
    lru_ns, lru_tt = (min(nseq, 32), length) if sample else (1, min(length, 256))
    rec, out["lru_conv"], h_last = lru(
        xr.reshape(nseq, length, lw), gate.reshape(nseq, length, lw), st["lru_conv"], st["lru_h"].reshape(nseq, 1, lw),
        P["lru_conv_w"][0], P["lru_conv_b"][0], W["wa_bd"], P["lru_ba"][0], W["wx_bd"], P["lru_bx"][0],
        P["lru_lambda"][0], lru_ns, lru_tt, mix_dtype)
    out["lru_h"] = h_last.reshape(nseq, lw)
    if sample:
        att, sk, sv = swa_sample(q.reshape(nseq, length, -1), k.reshape(nseq, length, kvw), v.reshape(nseq, length, kvw),
                                 st["swa_k"].reshape(nseq, WINDOW, kvw), st["swa_v"].reshape(nseq, WINDOW, kvw),
                                 P["swa_sink"][0], bs=min(nseq, 8))
        out["swa_k"] = sk.reshape(nseq, WINDOW, SWA_KV_HEADS, SWA_HEAD_DIM)
        out["swa_v"] = sv.reshape(nseq, WINDOW, SWA_KV_HEADS, SWA_HEAD_DIM)
    else:
        att = swa_prompt(q, k, vt, P["swa_sink"][0])
        out["swa_k"] = k[-WINDOW:].reshape(1, WINDOW, SWA_KV_HEADS, SWA_HEAD_DIM)
        out["swa_v"] = v[-WINDOW:].reshape(1, WINDOW, SWA_KV_HEADS, SWA_HEAD_DIM)
    wg, wu, wd = W["ffn"][0]
    x3, fc0 = ffn(x3, rec, W["even_out"][0], att, W["even_out"][1], not sample,
                  P["ffn_norm"][0], wg, wu, P["ffn_conv_w"][0], P["ffn_conv_b"][0], wd,
                  st["ffn_conv"][0], P["final_norm"], seq_tile, time_tile, final=False)

    z, xbc, dt, dtt, cq, ckv_raw, krp = norm_proj(x3.reshape(m, d), P["mix_norm"][1], W["odd_in"],
                                                   [False, False, False, True, False, False, False])
    inner = z.shape[-1]
    cd = xbc.shape[-1]
    dtt3 = jnp.transpose(dtt.reshape(SSD_HEADS, nseq, length), (1, 0, 2))
    y, out["ssd_conv"], out["ssd"] = ssd(
        z.reshape(nseq, length, inner), xbc.reshape(nseq, length, cd), dt.reshape(nseq, length, SSD_HEADS), dtt3,
        st["ssd_conv"], st["ssd"], P["ssd_conv_w"][0], P["ssd_conv_b"][0], P["ssd_dt_bias"][0], P["ssd_a_log"][0],
        W["dvec"], P["ssd_norm"][0], tin=min(length, SSD_CHUNK), out_dtype=mix_dtype)
    ctab, stab = _rope_tables(pos)
    if sample:
        lat, rp, qlat, qr = mla_prep(cq, ckv_raw, krp, ctab, stab, P["mla_q_norm"][0], P["mla_kv_norm"][0],
                                     W["wuq_pad"], W["wuk_t"], W["wuv"], sample=True)
        att = mla_sample(qlat, qr, lat.reshape(nseq, length, -1), rp.reshape(nseq, length, -1), W["wuv"],
                         st["mla_latent"], jnp.swapaxes(st["mla_rope"], 2, 3), st["page_table"], 0)
    else:
        lat, rp, qc, kc, vt = mla_prep(cq, ckv_raw, krp, ctab, stab, P["mla_q_norm"][0], P["mla_kv_norm"][0],
                                       W["wuq_pad"], W["wuk_pad"], W["wuv_t_pad"], sample=False)
        att = mla_flash(qc, kc, vt)
    out["mla_latent"] = lat.reshape(nseq, length, -1)
    out["mla_rope"] = rp.reshape(nseq, length, -1)
    wg, wu, wd = W["ffn"][1]
    y3, fc1 = ffn(x3, y, W["odd_out"][0], att, W["odd_out"][1], not sample,
                  P["ffn_norm"][1], wg, wu, P["ffn_conv_w"][1], P["ffn_conv_b"][1], wd,
                  st["ffn_conv"][1], P["final_norm"], seq_tile, time_tile, final=True)
    out["ffn_conv"] = jnp.stack([fc0, fc1])
    return y3, out


def kernel(x_prompt, x_sample, state_lru_conv, state_lru_h, cache_swa_k, cache_swa_v, state_ssd_conv, state_ssd,
           cache_mla_latent, cache_mla_rope, page_table, state_ffn_conv, mix_norm, ffn_norm, final_norm, even_w_in,
           lru_conv_w, lru_conv_b, lru_wa, lru_ba, lru_wx, lru_bx, lru_lambda, swa_sink, even_w_out, odd_w_in,
           ssd_conv_w, ssd_conv_b, ssd_dt_bias, ssd_a_log, ssd_d, ssd_norm, mla_q_norm, mla_w_uq, mla_kv_norm,
           mla_w_uk, mla_w_uv, odd_w_out, ffn_w_gate, ffn_w_up, ffn_conv_w, ffn_conv_b, ffn_w_down):
    P = dict(mix_norm=mix_norm, ffn_norm=ffn_norm, final_norm=final_norm, even_w_in=even_w_in,
             lru_conv_w=lru_conv_w, lru_conv_b=lru_conv_b, lru_wa=lru_wa, lru_ba=lru_ba, lru_wx=lru_wx,
             lru_bx=lru_bx, lru_lambda=lru_lambda, swa_sink=swa_sink, even_w_out=even_w_out,
             odd_w_in=odd_w_in, ssd_conv_w=ssd_conv_w, ssd_conv_b=ssd_conv_b, ssd_dt_bias=ssd_dt_bias,
             ssd_a_log=ssd_a_log, ssd_d=ssd_d, ssd_norm=ssd_norm, mla_q_norm=mla_q_norm,
             mla_w_uq=mla_w_uq, mla_kv_norm=mla_kv_norm, mla_w_uk=mla_w_uk, mla_w_uv=mla_w_uv,
             odd_w_out=odd_w_out, ffn_w_gate=ffn_w_gate, ffn_w_up=ffn_w_up, ffn_conv_w=ffn_conv_w,
             ffn_conv_b=ffn_conv_b, ffn_w_down=ffn_w_down)
    W = _prep_weights(P)
    bp, lp, d = x_prompt.shape
    bs, ls, _ = x_sample.shape
    depth = ffn_w_gate.shape[0]
    lw = lru_conv_w.shape[-1]
    cd = ssd_conv_w.shape[-1]
    f = ffn_w_gate.shape[-1]
    kvw = SWA_KV_HEADS * SWA_HEAD_DIM
    past_len = page_table.shape[1] * PAGE_SIZE

    st_p = dict(lru_conv=jnp.zeros((bp, CONV_W - 1, lw), F32), lru_h=jnp.zeros((bp, lw), F32),
                ssd_conv=jnp.zeros((bp, CONV_W - 1, cd), F32),
                ssd=jnp.zeros((bp, SSD_HEADS, SSD_HEAD_DIM, SSD_STATE), F32),
                ffn_conv=jnp.zeros((depth, bp, FFN_CONV - 1, f), F32))
    st_s = dict(lru_conv=state_lru_conv[0], lru_h=state_lru_h[0], swa_k=cache_swa_k[0], swa_v=cache_swa_v[0],
                ssd_conv=state_ssd_conv[0], ssd=state_ssd[0], mla_latent=cache_mla_latent, mla_rope=cache_mla_rope,
                page_table=page_table, ffn_conv=state_ffn_conv)
    pos_p = jnp.tile(jnp.arange(lp), bp)
    pos_s = jnp.tile(past_len + jnp.arange(ls), bs)
    y_p, sp = _trunk(x_prompt, pos_p, P, W, st_p, False)
    y_s, ss = _trunk(x_sample, pos_s, P, W, st_s, True)
    e = lambda a: a[None]
    return (y_p, y_s,
            e(sp["lru_conv"]), e(ss["lru_conv"]), e(sp["lru_h"]), e(ss["lru_h"]),
            e(sp["swa_k"]), e(ss["swa_k"]), e(sp["swa_v"]), e(ss["swa_v"]),
            e(sp["ssd_conv"]), e(ss["ssd_conv"]), e(sp["ssd"]), e(ss["ssd"]),
            e(sp["mla_latent"]), e(ss["mla_latent"]), e(sp["mla_rope"]), e(ss["mla_rope"]),
            sp["ffn_conv"], ss["ffn_conv"])
```

```python
import functools
import math

import jax
import jax.numpy as jnp
from jax import lax
from jax.experimental import pallas as pl
from jax.experimental.pallas import tpu as pltpu

F32 = jnp.float32
BF16 = jnp.bfloat16

EPS = 1e-6
LRU_C = 8.0
LRU_HEADS = 16
CONV_W = 4
SWA_HEADS = 8
SWA_KV_HEADS = 2
SWA_HEAD_DIM = 64
WINDOW = 128
SSD_HEADS = 16
SSD_HEAD_DIM = 64
SSD_GROUPS = 2
SSD_STATE = 128
SSD_CHUNK = 128
MLA_HEADS = 8
QK_NOPE = 64
QK_ROPE = 32
V_HEAD = 64
ROPE_THETA = 10000.0
PAGE_SIZE = 128
FFN_CONV = 3
NEG = -1e30
LOG2E = math.log2(math.e)
VT_ROWS = 80
MLA_PAGE_SLOTS = 4

VMEM_LIMIT = 56 * 1024 * 1024
SUBLANES = 8
BF16_SUBLANES = 16
LANES = 128


def _cparams(sem):
    return pltpu.CompilerParams(dimension_semantics=sem, vmem_limit_bytes=VMEM_LIMIT)


def _dot(a, b):
    return jnp.dot(a, b, preferred_element_type=F32)


def _dot_nt(a, b):
    return lax.dot_general(a, b, (((1,), (1,)), ((), ())), preferred_element_type=F32)


def _dot_tn(a, b):
    return lax.dot_general(a, b, (((0,), (0,)), ((), ())), preferred_element_type=F32)


def _sigmoid(x):
    return 1.0 / (1.0 + jnp.exp(-x))


def _silu(x):
    return x * _sigmoid(x)


def _softplus(x):
    return jnp.maximum(x, 0.0) + jnp.log1p(jnp.exp(-jnp.abs(x)))


def _gelu_tanh(x):
    return 0.5 * x * (1.0 + jnp.tanh(math.sqrt(2.0 / math.pi) * (x + 0.044715 * (x * x * x))))


def _rms(x, g):
    return x * lax.rsqrt(jnp.mean(x * x, axis=-1, keepdims=True) + EPS) * g


def _split3(x):
    x1 = x.astype(BF16)
    r1 = x - x1.astype(F32)
    x2 = r1.astype(BF16)
    x3 = (r1 - x2.astype(F32)).astype(BF16)
    return x1, x2, x3


def _const_spec(shape):
    nd = len(shape)
    return pl.BlockSpec(shape, lambda *_: (0,) * nd, pipeline_mode=pl.Buffered(1))


def _norm_proj_kernel(x_ref, g_ref, *refs, nts):
    n = len(nts)
    w_refs, o_refs = refs[:n], refs[n:]
    xn = _rms(x_ref[...], g_ref[...]).astype(BF16)
    for w_ref, o_ref, nt in zip(w_refs, o_refs, nts):
        if nt:
            o_ref[...] = _dot_nt(w_ref[...], xn)
        else:
            o_ref[...] = _dot(xn, w_ref[...])


def norm_proj(x2, g, ws, nts, tm=512):
    m, k = x2.shape
    tm = min(tm, m)
    in_specs = [pl.BlockSpec((tm, k), lambda i: (i, 0)), _const_spec((1, k))]
    out_shape, out_specs = [], []
    for w, nt in zip(ws, nts):
        in_specs.append(_const_spec(w.shape))
        if nt:
            out_shape.append(jax.ShapeDtypeStruct((w.shape[0], m), F32))
            out_specs.append(pl.BlockSpec((w.shape[0], tm), lambda i: (0, i)))
        else:
            out_shape.append(jax.ShapeDtypeStruct((m, w.shape[1]), F32))
            out_specs.append(pl.BlockSpec((tm, w.shape[1]), lambda i: (i, 0)))
    return pl.pallas_call(
        functools.partial(_norm_proj_kernel, nts=tuple(nts)),
        out_shape=out_shape, grid=(m // tm,), in_specs=in_specs, out_specs=out_specs,
        compiler_params=_cparams(("parallel",)), name="norm_proj",
    )(x2, g.reshape(1, k), *ws)


def _lru_kernel(xr_ref, gate_ref, hist_ref, h0_ref, cw_ref, cb_ref, wa_ref, ba_ref, wx_ref, bx_ref, lam_ref,
                rec_ref, conv_ref, hlast_ref, ext_ref, a_ref, b_ref, h_ref, hprev_ref, *, ns, tt, c, gw):
    t = pl.program_id(1)
    hk = CONV_W - 1

    @pl.when(t == 0)
    def _():
        ext_ref[:, SUBLANES - hk:SUBLANES, :] = hist_ref[...]
        hprev_ref[...] = h0_ref[...]

    ext_ref[:, SUBLANES:SUBLANES + tt, :] = xr_ref[...]
    xc = cb_ref[...]
    for k in range(CONV_W):
        xc = xc + ext_ref[:, SUBLANES - hk + k:SUBLANES - hk + k + tt, :] * cw_ref[k:k + 1, :]
    conv_ref[...] = ext_ref[:, SUBLANES + tt - hk:SUBLANES + tt, :]
    ext_ref[:, 0:SUBLANES, :] = ext_ref[:, tt:tt + SUBLANES, :]

    x2 = xc.reshape(ns * tt, c)
    ra, rx = [], []
    for j in range(c // gw):
        xg = x2[:, j * gw:(j + 1) * gw].astype(BF16)
        ra.append(_dot(xg, wa_ref[j]))
        rx.append(_dot(xg, wx_ref[j]))
    r = _sigmoid(jnp.concatenate(ra, axis=1) + ba_ref[...])
    ig = _sigmoid(jnp.concatenate(rx, axis=1) + bx_ref[...])
    log_a = (-LRU_C) * r * _softplus(-lam_ref[...])
    a = jnp.exp(log_a)
    b = jnp.sqrt(-jnp.tanh(log_a) * (a * a + 1.0)) * (ig * x2)

    a3 = a.reshape(ns * tt // SUBLANES, SUBLANES, c)
    b3 = b.reshape(ns * tt // SUBLANES, SUBLANES, c)
    row = lax.broadcasted_iota(jnp.int32, a3.shape, 1)
    d = 1
    while d < SUBLANES:
        a_sh = jnp.where(row >= d, pltpu.roll(a3, d, axis=1), 1.0)
        b_sh = jnp.where(row >= d, pltpu.roll(b3, d, axis=1), 0.0)
        b3 = a3 * b_sh + b3
        a3 = a3 * a_sh
        d *= 2
    a_ref[...] = a3.reshape(ns, tt, c)
    b_ref[...] = b3.reshape(ns, tt, c)

    def slab(j, hp):
        s = pl.multiple_of(j * SUBLANES, SUBLANES)
        h8 = a_ref[:, pl.ds(s, SUBLANES), :] * hp + b_ref[:, pl.ds(s, SUBLANES), :]
        h_ref[:, pl.ds(s, SUBLANES), :] = h8
        return h8[:, SUBLANES - 1:SUBLANES, :]

    hp = lax.fori_loop(0, tt // SUBLANES, slab, hprev_ref[...])
    hprev_ref[...] = hp
    hlast_ref[...] = hp
    rec_ref[...] = (h_ref[...] * _gelu_tanh(gate_ref[...])).astype(rec_ref.dtype)


def lru(xr3, gate3, hist, h0, cw, cb, wa_bd, ba, wx_bd, bx, lam, ns, tt, out_dtype):
    nseq, length, c = xr3.shape
    gw = wa_bd.shape[-1]
    hk = CONV_W - 1
    grid = (nseq // ns, length // tt)
    blk = pl.BlockSpec((ns, tt, c), lambda s, t: (s, t, 0))
    vec = _const_spec((1, c))
    return pl.pallas_call(
        functools.partial(_lru_kernel, ns=ns, tt=tt, c=c, gw=gw),
        out_shape=[jax.ShapeDtypeStruct((nseq, length, c), out_dtype),
                   jax.ShapeDtypeStruct((nseq, hk, c), F32),
                   jax.ShapeDtypeStruct((nseq, 1, c), F32)],
        grid=grid,
        in_specs=[blk, blk,
                  pl.BlockSpec((ns, hk, c), lambda s, t: (s, 0, 0)),
                  pl.BlockSpec((ns, 1, c), lambda s, t: (s, 0, 0)),
                  _const_spec((CONV_W, c)), vec,
                  _const_spec(wa_bd.shape), vec, _const_spec(wx_bd.shape), vec, vec],
        out_specs=[blk,
                   pl.BlockSpec((ns, hk, c), lambda s, t: (s, 0, 0)),
                   pl.BlockSpec((ns, 1, c), lambda s, t: (s, 0, 0))],
        scratch_shapes=[pltpu.VMEM((ns, SUBLANES + tt, c), F32),
                        pltpu.VMEM((ns, tt, c), F32), pltpu.VMEM((ns, tt, c), F32), pltpu.VMEM((ns, tt, c), F32),
                        pltpu.VMEM((ns, 1, c), F32)],
        compiler_params=_cparams(("arbitrary", "arbitrary")), name="lru",
    )(xr3, gate3, hist, h0, cw, cb.reshape(1, c), wa_bd, ba.reshape(1, c), wx_bd, bx.reshape(1, c), lam.reshape(1, c))


def _swa_prompt_kernel(sink_ref, q_ref, kc_ref, kp_ref, vtc_ref, vtp_ref, o_ref):
    i = pl.program_id(0)
    w, hd = WINDOW, SWA_HEAD_DIM
    grp = SWA_HEADS // SWA_KV_HEADS
    nq = grp * w
    key = lax.broadcasted_iota(jnp.int32, (2 * w, nq), 0)
    qpos = lax.broadcasted_iota(jnp.int32, (2 * w, nq), 1) % w
    keep = ((key < w) & (key >= qpos) & (i > 0)) | ((key >= w) & ((key - w) <= qpos))
    colh = lax.broadcasted_iota(jnp.int32, (1, nq), 1) // w
    scale = 1.0 / math.sqrt(hd)
    for kh in range(SWA_KV_HEADS):
        ls = slice(kh * hd, (kh + 1) * hd)
        kk = jnp.concatenate([kp_ref[:, ls], kc_ref[:, ls]], axis=0).astype(BF16)
        q4 = jnp.concatenate([q_ref[:, (kh * grp + g) * hd:(kh * grp + g + 1) * hd] for g in range(grp)], axis=0)
        st = jnp.where(keep, _dot_nt(kk, (q4 * scale).astype(BF16)), NEG)
        sk = jnp.zeros((1, nq), F32)
        for g in range(grp):
            sk = jnp.where(colh == g, sink_ref[kh * grp + g], sk)
        m = jnp.maximum(jnp.max(st, axis=0, keepdims=True), sk)
        p = jnp.exp(st - m)
        den = jnp.sum(p, axis=0, keepdims=True) + jnp.exp(sk - m)
        vt = jnp.concatenate([vtp_ref[ls, :], vtc_ref[ls, :]], axis=1).astype(BF16)
        ot = _dot(vt, p.astype(BF16)) / den
        for g in range(grp):
            h = kh * grp + g
            o_ref[h * hd:(h + 1) * hd, :] = ot[:, g * w:(g + 1) * w].astype(o_ref.dtype)


def swa_prompt(q, k, vt, sink):
    length, qw = q.shape
    w = WINDOW
    kvw = SWA_KV_HEADS * SWA_HEAD_DIM
    cur = lambda i: (i, 0)
    prev = lambda i: (jnp.maximum(i - 1, 0), 0)
    cur_t = lambda i: (0, i)
    prev_t = lambda i: (0, jnp.maximum(i - 1, 0))
    return pl.pallas_call(
        _swa_prompt_kernel,
        out_shape=jax.ShapeDtypeStruct((qw, length), BF16),
        grid=(length // w,),
        in_specs=[pl.BlockSpec(memory_space=pltpu.SMEM),
                  pl.BlockSpec((w, qw), cur),
                  pl.BlockSpec((w, kvw), cur), pl.BlockSpec((w, kvw), prev),
                  pl.BlockSpec((kvw, w), cur_t), pl.BlockSpec((kvw, w), prev_t)],
        out_specs=pl.BlockSpec((qw, w), cur_t),
        compiler_params=_cparams(("parallel",)), name="swa_prompt",
    )(sink, q, k, k, vt, vt)


def _swa_sample_kernel(sink_ref, q_ref, kn_ref, vn_ref, kb_ref, vb_ref, o_ref, ko_ref, vo_ref, kk_ref, vv_ref, *, bs, sl):
    w, hd = WINDOW, SWA_HEAD_DIM
    grp = SWA_HEADS // SWA_KV_HEADS
    nk = kk_ref.shape[0]
    scale = 1.0 / math.sqrt(hd)
    kk_ref[w + sl:nk, :] = jnp.zeros((nk - w - sl, kk_ref.shape[1]), F32)
    vv_ref[w + sl:nk, :] = jnp.zeros((nk - w - sl, vv_ref.shape[1]), F32)
    row = lax.broadcasted_iota(jnp.int32, (grp * sl, nk), 0)
    col = lax.broadcasted_iota(jnp.int32, (grp * sl, nk), 1)
    qi = row % sl
    mask = (col >= qi) & (col <= qi + w)
    rowc = lax.broadcasted_iota(jnp.int32, (grp * sl, 1), 0)

    def seq(b, carry):
        kb, kn = kb_ref[b], kn_ref[b]
        vb, vn = vb_ref[b], vn_ref[b]
        ko_ref[b, 0:w - sl, :] = kb_ref[b, sl:w, :]
        ko_ref[b, w - sl:w, :] = kn
        vo_ref[b, 0:w - sl, :] = vb_ref[b, sl:w, :]
        vo_ref[b, w - sl:w, :] = vn
        kk_ref[0:w, :] = kb
        kk_ref[w:w + sl, :] = kn
        vv_ref[0:w, :] = vb
        vv_ref[w:w + sl, :] = vn
        q = q_ref[b]
        for kh in range(SWA_KV_HEADS):
            ls = slice(kh * hd, (kh + 1) * hd)
            kkh = kk_ref[:, ls].astype(BF16)
            vvh = vv_ref[:, ls].astype(BF16)
            qs = jnp.concatenate([q[:, (kh * grp + g) * hd:(kh * grp + g + 1) * hd] for g in range(grp)], axis=0)
            s = jnp.where(mask, _dot_nt((qs * scale).astype(BF16), kkh), NEG)
            sk = jnp.zeros((grp * sl, 1), F32)
            for g in range(grp):
                sk = jnp.where(rowc // sl == g, sink_ref[kh * grp + g], sk)
            m = jnp.maximum(jnp.max(s, axis=1, keepdims=True), sk)
            p = jnp.exp(s - m)
            den = jnp.sum(p, axis=1, keepdims=True) + jnp.exp(sk - m)
            o = _dot(p.astype(BF16), vvh) / den
            for g in range(grp):
                h = kh * grp + g
                o_ref[b, :, h * hd:(h + 1) * hd] = o[g * sl:(g + 1) * sl, :]
        return carry

    lax.fori_loop(0, bs, seq, 0)


def swa_sample(q3, kn3, vn3, kbuf, vbuf, sink, bs=8):
    nseq, sl, qw = q3.shape
    w = WINDOW
    kvw = kn3.shape[-1]
    nk = 2 * w
    b3 = lambda shape: pl.BlockSpec(shape, lambda i: (i, 0, 0))
    return pl.pallas_call(
        functools.partial(_swa_sample_kernel, bs=bs, sl=sl),
        out_shape=[jax.ShapeDtypeStruct((nseq, sl, qw), F32),
                   jax.ShapeDtypeStruct((nseq, w, kvw), F32),
                   jax.ShapeDtypeStruct((nseq, w, kvw), F32)],
        grid=(nseq // bs,),
        in_specs=[pl.BlockSpec(memory_space=pltpu.SMEM),
                  b3((bs, sl, qw)), b3((bs, sl, kvw)), b3((bs, sl, kvw)), b3((bs, w, kvw)), b3((bs, w, kvw))],
        out_specs=[b3((bs, sl, qw)), b3((bs, w, kvw)), b3((bs, w, kvw))],
        scratch_shapes=[pltpu.VMEM((nk, kvw), F32), pltpu.VMEM((nk, kvw), F32)],
        compiler_params=_cparams(("arbitrary",)), name="swa_sample",
    )(sink, q3, kn3, vn3, kbuf, vbuf)


def _ffn_kernel(x_ref, a1_ref, w1_ref, a2_ref, w2_ref, g_ref, wg_ref, wu_ref, cw_ref, cb_ref, wd_ref, hist_ref, fg_ref,
                o_ref, hout_ref, ext_ref, act_ref, *, ns, tt, d, f, tf, final, a2_t):
    t = pl.program_id(1)
    hk = FFN_CONV - 1

    @pl.when(t == 0)
    def _():
        ext_ref[:, SUBLANES - hk:SUBLANES, :] = hist_ref[...]

    a1 = a1_ref[...].reshape(ns * tt, a1_ref.shape[-1]).astype(BF16)
    if a2_t:
        second = _dot_tn(a2_ref[...].astype(BF16), w2_ref[...])
    else:
        second = _dot(a2_ref[...].reshape(ns * tt, a2_ref.shape[-1]).astype(BF16), w2_ref[...])
    x = x_ref[...].reshape(ns * tt, d) + _dot(a1, w1_ref[...]) + second
    xn = _rms(x, g_ref[...]).astype(BF16)
    for c in range(f // tf):
        sl = slice(c * tf, (c + 1) * tf)
        g3 = _dot(xn, wg_ref[:, sl]).reshape(ns, tt, tf)
        ext_ref[:, SUBLANES:SUBLANES + tt, sl] = g3
        gc = cb_ref[:, sl] + g3 * cw_ref[hk:hk + 1, sl]
        for k in range(hk):
            gc = gc + ext_ref[:, SUBLANES - hk + k:SUBLANES - hk + k + tt, sl] * cw_ref[k:k + 1, sl]
        u = _dot(xn, wu_ref[:, sl])
        act_ref[:, sl] = (_silu(gc).reshape(ns * tt, tf) * u).astype(BF16)
    y = x + _dot(act_ref[...], wd_ref[...])
    if final:
        y = _rms(y, fg_ref[...])
    o_ref[...] = y.reshape(ns, tt, d)
    hout_ref[...] = ext_ref[:, SUBLANES + tt - hk:SUBLANES + tt, :]
    ext_ref[:, 0:SUBLANES, :] = ext_ref[:, tt:tt + SUBLANES, :]


def ffn(x3, a1, w1, a2, w2, a2_t, g, wg, wu, cw, cb, wd, hist, fg, ns, tt, final, tf=256):
    nseq, length, d = x3.shape
    f = wg.shape[1]
    hk = FFN_CONV - 1
    blk = lambda width: pl.BlockSpec((ns, tt, width), lambda s, t: (s, t, 0))
    hspec = pl.BlockSpec((ns, hk, f), lambda s, t: (s, 0, 0))
    if a2_t:
        assert nseq == 1
        a2_spec = pl.BlockSpec((a2.shape[0], tt), lambda s, t: (0, t))
    else:
        a2_spec = blk(a2.shape[-1])
    return pl.pallas_call(
        functools.partial(_ffn_kernel, ns=ns, tt=tt, d=d, f=f, tf=tf, final=final, a2_t=a2_t),
        out_shape=[jax.ShapeDtypeStruct((nseq, length, d), F32), jax.ShapeDtypeStruct((nseq, hk, f), F32)],
        grid=(nseq // ns, length // tt),
        in_specs=[blk(d), blk(a1.shape[-1]), _const_spec(w1.shape), a2_spec, _const_spec(w2.shape),
                  _const_spec((1, d)), _const_spec(wg.shape), _const_spec(wu.shape),
                  _const_spec((FFN_CONV, f)), _const_spec((1, f)), _const_spec(wd.shape), hspec, _const_spec((1, d))],
        out_specs=[blk(d), hspec],
        scratch_shapes=[pltpu.VMEM((ns, SUBLANES + tt, f), F32), pltpu.VMEM((ns * tt, f), BF16)],
        compiler_params=_cparams(("arbitrary", "arbitrary")), name="ffn",
    )(x3, a1, w1, a2, w2, g.reshape(1, d), wg, wu, cw, cb.reshape(1, f), wd, hist, fg.reshape(1, d))


def _ssd_kernel(z_ref, xbc_ref, dt_ref, dtt_ref, hist_ref, s0_ref, cw_ref, cb_ref, dtb_ref, dtbt_ref,
                alog_ref, alogt_ref, dvec_ref, nw_ref,
                y_ref, conv_ref, sout_ref, ext_ref, st_ref, ybuf_ref, dtp_ref, dttp_ref, *, tin, tc, inner, nst):
    c = pl.program_id(1)
    carried = tin == tc
    hk = CONV_W - 1
    nh, hp = SSD_HEADS, SSD_HEAD_DIM
    gh = nh // SSD_GROUPS

    @pl.when(c == 0)
    def _():
        if tin < tc:
            ext_ref[...] = jnp.zeros(ext_ref.shape, F32)
            dtp_ref[...] = jnp.zeros(dtp_ref.shape, F32)
            dttp_ref[...] = jnp.zeros(dttp_ref.shape, F32)
        ext_ref[SUBLANES - hk:SUBLANES, :] = hist_ref[0]
        if carried:
            for h in range(nh):
                st_ref[:, h * hp:(h + 1) * hp] = s0_ref[0, h].T

    ext_ref[SUBLANES:SUBLANES + tin, :] = xbc_ref[0]
    xc = cb_ref[...]
    for k in range(CONV_W):
        xc = xc + ext_ref[SUBLANES - hk + k:SUBLANES - hk + k + tc, :] * cw_ref[k:k + 1, :]
    conv_ref[0] = ext_ref[SUBLANES + tin - hk:SUBLANES + tin, :]
    if tin == tc:
        ext_ref[0:SUBLANES, :] = ext_ref[tc:tc + SUBLANES, :]
    xa = _silu(xc)
    xs = xa[:, :inner]
    bm = [xa[:, inner + g * nst:inner + (g + 1) * nst].astype(BF16) for g in range(SSD_GROUPS)]
    cm = [xa[:, inner + (SSD_GROUPS + g) * nst:inner + (SSD_GROUPS + g + 1) * nst].astype(BF16) for g in range(SSD_GROUPS)]

    if tin < tc:
        dtp_ref[0:tin, :] = dt_ref[0]
        dttp_ref[:, 0:tin] = dtt_ref[0]
        dt_raw, dtt_raw = dtp_ref[...], dttp_ref[...]
        valid_r = lax.broadcasted_iota(jnp.int32, (tc, nh), 0) < tin
        valid_c = lax.broadcasted_iota(jnp.int32, (nh, tc), 1) < tin
        dt = jnp.where(valid_r, _softplus(dt_raw + dtb_ref[...]), 0.0)
        dtt = jnp.where(valid_c, _softplus(dtt_raw + dtbt_ref[...]), 0.0)
    else:
        dt = _softplus(dt_ref[0] + dtb_ref[...])
        dtt = _softplus(dtt_ref[0] + dtbt_ref[...])
    da = dt * (-jnp.exp(alog_ref[...]))
    dat = dtt * (-jnp.exp(alogt_ref[...]))

    li = lax.broadcasted_iota(jnp.int32, (tc, tc), 0)
    si = lax.broadcasted_iota(jnp.int32, (tc, tc), 1)
    causal = li >= si
    tri = jnp.where(causal, 1.0, 0.0).astype(BF16)
    trit = jnp.where(li <= si, 1.0, 0.0).astype(BF16)
    cs = sum(_dot(tri, p) for p in _split3(da))
    cst = sum(_dot(p, trit) for p in _split3(dat))
    cb_g =[_dot_nt(cm[g], bm[g]) for g in range(SSD_GROUPS)]

    gwid = inner // SSD_GROUPS
    if carried:
        erow = lax.broadcasted_iota(jnp.int32, (nh, inner), 0)
        ehead = lax.broadcasted_iota(jnp.int32, (nh, inner), 1) // hp
        expand = jnp.where(erow == ehead, 1.0, 0.0).astype(BF16)
        dt_e = sum(_dot(p, expand) for p in _split3(dt))
        cs_e = sum(_dot(p, expand) for p in _split3(cs))
        last_e = cs_e[tc - 1:tc, :]
        xd = xs * dt_e
        xdb = xd.astype(BF16)
        xdd = (xd * jnp.exp(last_e - cs_e)).astype(BF16)
        y_off = []
        for g in range(SSD_GROUPS):
            gs = slice(g * gwid, (g + 1) * gwid)
            st_g = st_ref[:, gs]
            y_off.append(_dot(cm[g], st_g.astype(BF16)))
            st_ref[:, gs] = jnp.exp(last_e[:, gs]) * st_g + _dot_tn(bm[g], xdd[:, gs])
        for h in range(nh):
            ls = slice(h * hp, (h + 1) * hp)
            lmat = jnp.exp(jnp.where(causal, cs[:, h:h + 1] - cst[h:h + 1, :], NEG))
            ybuf_ref[:, ls] = _dot((cb_g[h // gh] * lmat).astype(BF16), xdb[:, ls])
        y_all = ybuf_ref[...] + jnp.exp(cs_e) * jnp.concatenate(y_off, axis=1) + dvec_ref[...] * xs

        @pl.when(c == pl.num_programs(1) - 1)
        def _():
            for h in range(nh):
                sout_ref[0, h] = st_ref[:, h * hp:(h + 1) * hp].T
    else:
        for h in range(nh):
            g = h // gh
            ls = slice(h * hp, (h + 1) * hp)
            col = cs[:, h:h + 1]
            last = cs[tc - 1:tc, h:h + 1]
            lmat = jnp.exp(jnp.where(causal, col - cst[h:h + 1, :], NEG))
            xs_h = xs[:, ls]
            xd = xs_h * dt[:, h:h + 1]
            s_h = s0_ref[0, h]
            yh = _dot((cb_g[g] * lmat).astype(BF16), xd.astype(BF16))
            yh = yh + jnp.exp(col) * _dot_nt(cm[g], s_h.astype(BF16))
            sout_ref[0, h] = jnp.exp(last) * s_h + _dot_tn((xd * jnp.exp(last - col)).astype(BF16), bm[g])
            ybuf_ref[:, ls] = yh + dvec_ref[:, ls] * xs_h
        y_all = ybuf_ref[...]

    y = y_all[0:tin, :] * _silu(z_ref[0])
    for g in range(SSD_GROUPS):
        gs = slice(g * gwid, (g + 1) * gwid)
        y_ref[0, :, gs] = _rms(y[:, gs], nw_ref[:, gs]).astype(y_ref.dtype)


def ssd(z3, xbc3, dt3, dtt3, hist, s0, cw, cb, dtb, alog, dvec, nw, tin, out_dtype):
    nseq, length, inner = z3.shape
    cd = xbc3.shape[-1]
    nh, hp, nst = s0.shape[1:]
    tc = SSD_CHUNK if tin == SSD_CHUNK else -(-tin // BF16_SUBLANES) * BF16_SUBLANES
    assert tin == SSD_CHUNK or tin == length
    hk = CONV_W - 1
    t3 = lambda width: pl.BlockSpec((1, tin, width), lambda s, c: (s, c, 0))
    per_seq = lambda shape: pl.BlockSpec((1,) + shape, lambda s, c: (s,) + (0,) * len(shape))
    return pl.pallas_call(
        functools.partial(_ssd_kernel, tin=tin, tc=tc, inner=inner, nst=nst),
        out_shape=[jax.ShapeDtypeStruct((nseq, length, inner), out_dtype),
                   jax.ShapeDtypeStruct((nseq, hk, cd), F32),
                   jax.ShapeDtypeStruct(s0.shape, F32)],
        grid=(nseq, length // tin),
        in_specs=[t3(inner), t3(cd), t3(nh),
                  pl.BlockSpec((1, nh, tin), lambda s, c: (s, 0, c)),
                  per_seq((hk, cd)), per_seq((nh, hp, nst)),
                  _const_spec((CONV_W, cd)), _const_spec((1, cd)), _const_spec((1, nh)), _const_spec((nh, 1)),
                  _const_spec((1, nh)), _const_spec((nh, 1)), _const_spec((1, inner)), _const_spec((1, inner))],
        out_specs=[t3(inner), per_seq((hk, cd)), per_seq((nh, hp, nst))],
        scratch_shapes=[pltpu.VMEM((SUBLANES + tc, cd), F32), pltpu.VMEM((nst, inner), F32),
                        pltpu.VMEM((tc, inner), F32), pltpu.VMEM((tc, nh), F32), pltpu.VMEM((nh, tc), F32)],
        compiler_params=_cparams(("arbitrary", "arbitrary")), name="ssd",
    )(z3, xbc3, dt3, dtt3, hist, s0, cw, cb.reshape(1, cd), dtb.reshape(1, nh), dtb.reshape(nh, 1),
      alog.reshape(1, nh), alog.reshape(nh, 1), dvec.reshape(1, inner), nw.reshape(1, inner))


def _rope128(t, ctab, stab):
    half = QK_ROPE // 2
    lane = lax.broadcasted_iota(jnp.int32, t.shape, 1)
    swapped = jnp.where(lane < QK_NOPE + half, pltpu.roll(t, LANES - half, axis=1), pltpu.roll(t, half, axis=1))
    return t * ctab + swapped * stab


def _mla_prep_kernel(cq_ref, ckv_ref, kr_ref, ct_ref, st_ref, qg_ref, kg_ref, wuq_ref, wk_ref, wv_ref,
                     lat_ref, rope_ref, *outs, sample):
    scale = 1.0 / math.sqrt(QK_NOPE + QK_ROPE)
    ctab, stab = ct_ref[...], st_ref[...]
    qf = _dot(_rms(cq_ref[...], qg_ref[...]).astype(BF16), wuq_ref[...])
    ckv = _rms(ckv_ref[...], kg_ref[...])
    lat_ref[...] = ckv
    krr = _rope128(kr_ref[...], ctab, stab)
    rope_ref[...] = krr[:, QK_NOPE:QK_NOPE + QK_ROPE]
    if sample:
        qlat_ref, qr_ref = outs
        for h in range(MLA_HEADS):
            qh = _rope128(qf[:, h * LANES:(h + 1) * LANES], ctab, stab) * scale
            qlat_ref[h] = _dot(qh[:, :QK_NOPE].astype(BF16), wk_ref[h])
            qr_ref[h] = qh[:, QK_NOPE:QK_NOPE + QK_ROPE]
    else:
        qc_ref, kc_ref, vt_ref = outs
        ckvb = ckv.astype(BF16)
        vrows = vt_ref.shape[1]
        ones_row = jnp.where(lax.broadcasted_iota(jnp.int32, (vrows, ckvb.shape[0]), 0) == V_HEAD, 1.0, 0.0)
        for h in range(MLA_HEADS):
            qh = _rope128(qf[:, h * LANES:(h + 1) * LANES], ctab, stab) * (scale * LOG2E)
            qc_ref[h] = qh.astype(BF16)
            kc_ref[h] = (_dot(ckvb, wk_ref[h]) + krr).astype(BF16)
            vt_ref[h] = (_dot_nt(wv_ref[h], ckvb) + ones_row).astype(BF16)


def mla_prep(cq, ckv_raw, krp, ctab, stab, qg, kg, wuq_pad, wk, wv, sample, tm=512):
    m = cq.shape[0]
    tm = min(tm, m)
    nh = MLA_HEADS
    row = lambda width: pl.BlockSpec((tm, width), lambda i: (i, 0))
    hrow = lambda width: pl.BlockSpec((nh, tm, width), lambda i: (0, i, 0))
    lq, lkv = cq.shape[1], ckv_raw.shape[1]
    out_shape = [jax.ShapeDtypeStruct((m, lkv), F32), jax.ShapeDtypeStruct((m, QK_ROPE), F32)]
    out_specs = [row(lkv), row(QK_ROPE)]
    if sample:
        out_shape += [jax.ShapeDtypeStruct((nh, m, lkv), F32), jax.ShapeDtypeStruct((nh, m, QK_ROPE), F32)]
        out_specs += [hrow(lkv), hrow(QK_ROPE)]
    else:
        out_shape += [jax.ShapeDtypeStruct((nh, m, LANES), BF16)] * 2 + [jax.ShapeDtypeStruct((nh, VT_ROWS, m), BF16)]
        out_specs += [hrow(LANES)] * 2 + [pl.BlockSpec((nh, VT_ROWS, tm), lambda i: (0, 0, i))]
    return pl.pallas_call(
        functools.partial(_mla_prep_kernel, sample=sample),
        out_shape=out_shape, grid=(m // tm,),
        in_specs=[row(lq), row(lkv), row(LANES), row(LANES), row(LANES), _const_spec((1, lq)), _const_spec((1, lkv)),
                  _const_spec(wuq_pad.shape), _const_spec(wk.shape), _const_spec(wv.shape)],
        out_specs=out_specs,
        compiler_params=_cparams(("parallel",)), name="mla_prep",
    )(cq, ckv_raw, krp, ctab, stab, qg.reshape(1, lq), kg.reshape(1, lkv), wuq_pad, wk, wv)


def _mla_flash_kernel(qi_ref, kj_ref, q_ref, k_ref, vt_ref, o_ref, m_ref, acc_ref, *, t, nq):
    n = pl.program_id(0)
    i, j = qi_ref[n], kj_ref[n]

    @pl.when(j == 0)
    def _():
        m_ref[...] = jnp.full(m_ref.shape, NEG, F32)
        acc_ref[...] = jnp.zeros(acc_ref.shape, F32)

    def tile(u, masked):
        qs = slice(u * t, (u + 1) * t)
        if masked:
            keep = lax.broadcasted_iota(jnp.int32, (t, t), 0) <= lax.broadcasted_iota(jnp.int32, (t, t), 1)
        nxt = _dot_nt(k_ref[0], q_ref[0, qs, :])
        pending = None
        for h in range(MLA_HEADS + 1):
            if h < MLA_HEADS:
                st = nxt
                if h + 1 < MLA_HEADS:
                    nxt = _dot_nt(k_ref[h + 1], q_ref[h + 1, qs, :])
                if masked:
                    st = jnp.where(keep, st, NEG)
                m_prev = m_ref[h, :, qs]
                m_new = jnp.maximum(m_prev, jnp.max(st, axis=0, keepdims=True))
                alpha = jnp.exp2(m_prev - m_new)
                p = jnp.exp2(st - m_new).astype(BF16)
                m_ref[h, :, qs] = m_new
            if pending is not None:
                hp, alpha_p, p_p = pending
                acc_ref[hp, :, qs] = alpha_p * acc_ref[hp, :, qs] + _dot(vt_ref[hp], p_p)
            pending = (h, alpha, p) if h < MLA_HEADS else None

    for u in range(nq):
        qt = nq * i + u

        @pl.when(j < qt)
        def _():
            tile(u, False)

        @pl.when(j == qt)
        def _():
            tile(u, True)
            for h in range(MLA_HEADS):
                a = acc_ref[h, :, u * t:(u + 1) * t]
                o_ref[h * V_HEAD:(h + 1) * V_HEAD, u * t:(u + 1) * t] = (
                    a[0:V_HEAD, :] / a[V_HEAD:V_HEAD + 1, :]).astype(o_ref.dtype)


def mla_flash(qc, kc, vt, t=512, nq=2):
    nh, length, _ = qc.shape
    t = min(t, length)
    nq = min(nq, length // t)
    nb = length // (t * nq)
    qi = [i for i in range(nb) for _ in range(nq * i + nq)]
    kj = [j for i in range(nb) for j in range(nq * i + nq)]
    grid_spec = pltpu.PrefetchScalarGridSpec(
        num_scalar_prefetch=2, grid=(len(qi),),
        in_specs=[pl.BlockSpec((nh, nq * t, LANES), lambda n, qi, kj: (0, qi[n], 0)),
                  pl.BlockSpec((nh, t, LANES), lambda n, qi, kj: (0, kj[n], 0)),
                  pl.BlockSpec((nh, VT_ROWS, t), lambda n, qi, kj: (0, 0, kj[n]))],
        out_specs=pl.BlockSpec((nh * V_HEAD, nq * t), lambda n, qi, kj: (0, qi[n])),
        scratch_shapes=[pltpu.VMEM((nh, 1, nq * t), F32), pltpu.VMEM((nh, VT_ROWS, nq * t), F32)])
    return pl.pallas_call(
        functools.partial(_mla_flash_kernel, t=t, nq=nq),
        out_shape=jax.ShapeDtypeStruct((nh * V_HEAD, length), BF16),
        grid_spec=grid_spec,
        compiler_params=_cparams(("arbitrary",)), name="mla_flash",
    )(jnp.asarray(qi, jnp.int32), jnp.asarray(kj, jnp.int32), qc, kc, vt)


def _mla_sample_kernel(pt_ref, qlat_ref, qr_ref, ckv_ref, kr_ref, wuv_ref, lat_hbm, rope_hbm,
                       o_ref, latbuf, ropebuf, sem, kbuf_ref, sbuf_ref, kl_ref, krl_ref,
                       *, cp, sl, npages, li):
    b, nb = pl.program_id(0), pl.num_programs(0)
    rows = MLA_HEADS * sl
    psz = latbuf.shape[2]
    pair = 2 * psz
    nch = npages // cp

    def chunk_copies(seq, ch, slot):
        out = []
        for c in range(cp):
            pid = pt_ref[seq, ch * cp + c]
            out.append(pltpu.make_async_copy(lat_hbm.at[li, pid], latbuf.at[slot, c], sem.at[0, slot]))
            out.append(pltpu.make_async_copy(rope_hbm.at[li, pid], ropebuf.at[slot, c], sem.at[1, slot]))
        return out

    def start(seq, ch, slot):
        for cpy in chunk_copies(seq, ch, slot):
            cpy.start()

    def wait(seq, ch, slot):
        for cpy in chunk_copies(seq, ch, slot):
            cpy.wait()

    nslots = latbuf.shape[0]
    ahead = nslots - 1

    @pl.when(b == 0)
    def _():
        for ch in range(ahead):
            start(0, ch, ch % nslots)

    q = qlat_ref[...].reshape(rows, qlat_ref.shape[-1]).astype(BF16)
    qr = qr_ref[...].reshape(rows, qr_ref.shape[-1]).astype(BF16)

    mrun = None
    for ch in range(nch):
        slot = ch % nslots
        wait(b, ch, slot)
        nxt = ch + ahead
        if nxt < nch:
            start(b, nxt, nxt % nslots)
        else:
            @pl.when(b + 1 < nb)
            def _():
                start(b + 1, nxt - nch, nxt % nslots)
        for c2 in range(cp // 2):
            base = (ch * cp + 2 * c2) * psz
            kk = jnp.concatenate([latbuf[slot, 2 * c2], latbuf[slot, 2 * c2 + 1]], axis=0).astype(BF16)
            kbuf_ref[base:base + pair, :] = kk
            rt = jnp.concatenate([ropebuf[slot, 2 * c2], ropebuf[slot, 2 * c2 + 1]], axis=1).astype(BF16)
            s = _dot_nt(q, kk) + _dot(qr, rt)
            sbuf_ref[:, base:base + pair] = s
            mrun = s if mrun is None else jnp.maximum(mrun, s)

    kl_ref[...] = jnp.zeros(kl_ref.shape, F32)
    krl_ref[...] = jnp.zeros(krl_ref.shape, F32)
    kl_ref[0:sl, :] = ckv_ref[0]
    krl_ref[0:sl, :] = kr_ref[0]
    kl = kl_ref[...].astype(BF16)
    kbuf_ref[npages * psz:npages * psz + pair, :] = kl
    s = _dot_nt(q, kl) + _dot_nt(qr, krl_ref[...].astype(BF16))
    row = lax.broadcasted_iota(jnp.int32, s.shape, 0)
    col = lax.broadcasted_iota(jnp.int32, s.shape, 1)
    s = jnp.where((col < sl) & (col <= row % sl), s, NEG)
    sbuf_ref[:, npages * psz:npages * psz + pair] = s

    m = jnp.broadcast_to(jnp.max(jnp.maximum(mrun, s), axis=1, keepdims=True), s.shape)
    nblk = sbuf_ref.shape[1] // pair
    hw = kbuf_ref.shape[1] // 2
    lacc = jnp.zeros(s.shape, F32)
    accs = [jnp.zeros((rows, hw), F32), jnp.zeros((rows, hw), F32)]
    for blk in range(nblk):
        cols = slice(blk * pair, (blk + 1) * pair)
        p = jnp.exp(sbuf_ref[:, cols] - m)
        lacc = lacc + p
        pb = p.astype(BF16)
        accs[0] = accs[0] + _dot(pb, kbuf_ref[cols, 0:hw])
        accs[1] = accs[1] + _dot(pb, kbuf_ref[cols, hw:2 * hw])
    lsum = jnp.sum(lacc, axis=1, keepdims=True)
    accn = (jnp.concatenate(accs, axis=1) / lsum).astype(BF16)
    for h in range(MLA_HEADS):
        o_ref[0, :, h * V_HEAD:(h + 1) * V_HEAD] = _dot(accn[h * sl:(h + 1) * sl, :], wuv_ref[h])


def mla_sample(qlat, qr, ckv3, kr3, wuv, cache_lat, cache_rope_t, page_table, li, cp=32):
    nh, m, lkv = qlat.shape
    nseq, sl, rd = kr3.shape
    npages = page_table.shape[1]
    nslots = MLA_PAGE_SLOTS
    cp = min(cp, npages // nslots)
    assert npages % (nslots * cp) == 0 and cp % 2 == 0
    psz = cache_lat.shape[2]
    nkeys = npages * psz + 2 * psz
    rows = nh * sl
    in_specs = [pl.BlockSpec((nh, sl, lkv), lambda b, pt: (0, b, 0)),
                pl.BlockSpec((nh, sl, rd), lambda b, pt: (0, b, 0)),
                pl.BlockSpec((1, sl, lkv), lambda b, pt: (b, 0, 0)),
                pl.BlockSpec((1, sl, rd), lambda b, pt: (b, 0, 0)),
                pl.BlockSpec(wuv.shape, lambda b, pt: (0, 0, 0)),
                pl.BlockSpec(memory_space=pl.ANY), pl.BlockSpec(memory_space=pl.ANY)]
    grid_spec = pltpu.PrefetchScalarGridSpec(
        num_scalar_prefetch=1, grid=(nseq,), in_specs=in_specs,
        out_specs=pl.BlockSpec((1, sl, nh * V_HEAD), lambda b, pt: (b, 0, 0)),
        scratch_shapes=[pltpu.VMEM((nslots, cp, psz, lkv), F32), pltpu.VMEM((nslots, cp, rd, psz), F32),
                        pltpu.SemaphoreType.DMA((2, nslots)),
                        pltpu.VMEM((nkeys, lkv), BF16), pltpu.VMEM((rows, nkeys), F32),
                        pltpu.VMEM((2 * psz, lkv), F32), pltpu.VMEM((2 * psz, rd), F32)])
    return pl.pallas_call(
        functools.partial(_mla_sample_kernel, cp=cp, sl=sl, npages=npages, li=li),
        out_shape=jax.ShapeDtypeStruct((nseq, sl, nh * V_HEAD), F32),
        grid_spec=grid_spec,
        compiler_params=_cparams(("arbitrary",)), name="mla_sample",
    )(page_table, qlat, qr, ckv3, kr3, wuv, cache_lat, cache_rope_t)


def _block_diag(w, per):
    nh, b, _ = w.shape
    w4 = w.reshape(nh // per, per, b, b)
    eye = jnp.eye(per, dtype=w.dtype)
    return jnp.einsum("gaij,ab->gaibj", w4, eye).reshape(nh // per, per * b, per * b)


def _rope_tables(pos):
    half = QK_ROPE // 2
    inv = jnp.exp(-(math.log(ROPE_THETA) / half) * jnp.arange(half, dtype=F32))
    ang = pos.astype(F32)[:, None] * inv[None, :]
    cos, sin = jnp.cos(ang), jnp.sin(ang)
    n = pos.shape[0]
    pad = LANES - QK_NOPE - QK_ROPE
    ctab = jnp.concatenate([jnp.ones((n, QK_NOPE), F32), cos, cos, jnp.ones((n, pad), F32)], axis=1)
    stab = jnp.concatenate([jnp.zeros((n, QK_NOPE), F32), -sin, sin, jnp.zeros((n, pad), F32)], axis=1)
    return ctab, stab


def _prep_weights(P):
    w = {}
    lw = P["lru_conv_w"].shape[-1]
    qw = SWA_HEADS * SWA_HEAD_DIM
    kvw = SWA_KV_HEADS * SWA_HEAD_DIM
    ew = P["even_w_in"][0].astype(BF16)
    b = [0, lw, 2 * lw, 2 * lw + qw, 2 * lw + qw + kvw, 2 * lw + qw + 2 * kvw]
    w["even_in"] = [ew[:, b[i]:b[i + 1]] for i in range(5)]
    per = 256 // (lw // LRU_HEADS)
    w["wa_bd"] = _block_diag(P["lru_wa"][0], per).astype(BF16)
    w["wx_bd"] = _block_diag(P["lru_wx"][0], per).astype(BF16)
    eo = P["even_w_out"][0].astype(BF16)
    w["even_out"] = (eo[:lw], eo[lw:])

    inner = P["ssd_norm"].shape[-1]
    cd = P["ssd_conv_w"].shape[-1]
    lq = P["mla_q_norm"].shape[-1]
    lkv = P["mla_kv_norm"].shape[-1]
    ow = P["odd_w_in"][0]
    b = [0, inner, inner + cd, inner + cd + SSD_HEADS, inner + cd + SSD_HEADS + lq,
         inner + cd + SSD_HEADS + lq + lkv, inner + cd + SSD_HEADS + lq + lkv + QK_ROPE]
    parts = [ow[:, b[i]:b[i + 1]] for i in range(6)]
    pad = LANES - QK_NOPE - QK_ROPE
    kr_pad = jnp.pad(parts[5], ((0, 0), (QK_NOPE, pad)))
    w["odd_in"] = [parts[0].astype(BF16), parts[1].astype(BF16), parts[2].astype(BF16), parts[2].T.astype(BF16),
                   parts[3].astype(BF16), parts[4].astype(BF16), kr_pad.astype(BF16)]
    uq = P["mla_w_uq"][0].reshape(lq, MLA_HEADS, QK_NOPE + QK_ROPE)
    w["wuq_pad"] = jnp.pad(uq, ((0, 0), (0, 0), (0, pad))).reshape(lq, MLA_HEADS * LANES).astype(BF16)
    uk = P["mla_w_uk"][0]
    uv = P["mla_w_uv"][0]
    w["wuk_pad"] = jnp.pad(jnp.transpose(uk, (1, 0, 2)), ((0, 0), (0, 0), (0, LANES - QK_NOPE))).astype(BF16)
    w["wuk_t"] = jnp.transpose(uk, (1, 2, 0)).astype(BF16)
    w["wuv_t_pad"] = jnp.pad(jnp.transpose(uv, (1, 2, 0)), ((0, 0), (0, VT_ROWS - V_HEAD), (0, 0))).astype(BF16)
    w["wuv"] = jnp.transpose(uv, (1, 0, 2)).astype(BF16)
    oo = P["odd_w_out"][0].astype(BF16)
    w["odd_out"] = (oo[:inner], oo[inner:])
    w["dvec"] = jnp.repeat(P["ssd_d"][0], SSD_HEAD_DIM)
    w["ffn"] = [(P["ffn_w_gate"][l].astype(BF16), P["ffn_w_up"][l].astype(BF16), P["ffn_w_down"][l].astype(BF16))
                for l in range(P["ffn_w_gate"].shape[0])]
    return w


def _trunk(x3, pos, P, W, st, sample):
    nseq, length, d = x3.shape
    m = nseq * length
    lw = P["lru_conv_w"].shape[-1]
    kvw = SWA_KV_HEADS * SWA_HEAD_DIM
    if sample:
        seq_tile, time_tile = min(nseq, 64), length
    else:
        seq_tile, time_tile = 1, min(length, 512)
    mix_dtype = F32 if sample else BF16
    out = {}

    if sample:
        xr, gate, q, k, v = norm_proj(x3.reshape(m, d), P["mix_norm"][0], W["even_in"], [False] * 5)
    else:
        xr, gate, q, k, v, vt = norm_proj(x3.reshape(m, d), P["mix_norm"][0], W["even_in"] + [W["even_in"][4].T],
                                          [False] * 5 + [True])
    lru_ns, lru_tt = (min(nseq, 32), length) if sample else (1, min(length, 256))
    rec, out["lru_conv"], h_last = lru(
        xr.reshape(nseq, length, lw), gate.reshape(nseq, length, lw), st["lru_conv"], st["lru_h"].reshape(nseq, 1, lw),
        P["lru_conv_w"][0], P["lru_conv_b"][0], W["wa_bd"], P["lru_ba"][0], W["wx_bd"], P["lru_bx"][0],
        P["lru_lambda"][0], lru_ns, lru_tt, mix_dtype)
    out["lru_h"] = h_last.reshape(nseq, lw)
    if sample:
        att, sk, sv = swa_sample(q.reshape(nseq, length, -1), k.reshape(nseq, length, kvw), v.reshape(nseq, length, kvw),
                                 st["swa_k"].reshape(nseq, WINDOW, kvw), st["swa_v"].reshape(nseq, WINDOW, kvw),
                                 P["swa_sink"][0], bs=min(nseq, 8))
        out["swa_k"] = sk.reshape(nseq, WINDOW, SWA_KV_HEADS, SWA_HEAD_DIM)
        out["swa_v"] = sv.reshape(nseq, WINDOW, SWA_KV_HEADS, SWA_HEAD_DIM)
    else:
        att = swa_prompt(q, k, vt, P["swa_sink"][0])
        out["swa_k"] = k[-WINDOW:].reshape(1, WINDOW, SWA_KV_HEADS, SWA_HEAD_DIM)
        out["swa_v"] = v[-WINDOW:].reshape(1, WINDOW, SWA_KV_HEADS, SWA_HEAD_DIM)
    wg, wu, wd = W["ffn"][0]
    x3, fc0 = ffn(x3, rec, W["even_out"][0], att, W["even_out"][1], not sample,
                  P["ffn_norm"][0], wg, wu, P["ffn_conv_w"][0], P["ffn_conv_b"][0], wd,
                  st["ffn_conv"][0], P["final_norm"], seq_tile, time_tile, final=False)

    z, xbc, dt, dtt, cq, ckv_raw, krp = norm_proj(x3.reshape(m, d), P["mix_norm"][1], W["odd_in"],
                                                   [False, False, False, True, False, False, False])
    inner = z.shape[-1]
    cd = xbc.shape[-1]
    dtt3 = jnp.transpose(dtt.reshape(SSD_HEADS, nseq, length), (1, 0, 2))
    y, out["ssd_conv"], out["ssd"] = ssd(
        z.reshape(nseq, length, inner), xbc.reshape(nseq, length, cd), dt.reshape(nseq, length, SSD_HEADS), dtt3,
        st["ssd_conv"], st["ssd"], P["ssd_conv_w"][0], P["ssd_conv_b"][0], P["ssd_dt_bias"][0], P["ssd_a_log"][0],
        W["dvec"], P["ssd_norm"][0], tin=min(length, SSD_CHUNK), out_dtype=mix_dtype)
    ctab, stab = _rope_tables(pos)
    if sample:
        lat, rp, qlat, qr = mla_prep(cq, ckv_raw, krp, ctab, stab, P["mla_q_norm"][0], P["mla_kv_norm"][0],
                                     W["wuq_pad"], W["wuk_t"], W["wuv"], sample=True)
        att = mla_sample(qlat, qr, lat.reshape(nseq, length, -1), rp.reshape(nseq, length, -1), W["wuv"],
                         st["mla_latent"], jnp.swapaxes(st["mla_rope"], 2, 3), st["page_table"], 0)
    else:
        lat, rp, qc, kc, vt = mla_prep(cq, ckv_raw, krp, ctab, stab, P["mla_q_norm"][0], P["mla_kv_norm"][0],
                                       W["wuq_pad"], W["wuk_pad"], W["wuv_t_pad"], sample=False)
        att = mla_flash(qc, kc, vt)
    out["mla_latent"] = lat.reshape(nseq, length, -1)
    out["mla_rope"] = rp.reshape(nseq, length, -1)
    wg, wu, wd = W["ffn"][1]
    y3, fc1 = ffn(x3, y, W["odd_out"][0], att, W["odd_out"][1], not sample,
                  P["ffn_norm"][1], wg, wu, P["ffn_conv_w"][1], P["ffn_conv_b"][1], wd,
                  st["ffn_conv"][1], P["final_norm"], seq_tile, time_tile, final=True)
    out["ffn_conv"] = jnp.stack([fc0, fc1])
    return y3, out


def kernel(x_prompt, x_sample, state_lru_conv, state_lru_h, cache_swa_k, cache_swa_v, state_ssd_conv, state_ssd,
           cache_mla_latent, cache_mla_rope, page_table, state_ffn_conv, mix_norm, ffn_norm, final_norm, even_w_in,
           lru_conv_w, lru_conv_b, lru_wa, lru_ba, lru_wx, lru_bx, lru_lambda, swa_sink, even_w_out, odd_w_in,
           ssd_conv_w, ssd_conv_b, ssd_dt_bias, ssd_a_log, ssd_d, ssd_norm, mla_q_norm, mla_w_uq, mla_kv_norm,
           mla_w_uk, mla_w_uv, odd_w_out, ffn_w_gate, ffn_w_up, ffn_conv_w, ffn_conv_b, ffn_w_down):
    P = dict(mix_norm=mix_norm, ffn_norm=ffn_norm, final_norm=final_norm, even_w_in=even_w_in,
             lru_conv_w=lru_conv_w, lru_conv_b=lru_conv_b, lru_wa=lru_wa, lru_ba=lru_ba, lru_wx=lru_wx,
             lru_bx=lru_bx, lru_lambda=lru_lambda, swa_sink=swa_sink, even_w_out=even_w_out,
             odd_w_in=odd_w_in, ssd_conv_w=ssd_conv_w, ssd_conv_b=ssd_conv_b, ssd_dt_bias=ssd_dt_bias,
             ssd_a_log=ssd_a_log, ssd_d=ssd_d, ssd_norm=ssd_norm, mla_q_norm=mla_q_norm,
             mla_w_uq=mla_w_uq, mla_kv_norm=mla_kv_norm, mla_w_uk=mla_w_uk, mla_w_uv=mla_w_uv,
             odd_w_out=odd_w_out, ffn_w_gate=ffn_w_gate, ffn_w_up=ffn_w_up, ffn_conv_w=ffn_conv_w,
             ffn_conv_b=ffn_conv_b, ffn_w_down=ffn_w_down)
    W = _prep_weights(P)
    bp, lp, d = x_prompt.shape
    bs, ls, _ = x_sample.shape
    depth = ffn_w_gate.shape[0]
    lw = lru_conv_w.shape[-1]
    cd = ssd_conv_w.shape[-1]
    f = ffn_w_gate.shape[-1]
    kvw = SWA_KV_HEADS * SWA_HEAD_DIM
    past_len = page_table.shape[1] * PAGE_SIZE

    st_p = dict(lru_conv=jnp.zeros((bp, CONV_W - 1, lw), F32), lru_h=jnp.zeros((bp, lw), F32),
                ssd_conv=jnp.zeros((bp, CONV_W - 1, cd), F32),
                ssd=jnp.zeros((bp, SSD_HEADS, SSD_HEAD_DIM, SSD_STATE), F32),
                ffn_conv=jnp.zeros((depth, bp, FFN_CONV - 1, f), F32))
    st_s = dict(lru_conv=state_lru_conv[0], lru_h=state_lru_h[0], swa_k=cache_swa_k[0], swa_v=cache_swa_v[0],
                ssd_conv=state_ssd_conv[0], ssd=state_ssd[0], mla_latent=cache_mla_latent, mla_rope=cache_mla_rope,
                page_table=page_table, ffn_conv=state_ffn_conv)
    pos_p = jnp.tile(jnp.arange(lp), bp)
    pos_s = jnp.tile(past_len + jnp.arange(ls), bs)
    y_p, sp = _trunk(x_prompt, pos_p, P, W, st_p, False)
    y_s, ss = _trunk(x_sample, pos_s, P, W, st_s, True)
    e = lambda a: a[None]
    return (y_p, y_s,
            e(sp["lru_conv"]), e(ss["lru_conv"]), e(sp["lru_h"]), e(ss["lru_h"]),
            e(sp["swa_k"]), e(ss["swa_k"]), e(sp["swa_v"]), e(ss["swa_v"]),
            e(sp["ssd_conv"]), e(ss["ssd_conv"]), e(sp["ssd"]), e(ss["ssd"]),
            e(sp["mla_latent"]), e(ss["mla_latent"]), e(sp["mla_rope"]), e(ss["mla_rope"]),
            sp["ffn_conv"], ss["ffn_conv"])
```

```python
import functools
import math

import jax
import jax.numpy as jnp
from jax import lax
from jax.experimental import pallas as pl
from jax.experimental.pallas import tpu as pltpu

F32 = jnp.float32
BF16 = jnp.bfloat16

EPS = 1e-6
LRU_C = 8.0
LRU_HEADS = 16
CONV_W = 4
SWA_HEADS = 8
SWA_KV_HEADS = 2
SWA_HEAD_DIM = 64
WINDOW = 128
SSD_HEADS = 16
SSD_HEAD_DIM = 64
SSD_GROUPS = 2
SSD_STATE = 128
SSD_CHUNK = 128
MLA_HEADS = 8
QK_NOPE = 64
QK_ROPE = 32
V_HEAD = 64
ROPE_THETA = 10000.0
PAGE_SIZE = 128
FFN_CONV = 3
NEG = -1e30
LOG2E = math.log2(math.e)
VT_ROWS = 80
MLA_PAGE_SLOTS = 4

VMEM_LIMIT = 56 * 1024 * 1024
SUBLANES = 8
BF16_SUBLANES = 16
LANES = 128


def _cparams(sem):
    return pltpu.CompilerParams(dimension_semantics=sem, vmem_limit_bytes=VMEM_LIMIT)


def _dot(a, b):
    return jnp.dot(a, b, preferred_element_type=F32)


def _dot_nt(a, b):
    return lax.dot_general(a, b, (((1,), (1,)), ((), ())), preferred_element_type=F32)


def _dot_tn(a, b):
    return lax.dot_general(a, b, (((0,), (0,)), ((), ())), preferred_element_type=F32)


def _sigmoid(x):
    return 1.0 / (1.0 + jnp.exp(-x))


def _silu(x):
    return x * _sigmoid(x)


def _softplus(x):
    return jnp.maximum(x, 0.0) + jnp.log1p(jnp.exp(-jnp.abs(x)))


def _gelu_tanh(x):
    return 0.5 * x * (1.0 + jnp.tanh(math.sqrt(2.0 / math.pi) * (x + 0.044715 * (x * x * x))))


def _rms(x, g):
    return x * lax.rsqrt(jnp.mean(x * x, axis=-1, keepdims=True) + EPS) * g


def _split3(x):
    x1 = x.astype(BF16)
    r1 = x - x1.astype(F32)
    x2 = r1.astype(BF16)
    x3 = (r1 - x2.astype(F32)).astype(BF16)
    return x1, x2, x3


def _const_spec(shape):
    nd = len(shape)
    return pl.BlockSpec(shape, lambda *_: (0,) * nd, pipeline_mode=pl.Buffered(1))


def _norm_proj_kernel(x_ref, g_ref, *refs, nts):
    n = len(nts)
    w_refs, o_refs = refs[:n], refs[n:]
    xn = _rms(x_ref[...], g_ref[...]).astype(BF16)
    for w_ref, o_ref, nt in zip(w_refs, o_refs, nts):
        if nt:
            o_ref[...] = _dot_nt(w_ref[...], xn)
        else:
            o_ref[...] = _dot(xn, w_ref[...])


def norm_proj(x2, g, ws, nts, tm=512):
    m, k = x2.shape
    tm = min(tm, m)
    in_specs = [pl.BlockSpec((tm, k), lambda i: (i, 0)), _const_spec((1, k))]
    out_shape, out_specs = [], []
    for w, nt in zip(ws, nts):
        in_specs.append(_const_spec(w.shape))
        if nt:
            out_shape.append(jax.ShapeDtypeStruct((w.shape[0], m), F32))
            out_specs.append(pl.BlockSpec((w.shape[0], tm), lambda i: (0, i)))
        else:
            out_shape.append(jax.ShapeDtypeStruct((m, w.shape[1]), F32))
            out_specs.append(pl.BlockSpec((tm, w.shape[1]), lambda i: (i, 0)))
    return pl.pallas_call(
        functools.partial(_norm_proj_kernel, nts=tuple(nts)),
        out_shape=out_shape, grid=(m // tm,), in_specs=in_specs, out_specs=out_specs,
        compiler_params=_cparams(("parallel",)), name="norm_proj",
    )(x2, g.reshape(1, k), *ws)


def _lru_kernel(xr_ref, gate_ref, hist_ref, h0_ref, cw_ref, cb_ref, wa_ref, ba_ref, wx_ref, bx_ref, lam_ref,
                rec_ref, conv_ref, hlast_ref, ext_ref, a_ref, b_ref, h_ref, hprev_ref, *, ns, tt, c, gw):
    t = pl.program_id(1)
    hk = CONV_W - 1

    @pl.when(t == 0)
    def _():
        ext_ref[:, SUBLANES - hk:SUBLANES, :] = hist_ref[...]
        hprev_ref[...] = h0_ref[...]

    ext_ref[:, SUBLANES:SUBLANES + tt, :] = xr_ref[...]
    xc = cb_ref[...]
    for k in range(CONV_W):
        xc = xc + ext_ref[:, SUBLANES - hk + k:SUBLANES - hk + k + tt, :] * cw_ref[k:k + 1, :]
    conv_ref[...] = ext_ref[:, SUBLANES + tt - hk:SUBLANES + tt, :]
    ext_ref[:, 0:SUBLANES, :] = ext_ref[:, tt:tt + SUBLANES, :]

    x2 = xc.reshape(ns * tt, c)
    ra, rx = [], []
    for j in range(c // gw):
        xg = x2[:, j * gw:(j + 1) * gw].astype(BF16)
        ra.append(_dot(xg, wa_ref[j]))
        rx.append(_dot(xg, wx_ref[j]))
    r = _sigmoid(jnp.concatenate(ra, axis=1) + ba_ref[...])
    ig = _sigmoid(jnp.concatenate(rx, axis=1) + bx_ref[...])
    log_a = (-LRU_C) * r * _softplus(-lam_ref[...])
    a = jnp.exp(log_a)
    b = jnp.sqrt(-jnp.tanh(log_a) * (a * a + 1.0)) * (ig * x2)

    a3 = a.reshape(ns * tt // SUBLANES, SUBLANES, c)
    b3 = b.reshape(ns * tt // SUBLANES, SUBLANES, c)
    row = lax.broadcasted_iota(jnp.int32, a3.shape, 1)
    d = 1
    while d < SUBLANES:
        a_sh = jnp.where(row >= d, pltpu.roll(a3, d, axis=1), 1.0)
        b_sh = jnp.where(row >= d, pltpu.roll(b3, d, axis=1), 0.0)
        b3 = a3 * b_sh + b3
        a3 = a3 * a_sh
        d *= 2
    a_ref[...] = a3.reshape(ns, tt, c)
    b_ref[...] = b3.reshape(ns, tt, c)

    def slab(j, hp):
        s = pl.multiple_of(j * SUBLANES, SUBLANES)
        h8 = a_ref[:, pl.ds(s, SUBLANES), :] * hp + b_ref[:, pl.ds(s, SUBLANES), :]
        h_ref[:, pl.ds(s, SUBLANES), :] = h8
        return h8[:, SUBLANES - 1:SUBLANES, :]

    hp = lax.fori_loop(0, tt // SUBLANES, slab, hprev_ref[...])
    hprev_ref[...] = hp
    hlast_ref[...] = hp
    rec_ref[...] = (h_ref[...] * _gelu_tanh(gate_ref[...])).astype(rec_ref.dtype)


def lru(xr3, gate3, hist, h0, cw, cb, wa_bd, ba, wx_bd, bx, lam, ns, tt, out_dtype):
    nseq, length, c = xr3.shape
    gw = wa_bd.shape[-1]
    hk = CONV_W - 1
    grid = (nseq // ns, length // tt)
    blk = pl.BlockSpec((ns, tt, c), lambda s, t: (s, t, 0))
    vec = _const_spec((1, c))
    return pl.pallas_call(
        functools.partial(_lru_kernel, ns=ns, tt=tt, c=c, gw=gw),
        out_shape=[jax.ShapeDtypeStruct((nseq, length, c), out_dtype),
                   jax.ShapeDtypeStruct((nseq, hk, c), F32),
                   jax.ShapeDtypeStruct((nseq, 1, c), F32)],
        grid=grid,
        in_specs=[blk, blk,
                  pl.BlockSpec((ns, hk, c), lambda s, t: (s, 0, 0)),
                  pl.BlockSpec((ns, 1, c), lambda s, t: (s, 0, 0)),
                  _const_spec((CONV_W, c)), vec,
                  _const_spec(wa_bd.shape), vec, _const_spec(wx_bd.shape), vec, vec],
        out_specs=[blk,
                   pl.BlockSpec((ns, hk, c), lambda s, t: (s, 0, 0)),
                   pl.BlockSpec((ns, 1, c), lambda s, t: (s, 0, 0))],
        scratch_shapes=[pltpu.VMEM((ns, SUBLANES + tt, c), F32),
                        pltpu.VMEM((ns, tt, c), F32), pltpu.VMEM((ns, tt, c), F32), pltpu.VMEM((ns, tt, c), F32),
                        pltpu.VMEM((ns, 1, c), F32)],
        compiler_params=_cparams(("arbitrary", "arbitrary")), name="lru",
    )(xr3, gate3, hist, h0, cw, cb.reshape(1, c), wa_bd, ba.reshape(1, c), wx_bd, bx.reshape(1, c), lam.reshape(1, c))


def _swa_prompt_kernel(sink_ref, q_ref, kc_ref, kp_ref, vtc_ref, vtp_ref, o_ref):
    i = pl.program_id(0)
    w, hd = WINDOW, SWA_HEAD_DIM
    grp = SWA_HEADS // SWA_KV_HEADS
    nq = grp * w
    key = lax.broadcasted_iota(jnp.int32, (2 * w, nq), 0)
    qpos = lax.broadcasted_iota(jnp.int32, (2 * w, nq), 1) % w
    keep = ((key < w) & (key >= qpos) & (i > 0)) | ((key >= w) & ((key - w) <= qpos))
    colh = lax.broadcasted_iota(jnp.int32, (1, nq), 1) // w
    scale = 1.0 / math.sqrt(hd)
    for kh in range(SWA_KV_HEADS):
        ls = slice(kh * hd, (kh + 1) * hd)
        kk = jnp.concatenate([kp_ref[:, ls], kc_ref[:, ls]], axis=0).astype(BF16)
        q4 = jnp.concatenate([q_ref[:, (kh * grp + g) * hd:(kh * grp + g + 1) * hd] for g in range(grp)], axis=0)
        st = jnp.where(keep, _dot_nt(kk, (q4 * scale).astype(BF16)), NEG)
        sk = jnp.zeros((1, nq), F32)
        for g in range(grp):
            sk = jnp.where(colh == g, sink_ref[kh * grp + g], sk)
        m = jnp.maximum(jnp.max(st, axis=0, keepdims=True), sk)
        p = jnp.exp(st - m)
        den = jnp.sum(p, axis=0, keepdims=True) + jnp.exp(sk - m)
        vt = jnp.concatenate([vtp_ref[ls, :], vtc_ref[ls, :]], axis=1).astype(BF16)
        ot = _dot(vt, p.astype(BF16)) / den
        for g in range(grp):
            h = kh * grp + g
            o_ref[h * hd:(h + 1) * hd, :] = ot[:, g * w:(g + 1) * w].astype(o_ref.dtype)


def swa_prompt(q, k, vt, sink):
    length, qw = q.shape
    w = WINDOW
    kvw = SWA_KV_HEADS * SWA_HEAD_DIM
    cur = lambda i: (i, 0)
    prev = lambda i: (jnp.maximum(i - 1, 0), 0)
    cur_t = lambda i: (0, i)
    prev_t = lambda i: (0, jnp.maximum(i - 1, 0))
    return pl.pallas_call(
        _swa_prompt_kernel,
        out_shape=jax.ShapeDtypeStruct((qw, length), BF16),
        grid=(length // w,),
        in_specs=[pl.BlockSpec(memory_space=pltpu.SMEM),
                  pl.BlockSpec((w, qw), cur),
                  pl.BlockSpec((w, kvw), cur), pl.BlockSpec((w, kvw), prev),
                  pl.BlockSpec((kvw, w), cur_t), pl.BlockSpec((kvw, w), prev_t)],
        out_specs=pl.BlockSpec((qw, w), cur_t),
        compiler_params=_cparams(("parallel",)), name="swa_prompt",
    )(sink, q, k, k, vt, vt)


def _swa_sample_kernel(sink_ref, q_ref, kn_ref, vn_ref, kb_ref, vb_ref, o_ref, ko_ref, vo_ref, kk_ref, vv_ref, *, bs, sl):
    w, hd = WINDOW, SWA_HEAD_DIM
    grp = SWA_HEADS // SWA_KV_HEADS
    nk = kk_ref.shape[0]
    scale = 1.0 / math.sqrt(hd)
    kk_ref[w + sl:nk, :] = jnp.zeros((nk - w - sl, kk_ref.shape[1]), F32)
    vv_ref[w + sl:nk, :] = jnp.zeros((nk - w - sl, vv_ref.shape[1]), F32)
    row = lax.broadcasted_iota(jnp.int32, (grp * sl, nk), 0)
    col = lax.broadcasted_iota(jnp.int32, (grp * sl, nk), 1)
    qi = row % sl
    mask = (col >= qi) & (col <= qi + w)
    rowc = lax.broadcasted_iota(jnp.int32, (grp * sl, 1), 0)

    def seq(b, carry):
        kb, kn = kb_ref[b], kn_ref[b]
        vb, vn = vb_ref[b], vn_ref[b]
        ko_ref[b, 0:w - sl, :] = kb_ref[b, sl:w, :]
        ko_ref[b, w - sl:w, :] = kn
        vo_ref[b, 0:w - sl, :] = vb_ref[b, sl:w, :]
        vo_ref[b, w - sl:w, :] = vn
        kk_ref[0:w, :] = kb
        kk_ref[w:w + sl, :] = kn
        vv_ref[0:w, :] = vb
        vv_ref[w:w + sl, :] = vn
        q = q_ref[b]
        for kh in range(SWA_KV_HEADS):
            ls = slice(kh * hd, (kh + 1) * hd)
            kkh = kk_ref[:, ls].astype(BF16)
            vvh = vv_ref[:, ls].astype(BF16)
            qs = jnp.concatenate([q[:, (kh * grp + g) * hd:(kh * grp + g + 1) * hd] for g in range(grp)], axis=0)
            s = jnp.where(mask, _dot_nt((qs * scale).astype(BF16), kkh), NEG)
            sk = jnp.zeros((grp * sl, 1), F32)
            for g in range(grp):
                sk = jnp.where(rowc // sl == g, sink_ref[kh * grp + g], sk)
            m = jnp.maximum(jnp.max(s, axis=1, keepdims=True), sk)
            p = jnp.exp(s - m)
            den = jnp.sum(p, axis=1, keepdims=True) + jnp.exp(sk - m)
            o = _dot(p.astype(BF16), vvh) / den
            for g in range(grp):
                h = kh * grp + g
                o_ref[b, :, h * hd:(h + 1) * hd] = o[g * sl:(g + 1) * sl, :]
        return carry

    lax.fori_loop(0, bs, seq, 0)


def swa_sample(q3, kn3, vn3, kbuf, vbuf, sink, bs=8):
    nseq, sl, qw = q3.shape
    w = WINDOW
    kvw = kn3.shape[-1]
    nk = 2 * w
    b3 = lambda shape: pl.BlockSpec(shape, lambda i: (i, 0, 0))
    return pl.pallas_call(
        functools.partial(_swa_sample_kernel, bs=bs, sl=sl),
        out_shape=[jax.ShapeDtypeStruct((nseq, sl, qw), F32),
                   jax.ShapeDtypeStruct((nseq, w, kvw), F32),
                   jax.ShapeDtypeStruct((nseq, w, kvw), F32)],
        grid=(nseq // bs,),
        in_specs=[pl.BlockSpec(memory_space=pltpu.SMEM),
                  b3((bs, sl, qw)), b3((bs, sl, kvw)), b3((bs, sl, kvw)), b3((bs, w, kvw)), b3((bs, w, kvw))],
        out_specs=[b3((bs, sl, qw)), b3((bs, w, kvw)), b3((bs, w, kvw))],
        scratch_shapes=[pltpu.VMEM((nk, kvw), F32), pltpu.VMEM((nk, kvw), F32)],
        compiler_params=_cparams(("arbitrary",)), name="swa_sample",
    )(sink, q3, kn3, vn3, kbuf, vbuf)


def _ffn_kernel(x_ref, a1_ref, w1_ref, a2_ref, w2_ref, g_ref, wg_ref, wu_ref, cw_ref, cb_ref, wd_ref, hist_ref, fg_ref,
                o_ref, hout_ref, ext_ref, act_ref, *, ns, tt, d, f, tf, final, a2_t):
    t = pl.program_id(1)
    hk = FFN_CONV - 1

    @pl.when(t == 0)
    def _():
        ext_ref[:, SUBLANES - hk:SUBLANES, :] = hist_ref[...]

    a1 = a1_ref[...].reshape(ns * tt, a1_ref.shape[-1]).astype(BF16)
    if a2_t:
        second = _dot_tn(a2_ref[...].astype(BF16), w2_ref[...])
    else:
        second = _dot(a2_ref[...].reshape(ns * tt, a2_ref.shape[-1]).astype(BF16), w2_ref[...])
    x = x_ref[...].reshape(ns * tt, d) + _dot(a1, w1_ref[...]) + second
    xn = _rms(x, g_ref[...]).astype(BF16)
    for c in range(f // tf):
        sl = slice(c * tf, (c + 1) * tf)
        g3 = _dot(xn, wg_ref[:, sl]).reshape(ns, tt, tf)
        ext_ref[:, SUBLANES:SUBLANES + tt, sl] = g3
        gc = cb_ref[:, sl] + g3 * cw_ref[hk:hk + 1, sl]
        for k in range(hk):
            gc = gc + ext_ref[:, SUBLANES - hk + k:SUBLANES - hk + k + tt, sl] * cw_ref[k:k + 1, sl]
        u = _dot(xn, wu_ref[:, sl])
        act_ref[:, sl] = (_silu(gc).reshape(ns * tt, tf) * u).astype(BF16)
    y = x + _dot(act_ref[...], wd_ref[...])
    if final:
        y = _rms(y, fg_ref[...])
    o_ref[...] = y.reshape(ns, tt, d)
    hout_ref[...] = ext_ref[:, SUBLANES + tt - hk:SUBLANES + tt, :]
    ext_ref[:, 0:SUBLANES, :] = ext_ref[:, tt:tt + SUBLANES, :]


def ffn(x3, a1, w1, a2, w2, a2_t, g, wg, wu, cw, cb, wd, hist, fg, ns, tt, final, tf=256):
    nseq, length, d = x3.shape
    f = wg.shape[1]
    hk = FFN_CONV - 1
    blk = lambda width: pl.BlockSpec((ns, tt, width), lambda s, t: (s, t, 0))
    hspec = pl.BlockSpec((ns, hk, f), lambda s, t: (s, 0, 0))
    if a2_t:
        assert nseq == 1
        a2_spec = pl.BlockSpec((a2.shape[0], tt), lambda s, t: (0, t))
    else:
        a2_spec = blk(a2.shape[-1])
    return pl.pallas_call(
        functools.partial(_ffn_kernel, ns=ns, tt=tt, d=d, f=f, tf=tf, final=final, a2_t=a2_t),
        out_shape=[jax.ShapeDtypeStruct((nseq, length, d), F32), jax.ShapeDtypeStruct((nseq, hk, f), F32)],
        grid=(nseq // ns, length // tt),
        in_specs=[blk(d), blk(a1.shape[-1]), _const_spec(w1.shape), a2_spec, _const_spec(w2.shape),
                  _const_spec((1, d)), _const_spec(wg.shape), _const_spec(wu.shape),
                  _const_spec((FFN_CONV, f)), _const_spec((1, f)), _const_spec(wd.shape), hspec, _const_spec((1, d))],
        out_specs=[blk(d), hspec],
        scratch_shapes=[pltpu.VMEM((ns, SUBLANES + tt, f), F32), pltpu.VMEM((ns * tt, f), BF16)],
        compiler_params=_cparams(("arbitrary", "arbitrary")), name="ffn",
    )(x3, a1, w1, a2, w2, g.reshape(1, d), wg, wu, cw, cb.reshape(1, f), wd, hist, fg.reshape(1, d))


def _ssd_kernel(z_ref, xbc_ref, dt_ref, dtt_ref, hist_ref, s0_ref, cw_ref, cb_ref, dtb_ref, dtbt_ref,
                alog_ref, alogt_ref, dvec_ref, nw_ref,
                y_ref, conv_ref, sout_ref, ext_ref, st_ref, ybuf_ref, dtp_ref, dttp_ref, *, tin, tc, inner, nst):
    c = pl.program_id(1)
    carried = tin == tc
    hk = CONV_W - 1
    nh, hp = SSD_HEADS, SSD_HEAD_DIM
    gh = nh // SSD_GROUPS

    @pl.when(c == 0)
    def _():
        if tin < tc:
            ext_ref[...] = jnp.zeros(ext_ref.shape, F32)
            dtp_ref[...] = jnp.zeros(dtp_ref.shape, F32)
            dttp_ref[...] = jnp.zeros(dttp_ref.shape, F32)
        ext_ref[SUBLANES - hk:SUBLANES, :] = hist_ref[0]
        if carried:
            for h in range(nh):
                st_ref[:, h * hp:(h + 1) * hp] = s0_ref[0, h].T

    ext_ref[SUBLANES:SUBLANES + tin, :] = xbc_ref[0]
    xc = cb_ref[...]
    for k in range(CONV_W):
        xc = xc + ext_ref[SUBLANES - hk + k:SUBLANES - hk + k + tc, :] * cw_ref[k:k + 1, :]
    conv_ref[0] = ext_ref[SUBLANES + tin - hk:SUBLANES + tin, :]
    if tin == tc:
        ext_ref[0:SUBLANES, :] = ext_ref[tc:tc + SUBLANES, :]
    xa = _silu(xc)
    xs = xa[:, :inner]
    bm = [xa[:, inner + g * nst:inner + (g + 1) * nst].astype(BF16) for g in range(SSD_GROUPS)]
    cm = [xa[:, inner + (SSD_GROUPS + g) * nst:inner + (SSD_GROUPS + g + 1) * nst].astype(BF16) for g in range(SSD_GROUPS)]

    if tin < tc:
        dtp_ref[0:tin, :] = dt_ref[0]
        dttp_ref[:, 0:tin] = dtt_ref[0]
        dt_raw, dtt_raw = dtp_ref[...], dttp_ref[...]
        valid_r = lax.broadcasted_iota(jnp.int32, (tc, nh), 0) < tin
        valid_c = lax.broadcasted_iota(jnp.int32, (nh, tc), 1) < tin
        dt = jnp.where(valid_r, _softplus(dt_raw + dtb_ref[...]), 0.0)
        dtt = jnp.where(valid_c, _softplus(dtt_raw + dtbt_ref[...]), 0.0)
    else:
        dt = _softplus(dt_ref[0] + dtb_ref[...])
        dtt = _softplus(dtt_ref[0] + dtbt_ref[...])
    da = dt * (-jnp.exp(alog_ref[...]))
    dat = dtt * (-jnp.exp(alogt_ref[...]))

    li = lax.broadcasted_iota(jnp.int32, (tc, tc), 0)
    si = lax.broadcasted_iota(jnp.int32, (tc, tc), 1)
    causal = li >= si
    tri = jnp.where(causal, 1.0, 0.0).astype(BF16)
    trit = jnp.where(li <= si, 1.0, 0.0).astype(BF16)
    cs = sum(_dot(tri, p) for p in _split3(da))
    cst = sum(_dot(p, trit) for p in _split3(dat))
    cb_g =[_dot_nt(cm[g], bm[g]) for g in range(SSD_GROUPS)]

    gwid = inner // SSD_GROUPS
    if carried:
        erow = lax.broadcasted_iota(jnp.int32, (nh, inner), 0)
        ehead = lax.broadcasted_iota(jnp.int32, (nh, inner), 1) // hp
        expand = jnp.where(erow == ehead, 1.0, 0.0).astype(BF16)
        dt_e = sum(_dot(p, expand) for p in _split3(dt))
        cs_e = sum(_dot(p, expand) for p in _split3(cs))
        last_e = cs_e[tc - 1:tc, :]
        xd = xs * dt_e
        xdb = xd.astype(BF16)
        xdd = (xd * jnp.exp(last_e - cs_e)).astype(BF16)
        y_off = []
        for g in range(SSD_GROUPS):
            gs = slice(g * gwid, (g + 1) * gwid)
            st_g = st_ref[:, gs]
            y_off.append(_dot(cm[g], st_g.astype(BF16)))
            st_ref[:, gs] = jnp.exp(last_e[:, gs]) * st_g + _dot_tn(bm[g], xdd[:, gs])
        for h in range(nh):
            ls = slice(h * hp, (h + 1) * hp)
            lmat = jnp.exp(jnp.where(causal, cs[:, h:h + 1] - cst[h:h + 1, :], NEG))
            ybuf_ref[:, ls] = _dot((cb_g[h // gh] * lmat).astype(BF16), xdb[:, ls])
        y_all = ybuf_ref[...] + jnp.exp(cs_e) * jnp.concatenate(y_off, axis=1) + dvec_ref[...] * xs

        @pl.when(c == pl.num_programs(1) - 1)
        def _():
            for h in range(nh):
                sout_ref[0, h] = st_ref[:, h * hp:(h + 1) * hp].T
    else:
        for h in range(nh):
            g = h // gh
            ls = slice(h * hp, (h + 1) * hp)
            col = cs[:, h:h + 1]
            last = cs[tc - 1:tc, h:h + 1]
            lmat = jnp.exp(jnp.where(causal, col - cst[h:h + 1, :], NEG))
            xs_h = xs[:, ls]
            xd = xs_h * dt[:, h:h + 1]
            s_h = s0_ref[0, h]
            yh = _dot((cb_g[g] * lmat).astype(BF16), xd.astype(BF16))
            yh = yh + jnp.exp(col) * _dot_nt(cm[g], s_h.astype(BF16))
            sout_ref[0, h] = jnp.exp(last) * s_h + _dot_tn((xd * jnp.exp(last - col)).astype(BF16), bm[g])
            ybuf_ref[:, ls] = yh + dvec_ref[:, ls] * xs_h
        y_all = ybuf_ref[...]

    y = y_all[0:tin, :] * _silu(z_ref[0])
    for g in range(SSD_GROUPS):
        gs = slice(g * gwid, (g + 1) * gwid)
        y_ref[0, :, gs] = _rms(y[:, gs], nw_ref[:, gs]).astype(y_ref.dtype)


def ssd(z3, xbc3, dt3, dtt3, hist, s0, cw, cb, dtb, alog, dvec, nw, tin, out_dtype):
    nseq, length, inner = z3.shape
    cd = xbc3.shape[-1]
    nh, hp, nst = s0.shape[1:]
    tc = SSD_CHUNK if tin == SSD_CHUNK else -(-tin // BF16_SUBLANES) * BF16_SUBLANES
    assert tin == SSD_CHUNK or tin == length
    hk = CONV_W - 1
    t3 = lambda width: pl.BlockSpec((1, tin, width), lambda s, c: (s, c, 0))
    per_seq = lambda shape: pl.BlockSpec((1,) + shape, lambda s, c: (s,) + (0,) * len(shape))
    return pl.pallas_call(
        functools.partial(_ssd_kernel, tin=tin, tc=tc, inner=inner, nst=nst),
        out_shape=[jax.ShapeDtypeStruct((nseq, length, inner), out_dtype),
                   jax.ShapeDtypeStruct((nseq, hk, cd), F32),
                   jax.ShapeDtypeStruct(s0.shape, F32)],
        grid=(nseq, length // tin),
        in_specs=[t3(inner), t3(cd), t3(nh),
                  pl.BlockSpec((1, nh, tin), lambda s, c: (s, 0, c)),
                  per_seq((hk, cd)), per_seq((nh, hp, nst)),
                  _const_spec((CONV_W, cd)), _const_spec((1, cd)), _const_spec((1, nh)), _const_spec((nh, 1)),
                  _const_spec((1, nh)), _const_spec((nh, 1)), _const_spec((1, inner)), _const_spec((1, inner))],
        out_specs=[t3(inner), per_seq((hk, cd)), per_seq((nh, hp, nst))],
        scratch_shapes=[pltpu.VMEM((SUBLANES + tc, cd), F32), pltpu.VMEM((nst, inner), F32),
                        pltpu.VMEM((tc, inner), F32), pltpu.VMEM((tc, nh), F32), pltpu.VMEM((nh, tc), F32)],
        compiler_params=_cparams(("arbitrary", "arbitrary")), name="ssd",
    )(z3, xbc3, dt3, dtt3, hist, s0, cw, cb.reshape(1, cd), dtb.reshape(1, nh), dtb.reshape(nh, 1),
      alog.reshape(1, nh), alog.reshape(nh, 1), dvec.reshape(1, inner), nw.reshape(1, inner))


def _rope128(t, ctab, stab):
    half = QK_ROPE // 2
    lane = lax.broadcasted_iota(jnp.int32, t.shape, 1)
    swapped = jnp.where(lane < QK_NOPE + half, pltpu.roll(t, LANES - half, axis=1), pltpu.roll(t, half, axis=1))
    return t * ctab + swapped * stab


def _mla_prep_kernel(cq_ref, ckv_ref, kr_ref, ct_ref, st_ref, qg_ref, kg_ref, wuq_ref, wk_ref, wv_ref,
                     lat_ref, rope_ref, *outs, sample):
    scale = 1.0 / math.sqrt(QK_NOPE + QK_ROPE)
    ctab, stab = ct_ref[...], st_ref[...]
    qf = _dot(_rms(cq_ref[...], qg_ref[...]).astype(BF16), wuq_ref[...])
    ckv = _rms(ckv_ref[...], kg_ref[...])
    lat_ref[...] = ckv
    krr = _rope128(kr_ref[...], ctab, stab)
    rope_ref[...] = krr[:, QK_NOPE:QK_NOPE + QK_ROPE]
    if sample:
        qlat_ref, qr_ref = outs
        for h in range(MLA_HEADS):
            qh = _rope128(qf[:, h * LANES:(h + 1) * LANES], ctab, stab) * scale
            qlat_ref[h] = _dot(qh[:, :QK_NOPE].astype(BF16), wk_ref[h])
            qr_ref[h] = qh[:, QK_NOPE:QK_NOPE + QK_ROPE]
    else:
        qc_ref, kc_ref, vt_ref = outs
        ckvb = ckv.astype(BF16)
        vrows = vt_ref.shape[1]
        ones_row = jnp.where(lax.broadcasted_iota(jnp.int32, (vrows, ckvb.shape[0]), 0) == V_HEAD, 1.0, 0.0)
        for h in range(MLA_HEADS):
            qh = _rope128(qf[:, h * LANES:(h + 1) * LANES], ctab, stab) * (scale * LOG2E)
            qc_ref[h] = qh.astype(BF16)
            kc_ref[h] = (_dot(ckvb, wk_ref[h]) + krr).astype(BF16)
            vt_ref[h] = (_dot_nt(wv_ref[h], ckvb) + ones_row).astype(BF16)


def mla_prep(cq, ckv_raw, krp, ctab, stab, qg, kg, wuq_pad, wk, wv, sample, tm=512):
    m = cq.shape[0]
    tm = min(tm, m)
    nh = MLA_HEADS
    row = lambda width: pl.BlockSpec((tm, width), lambda i: (i, 0))
    hrow = lambda width: pl.BlockSpec((nh, tm, width), lambda i: (0, i, 0))
    lq, lkv = cq.shape[1], ckv_raw.shape[1]
    out_shape = [jax.ShapeDtypeStruct((m, lkv), F32), jax.ShapeDtypeStruct((m, QK_ROPE), F32)]
    out_specs = [row(lkv), row(QK_ROPE)]
    if sample:
        out_shape += [jax.ShapeDtypeStruct((nh, m, lkv), F32), jax.ShapeDtypeStruct((nh, m, QK_ROPE), F32)]
        out_specs += [hrow(lkv), hrow(QK_ROPE)]
    else:
        out_shape += [jax.ShapeDtypeStruct((nh, m, LANES), BF16)] * 2 + [jax.ShapeDtypeStruct((nh, VT_ROWS, m), BF16)]
        out_specs += [hrow(LANES)] * 2 + [pl.BlockSpec((nh, VT_ROWS, tm), lambda i: (0, 0, i))]
    return pl.pallas_call(
        functools.partial(_mla_prep_kernel, sample=sample),
        out_shape=out_shape, grid=(m // tm,),
        in_specs=[row(lq), row(lkv), row(LANES), row(LANES), row(LANES), _const_spec((1, lq)), _const_spec((1, lkv)),
                  _const_spec(wuq_pad.shape), _const_spec(wk.shape), _const_spec(wv.shape)],
        out_specs=out_specs,
        compiler_params=_cparams(("parallel",)), name="mla_prep",
    )(cq, ckv_raw, krp, ctab, stab, qg.reshape(1, lq), kg.reshape(1, lkv), wuq_pad, wk, wv)


def _mla_flash_kernel(qi_ref, kj_ref, q_ref, k_ref, vt_ref, o_ref, m_ref, acc_ref, *, t, nq):
    n = pl.program_id(0)
    i, j = qi_ref[n], kj_ref[n]

    @pl.when(j == 0)
    def _():
        m_ref[...] = jnp.full(m_ref.shape, NEG, F32)
        acc_ref[...] = jnp.zeros(acc_ref.shape, F32)

    def tile(u, masked):
        qs = slice(u * t, (u + 1) * t)
        if masked:
            keep = lax.broadcasted_iota(jnp.int32, (t, t), 0) <= lax.broadcasted_iota(jnp.int32, (t, t), 1)
        nxt = _dot_nt(k_ref[0], q_ref[0, qs, :])
        pending = None
        for h in range(MLA_HEADS + 1):
            if h < MLA_HEADS:
                st = nxt
                if h + 1 < MLA_HEADS:
                    nxt = _dot_nt(k_ref[h + 1], q_ref[h + 1, qs, :])
                if masked:
                    st = jnp.where(keep, st, NEG)
                m_prev = m_ref[h, :, qs]
                m_new = jnp.maximum(m_prev, jnp.max(st, axis=0, keepdims=True))
                alpha = jnp.exp2(m_prev - m_new)
                p = jnp.exp2(st - m_new).astype(BF16)
                m_ref[h, :, qs] = m_new
            if pending is not None:
                hp, alpha_p, p_p = pending
                acc_ref[hp, :, qs] = alpha_p * acc_ref[hp, :, qs] + _dot(vt_ref[hp], p_p)
            pending = (h, alpha, p) if h < MLA_HEADS else None

    for u in range(nq):
        qt = nq * i + u

        @pl.when(j < qt)
        def _():
            tile(u, False)

        @pl.when(j == qt)
        def _():
            tile(u, True)
            for h in range(MLA_HEADS):
                a = acc_ref[h, :, u * t:(u + 1) * t]
                o_ref[h * V_HEAD:(h + 1) * V_HEAD, u * t:(u + 1) * t] = (
                    a[0:V_HEAD, :] / a[V_HEAD:V_HEAD + 1, :]).astype(o_ref.dtype)


def mla_flash(qc, kc, vt, t=512, nq=2):
    nh, length, _ = qc.shape
    t = min(t, length)
    nq = min(nq, length // t)
    nb = length // (t * nq)
    qi = [i for i in range(nb) for _ in range(nq * i + nq)]
    kj = [j for i in range(nb) for j in range(nq * i + nq)]
    grid_spec = pltpu.PrefetchScalarGridSpec(
        num_scalar_prefetch=2, grid=(len(qi),),
        in_specs=[pl.BlockSpec((nh, nq * t, LANES), lambda n, qi, kj: (0, qi[n], 0)),
                  pl.BlockSpec((nh, t, LANES), lambda n, qi, kj: (0, kj[n], 0)),
                  pl.BlockSpec((nh, VT_ROWS, t), lambda n, qi, kj: (0, 0, kj[n]))],
        out_specs=pl.BlockSpec((nh * V_HEAD, nq * t), lambda n, qi, kj: (0, qi[n])),
        scratch_shapes=[pltpu.VMEM((nh, 1, nq * t), F32), pltpu.VMEM((nh, VT_ROWS, nq * t), F32)])
    return pl.pallas_call(
        functools.partial(_mla_flash_kernel, t=t, nq=nq),
        out_shape=jax.ShapeDtypeStruct((nh * V_HEAD, length), BF16),
        grid_spec=grid_spec,
        compiler_params=_cparams(("arbitrary",)), name="mla_flash",
    )(jnp.asarray(qi, jnp.int32), jnp.asarray(kj, jnp.int32), qc, kc, vt)


def _mla_sample_kernel(pt_ref, qlat_ref, qr_ref, ckv_ref, kr_ref, wuv_ref, lat_hbm, rope_hbm,
                       o_ref, latbuf, ropebuf, sem, kbuf_ref, sbuf_ref, kl_ref, krl_ref,
                       *, cp, sl, npages, li):
    b, nb = pl.program_id(0), pl.num_programs(0)
    rows = MLA_HEADS * sl
    psz = latbuf.shape[2]
    pair = 2 * psz
    nch = npages // cp

    def chunk_copies(seq, ch, slot):
        out = []
        for c in range(cp):
            pid = pt_ref[seq, ch * cp + c]
            out.append(pltpu.make_async_copy(lat_hbm.at[li, pid], latbuf.at[slot, c], sem.at[0, slot]))
            out.append(pltpu.make_async_copy(rope_hbm.at[li, pid], ropebuf.at[slot, c], sem.at[1, slot]))
        return out

    def start(seq, ch, slot):
        for n, cpy in enumerate(chunk_copies(seq, ch, slot)):
            cpy.start(priority=(n // 2) % 2)

    def wait(seq, ch, slot):
        for cpy in chunk_copies(seq, ch, slot):
            cpy.wait()

    nslots = latbuf.shape[0]
    ahead = nslots - 1

    @pl.when(b == 0)
    def _():
        for ch in range(ahead):
            start(0, ch, ch % nslots)

    q = qlat_ref[...].reshape(rows, qlat_ref.shape[-1]).astype(BF16)
    qr = qr_ref[...].reshape(rows, qr_ref.shape[-1]).astype(BF16)

    mrun = None
    for ch in range(nch):
        slot = ch % nslots
        wait(b, ch, slot)
        nxt = ch + ahead
        if nxt < nch:
            start(b, nxt, nxt % nslots)
        else:
            @pl.when(b + 1 < nb)
            def _():
                start(b + 1, nxt - nch, nxt % nslots)
        for c2 in range(cp // 2):
            base = (ch * cp + 2 * c2) * psz
            kk = jnp.concatenate([latbuf[slot, 2 * c2], latbuf[slot, 2 * c2 + 1]], axis=0).astype(BF16)
            kbuf_ref[base:base + pair, :] = kk
            rt = jnp.concatenate([ropebuf[slot, 2 * c2], ropebuf[slot, 2 * c2 + 1]], axis=1).astype(BF16)
            s = _dot_nt(q, kk) + _dot(qr, rt)
            sbuf_ref[:, base:base + pair] = s
            mrun = s if mrun is None else jnp.maximum(mrun, s)

    kl_ref[...] = jnp.zeros(kl_ref.shape, F32)
    krl_ref[...] = jnp.zeros(krl_ref.shape, F32)
    kl_ref[0:sl, :] = ckv_ref[0]
    krl_ref[0:sl, :] = kr_ref[0]
    kl = kl_ref[...].astype(BF16)
    kbuf_ref[npages * psz:npages * psz + pair, :] = kl
    s = _dot_nt(q, kl) + _dot_nt(qr, krl_ref[...].astype(BF16))
    row = lax.broadcasted_iota(jnp.int32, s.shape, 0)
    col = lax.broadcasted_iota(jnp.int32, s.shape, 1)
    s = jnp.where((col < sl) & (col <= row % sl), s, NEG)
    sbuf_ref[:, npages * psz:npages * psz + pair] = s

    m = jnp.broadcast_to(jnp.max(jnp.maximum(mrun, s), axis=1, keepdims=True), s.shape)
    nblk = sbuf_ref.shape[1] // pair
    hw = kbuf_ref.shape[1] // 2
    lacc = jnp.zeros(s.shape, F32)
    accs = [jnp.zeros((rows, hw), F32), jnp.zeros((rows, hw), F32)]
    for blk in range(nblk):
        cols = slice(blk * pair, (blk + 1) * pair)
        p = jnp.exp(sbuf_ref[:, cols] - m)
        lacc = lacc + p
        pb = p.astype(BF16)
        accs[0] = accs[0] + _dot(pb, kbuf_ref[cols, 0:hw])
        accs[1] = accs[1] + _dot(pb, kbuf_ref[cols, hw:2 * hw])
    lsum = jnp.sum(lacc, axis=1, keepdims=True)
    accn = (jnp.concatenate(accs, axis=1) / lsum).astype(BF16)
    for h in range(MLA_HEADS):
        o_ref[0, :, h * V_HEAD:(h + 1) * V_HEAD] = _dot(accn[h * sl:(h + 1) * sl, :], wuv_ref[h])


def mla_sample(qlat, qr, ckv3, kr3, wuv, cache_lat, cache_rope_t, page_table, li, cp=32):
    nh, m, lkv = qlat.shape
    nseq, sl, rd = kr3.shape
    npages = page_table.shape[1]
    nslots = MLA_PAGE_SLOTS
    cp = min(cp, npages // nslots)
    assert npages % (nslots * cp) == 0 and cp % 2 == 0
    psz = cache_lat.shape[2]
    nkeys = npages * psz + 2 * psz
    rows = nh * sl
    in_specs = [pl.BlockSpec((nh, sl, lkv), lambda b, pt: (0, b, 0)),
                pl.BlockSpec((nh, sl, rd), lambda b, pt: (0, b, 0)),
                pl.BlockSpec((1, sl, lkv), lambda b, pt: (b, 0, 0)),
                pl.BlockSpec((1, sl, rd), lambda b, pt: (b, 0, 0)),
                pl.BlockSpec(wuv.shape, lambda b, pt: (0, 0, 0)),
                pl.BlockSpec(memory_space=pl.ANY), pl.BlockSpec(memory_space=pl.ANY)]
    grid_spec = pltpu.PrefetchScalarGridSpec(
        num_scalar_prefetch=1, grid=(nseq,), in_specs=in_specs,
        out_specs=pl.BlockSpec((1, sl, nh * V_HEAD), lambda b, pt: (b, 0, 0)),
        scratch_shapes=[pltpu.VMEM((nslots, cp, psz, lkv), F32), pltpu.VMEM((nslots, cp, rd, psz), F32),
                        pltpu.SemaphoreType.DMA((2, nslots)),
                        pltpu.VMEM((nkeys, lkv), BF16), pltpu.VMEM((rows, nkeys), F32),
                        pltpu.VMEM((2 * psz, lkv), F32), pltpu.VMEM((2 * psz, rd), F32)])
    return pl.pallas_call(
        functools.partial(_mla_sample_kernel, cp=cp, sl=sl, npages=npages, li=li),
        out_shape=jax.ShapeDtypeStruct((nseq, sl, nh * V_HEAD), F32),
        grid_spec=grid_spec,
        compiler_params=_cparams(("arbitrary",)), name="mla_sample",
    )(page_table, qlat, qr, ckv3, kr3, wuv, cache_lat, cache_rope_t)


def _block_diag(w, per):
    nh, b, _ = w.shape
    w4 = w.reshape(nh // per, per, b, b)
    eye = jnp.eye(per, dtype=w.dtype)
    return jnp.einsum("gaij,ab->gaibj", w4, eye).reshape(nh // per, per * b, per * b)


def _rope_tables(pos):
    half = QK_ROPE // 2
    inv = jnp.exp(-(math.log(ROPE_THETA) / half) * jnp.arange(half, dtype=F32))
    ang = pos.astype(F32)[:, None] * inv[None, :]
    cos, sin = jnp.cos(ang), jnp.sin(ang)
    n = pos.shape[0]
    pad = LANES - QK_NOPE - QK_ROPE
    ctab = jnp.concatenate([jnp.ones((n, QK_NOPE), F32), cos, cos, jnp.ones((n, pad), F32)], axis=1)
    stab = jnp.concatenate([jnp.zeros((n, QK_NOPE), F32), -sin, sin, jnp.zeros((n, pad), F32)], axis=1)
    return ctab, stab


def _prep_weights(P):
    w = {}
    lw = P["lru_conv_w"].shape[-1]
    qw = SWA_HEADS * SWA_HEAD_DIM
    kvw = SWA_KV_HEADS * SWA_HEAD_DIM
    ew = P["even_w_in"][0].astype(BF16)
    b = [0, lw, 2 * lw, 2 * lw + qw, 2 * lw + qw + kvw, 2 * lw + qw + 2 * kvw]
    w["even_in"] = [ew[:, b[i]:b[i + 1]] for i in range(5)]
    per = 256 // (lw // LRU_HEADS)
    w["wa_bd"] = _block_diag(P["lru_wa"][0], per).astype(BF16)
    w["wx_bd"] = _block_diag(P["lru_wx"][0], per).astype(BF16)
    eo = P["even_w_out"][0].astype(BF16)
    w["even_out"] = (eo[:lw], eo[lw:])

    inner = P["ssd_norm"].shape[-1]
    cd = P["ssd_conv_w"].shape[-1]
    lq = P["mla_q_norm"].shape[-1]
    lkv = P["mla_kv_norm"].shape[-1]
    ow = P["odd_w_in"][0]
    b = [0, inner, inner + cd, inner + cd + SSD_HEADS, inner + cd + SSD_HEADS + lq,
         inner + cd + SSD_HEADS + lq + lkv, inner + cd + SSD_HEADS + lq + lkv + QK_ROPE]
    parts = [ow[:, b[i]:b[i + 1]] for i in range(6)]
    pad = LANES - QK_NOPE - QK_ROPE
    kr_pad = jnp.pad(parts[5], ((0, 0), (QK_NOPE, pad)))
    w["odd_in"] = [parts[0].astype(BF16), parts[1].astype(BF16), parts[2].astype(BF16), parts[2].T.astype(BF16),
                   parts[3].astype(BF16), parts[4].astype(BF16), kr_pad.astype(BF16)]
    uq = P["mla_w_uq"][0].reshape(lq, MLA_HEADS, QK_NOPE + QK_ROPE)
    w["wuq_pad"] = jnp.pad(uq, ((0, 0), (0, 0), (0, pad))).reshape(lq, MLA_HEADS * LANES).astype(BF16)
    uk = P["mla_w_uk"][0]
    uv = P["mla_w_uv"][0]
    w["wuk_pad"] = jnp.pad(jnp.transpose(uk, (1, 0, 2)), ((0, 0), (0, 0), (0, LANES - QK_NOPE))).astype(BF16)
    w["wuk_t"] = jnp.transpose(uk, (1, 2, 0)).astype(BF16)
    w["wuv_t_pad"] = jnp.pad(jnp.transpose(uv, (1, 2, 0)), ((0, 0), (0, VT_ROWS - V_HEAD), (0, 0))).astype(BF16)
    w["wuv"] = jnp.transpose(uv, (1, 0, 2)).astype(BF16)
    oo = P["odd_w_out"][0].astype(BF16)
    w["odd_out"] = (oo[:inner], oo[inner:])
    w["dvec"] = jnp.repeat(P["ssd_d"][0], SSD_HEAD_DIM)
    w["ffn"] = [(P["ffn_w_gate"][l].astype(BF16), P["ffn_w_up"][l].astype(BF16), P["ffn_w_down"][l].astype(BF16))
                for l in range(P["ffn_w_gate"].shape[0])]
    return w


def _trunk(x3, pos, P, W, st, sample):
    nseq, length, d = x3.shape
    m = nseq * length
    lw = P["lru_conv_w"].shape[-1]
    kvw = SWA_KV_HEADS * SWA_HEAD_DIM
    if sample:
        seq_tile, time_tile = min(nseq, 64), length
    else:
        seq_tile, time_tile = 1, min(length, 512)
    mix_dtype = F32 if sample else BF16
    out = {}

    if sample:
        xr, gate, q, k, v = norm_proj(x3.reshape(m, d), P["mix_norm"][0], W["even_in"], [False] * 5)
    else:
        xr, gate, q, k, v, vt = norm_proj(x3.reshape(m, d), P["mix_norm"][0], W["even_in"] + [W["even_in"][4].T],
                                          [False] * 5 + [True])
    lru_ns, lru_tt = (min(nseq, 32), length) if sample else (1, min(length, 256))
    rec, out["lru_conv"], h_last = lru(
        xr.reshape(nseq, length, lw), gate.reshape(nseq, length, lw), st["lru_conv"], st["lru_h"].reshape(nseq, 1, lw),
        P["lru_conv_w"][0], P["lru_conv_b"][0], W["wa_bd"], P["lru_ba"][0], W["wx_bd"], P["lru_bx"][0],
        P["lru_lambda"][0], lru_ns, lru_tt, mix_dtype)
    out["lru_h"] = h_last.reshape(nseq, lw)
    if sample:
        att, sk, sv = swa_sample(q.reshape(nseq, length, -1), k.reshape(nseq, length, kvw), v.reshape(nseq, length, kvw),
                                 st["swa_k"].reshape(nseq, WINDOW, kvw), st["swa_v"].reshape(nseq, WINDOW, kvw),
                                 P["swa_sink"][0], bs=min(nseq, 8))
        out["swa_k"] = sk.reshape(nseq, WINDOW, SWA_KV_HEADS, SWA_HEAD_DIM)
        out["swa_v"] = sv.reshape(nseq, WINDOW, SWA_KV_HEADS, SWA_HEAD_DIM)
    else:
        att = swa_prompt(q, k, vt, P["swa_sink"][0])
        out["swa_k"] = k[-WINDOW:].reshape(1, WINDOW, SWA_KV_HEADS, SWA_HEAD_DIM)
        out["swa_v"] = v[-WINDOW:].reshape(1, WINDOW, SWA_KV_HEADS, SWA_HEAD_DIM)
    wg, wu, wd = W["ffn"][0]
    x3, fc0 = ffn(x3, rec, W["even_out"][0], att, W["even_out"][1], not sample,
                  P["ffn_norm"][0], wg, wu, P["ffn_conv_w"][0], P["ffn_conv_b"][0], wd,
                  st["ffn_conv"][0], P["final_norm"], seq_tile, time_tile, final=False)

    z, xbc, dt, dtt, cq, ckv_raw, krp = norm_proj(x3.reshape(m, d), P["mix_norm"][1], W["odd_in"],
                                                   [False, False, False, True, False, False, False])
    inner = z.shape[-1]
    cd = xbc.shape[-1]
    dtt3 = jnp.transpose(dtt.reshape(SSD_HEADS, nseq, length), (1, 0, 2))
    y, out["ssd_conv"], out["ssd"] = ssd(
        z.reshape(nseq, length, inner), xbc.reshape(nseq, length, cd), dt.reshape(nseq, length, SSD_HEADS), dtt3,
        st["ssd_conv"], st["ssd"], P["ssd_conv_w"][0], P["ssd_conv_b"][0], P["ssd_dt_bias"][0], P["ssd_a_log"][0],
        W["dvec"], P["ssd_norm"][0], tin=min(length, SSD_CHUNK), out_dtype=mix_dtype)
    ctab, stab = _rope_tables(pos)
    if sample:
        lat, rp, qlat, qr = mla_prep(cq, ckv_raw, krp, ctab, stab, P["mla_q_norm"][0], P["mla_kv_norm"][0],
                                     W["wuq_pad"], W["wuk_t"], W["wuv"], sample=True)
        att = mla_sample(qlat, qr, lat.reshape(nseq, length, -1), rp.reshape(nseq, length, -1), W["wuv"],
                         st["mla_latent"], jnp.swapaxes(st["mla_rope"], 2, 3), st["page_table"], 0)
    else:
        lat, rp, qc, kc, vt = mla_prep(cq, ckv_raw, krp, ctab, stab, P["mla_q_norm"][0], P["mla_kv_norm"][0],
                                       W["wuq_pad"], W["wuk_pad"], W["wuv_t_pad"], sample=False)
        att = mla_flash(qc, kc, vt)
    out["mla_latent"] = lat.reshape(nseq, length, -1)
    out["mla_rope"] = rp.reshape(nseq, length, -1)
    wg, wu, wd = W["ffn"][1]
    y3, fc1 = ffn(x3, y, W["odd_out"][0], att, W["odd_out"][1], not sample,
                  P["ffn_norm"][1], wg, wu, P["ffn_conv_w"][1], P["ffn_conv_b"][1], wd,
                  st["ffn_conv"][1], P["final_norm"], seq_tile, time_tile, final=True)
    out["ffn_conv"] = jnp.stack([fc0, fc1])
    return y3, out


def kernel(x_prompt, x_sample, state_lru_conv, state_lru_h, cache_swa_k, cache_swa_v, state_ssd_conv, state_ssd,
           cache_mla_latent, cache_mla_rope, page_table, state_ffn_conv, mix_norm, ffn_norm, final_norm, even_w_in,
           lru_conv_w, lru_conv_b, lru_wa, lru_ba, lru_wx, lru_bx, lru_lambda, swa_sink, even_w_out, odd_w_in,
           ssd_conv_w, ssd_conv_b, ssd_dt_bias, ssd_a_log, ssd_d, ssd_norm, mla_q_norm, mla_w_uq, mla_kv_norm,
           mla_w_uk, mla_w_uv, odd_w_out, ffn_w_gate, ffn_w_up, ffn_conv_w, ffn_conv_b, ffn_w_down):
    P = dict(mix_norm=mix_norm, ffn_norm=ffn_norm, final_norm=final_norm, even_w_in=even_w_in,
             lru_conv_w=lru_conv_w, lru_conv_b=lru_conv_b, lru_wa=lru_wa, lru_ba=lru_ba, lru_wx=lru_wx,
             lru_bx=lru_bx, lru_lambda=lru_lambda, swa_sink=swa_sink, even_w_out=even_w_out,
             odd_w_in=odd_w_in, ssd_conv_w=ssd_conv_w, ssd_conv_b=ssd_conv_b, ssd_dt_bias=ssd_dt_bias,
             ssd_a_log=ssd_a_log, ssd_d=ssd_d, ssd_norm=ssd_norm, mla_q_norm=mla_q_norm,
             mla_w_uq=mla_w_uq, mla_kv_norm=mla_kv_norm, mla_w_uk=mla_w_uk, mla_w_uv=mla_w_uv,
             odd_w_out=odd_w_out, ffn_w_gate=ffn_w_gate, ffn_w_up=ffn_w_up, ffn_conv_w=ffn_conv_w,
             ffn_conv_b=ffn_conv_b, ffn_w_down=ffn_w_down)
    W = _prep_weights(P)
    bp, lp, d = x_prompt.shape
    bs, ls, _ = x_sample.shape
    depth = ffn_w_gate.shape[0]
    lw = lru_conv_w.shape[-1]
    cd = ssd_conv_w.shape[-1]
    f = ffn_w_gate.shape[-1]
    kvw = SWA_KV_HEADS * SWA_HEAD_DIM
    past_len = page_table.shape[1] * PAGE_SIZE

    st_p = dict(lru_conv=jnp.zeros((bp, CONV_W - 1, lw), F32), lru_h=jnp.zeros((bp, lw), F32),
                ssd_conv=jnp.zeros((bp, CONV_W - 1, cd), F32),
                ssd=jnp.zeros((bp, SSD_HEADS, SSD_HEAD_DIM, SSD_STATE), F32),
                ffn_conv=jnp.zeros((depth, bp, FFN_CONV - 1, f), F32))
    st_s = dict(lru_conv=state_lru_conv[0], lru_h=state_lru_h[0], swa_k=cache_swa_k[0], swa_v=cache_swa_v[0],
                ssd_conv=state_ssd_conv[0], ssd=state_ssd[0], mla_latent=cache_mla_latent, mla_rope=cache_mla_rope,
                page_table=page_table, ffn_conv=state_ffn_conv)
    pos_p = jnp.tile(jnp.arange(lp), bp)
    pos_s = jnp.tile(past_len + jnp.arange(ls), bs)
    y_p, sp = _trunk(x_prompt, pos_p, P, W, st_p, False)
    y_s, ss = _trunk(x_sample, pos_s, P, W, st_s, True)
    e = lambda a: a[None]
    return (y_p, y_s,
            e(sp["lru_conv"]), e(ss["lru_conv"]), e(sp["lru_h"]), e(ss["lru_h"]),
            e(sp["swa_k"]), e(ss["swa_k"]), e(sp["swa_v"]), e(ss["swa_v"]),
            e(sp["ssd_conv"]), e(ss["ssd_conv"]), e(sp["ssd"]), e(ss["ssd"]),
            e(sp["mla_latent"]), e(ss["mla_latent"]), e(sp["mla_rope"]), e(ss["mla_rope"]),
            sp["ffn_conv"], ss["ffn_conv"])
```
